```python
import jax, jax.numpy as jnp
from jax import lax
import numpy as np

D_MODEL = 1024
BATCH = 4
SEQ = 8192
DEPTH = 4

ATTN_HEADS = 8
ATTN_KV_HEADS = 2
ATTN_HEAD_DIM = 64
ATTN_GROUP = ATTN_HEADS // ATTN_KV_HEADS
ATTN_WIDTH = ATTN_HEADS * ATTN_HEAD_DIM
KV_WIDTH = ATTN_KV_HEADS * ATTN_HEAD_DIM
WINDOW = 128
ATTN_BLOCK = WINDOW
ROPE_THETA = 500000.0
ROPE_DIM = ATTN_HEAD_DIM // 4
REC_HEADS = 4
REC_KEY_DIM = 128
REC_VAL_DIM = 128
REC_KEY_WIDTH = REC_HEADS * REC_KEY_DIM
REC_WIDTH = REC_HEADS * REC_VAL_DIM
REC_CHUNK = 64
MIX_WIDTH = ATTN_WIDTH + REC_WIDTH
SPLIT_SIZES = (ATTN_WIDTH, KV_WIDTH, KV_WIDTH, REC_KEY_WIDTH, REC_KEY_WIDTH, REC_WIDTH, REC_WIDTH)
IN_WIDTH = 2816
N_GROUPS = 4
EXPERTS_PER_GROUP = 8
N_EXPERTS = N_GROUPS * EXPERTS_PER_GROUP
TOP_K = 2
EXPERT_FF = 512
MOE_BLOCK = 128
PLE_DIM = 256
RMS_EPS = 1e-6
MASK_VALUE = -1e30

kernel_name = 'hymba_swa_hgrn2_hiermoe_ple'

F32 = jnp.float32


def rms_norm(x, gain):
    xf = x.astype(F32)
    y = xf * lax.rsqrt(jnp.mean(xf * xf, axis=-1, keepdims=True) + RMS_EPS)
    return (y * gain.astype(F32)).astype(x.dtype)


def apply_partial_rope(x, cos, sin):
    half = ROPE_DIM // 2
    xf = x.astype(F32)
    x1 = xf[..., :half]
    x2 = xf[..., half:ROPE_DIM]
    out = jnp.concatenate([x1 * cos - x2 * sin, x2 * cos + x1 * sin, xf[..., ROPE_DIM:]], axis=-1)
    return out.astype(x.dtype)


def sliding_window_attention(q, k, v, sinks):
    B, S = q.shape[0], q.shape[1]
    nb = S // ATTN_BLOCK
    qb = q.reshape(B, nb, ATTN_BLOCK, ATTN_KV_HEADS, ATTN_GROUP, ATTN_HEAD_DIM)

    def with_prev_block(t):
        tb = t.reshape(B, nb, ATTN_BLOCK, ATTN_KV_HEADS, ATTN_HEAD_DIM)
        prev = jnp.pad(tb[:, :-1], ((0, 0), (1, 0), (0, 0), (0, 0), (0, 0)))
        return jnp.concatenate([prev, tb], axis=2)

    kc = with_prev_block(k)
    vc = with_prev_block(v)
    scores = jnp.einsum('bnqhgd,bnkhd->bnhgqk', qb, kc).astype(F32) * (ATTN_HEAD_DIM ** -0.5)
    qi = jnp.arange(ATTN_BLOCK)[:, None]
    kj = jnp.arange(2 * ATTN_BLOCK)[None, :]
    dist = ATTN_BLOCK + qi - kj
    in_window = (dist >= 0) & (dist < WINDOW)
    key_exists = (jnp.arange(nb) > 0)[:, None, None] | (kj >= ATTN_BLOCK)[None]
    mask = (in_window[None] & key_exists)[None, :, None, None]
    scores = jnp.where(mask, scores, MASK_VALUE)
    sink = sinks.astype(F32).reshape(1, 1, ATTN_KV_HEADS, ATTN_GROUP, 1, 1)
    m = jnp.maximum(jnp.max(scores, axis=-1, keepdims=True), sink)
    e = jnp.exp(scores - m)
    denom = jnp.sum(e, axis=-1, keepdims=True) + jnp.exp(sink - m)
    probs = (e / denom).astype(v.dtype)
    out = jnp.einsum('bnhgqk,bnkhd->bnqhgd', probs, vc)
    return out.reshape(B, S, ATTN_WIDTH)


def hgrn2(q, f, i, og, lb, out_gain):
    B, S = q.shape[0], q.shape[1]
    nc = S // REC_CHUNK
    fz = f.astype(F32)
    lbf = lb.astype(F32)
    forget = lbf + (1.0 - lbf) * jax.nn.sigmoid(fz)
    log_f = jnp.log(forget)
    key = (1.0 - lbf) * jax.nn.sigmoid(-fz)
    qf = jax.nn.silu(q.astype(F32))
    vf = i.astype(F32)

    def chunked(t, d):
        return t.reshape(B, nc, REC_CHUNK, REC_HEADS, d).transpose(1, 0, 3, 2, 4)

    qc = chunked(qf, REC_KEY_DIM)
    kc = chunked(key, REC_KEY_DIM)
    vc = chunked(vf, REC_VAL_DIM)
    bc = jnp.cumsum(chunked(log_f, REC_KEY_DIM), axis=3)
    causal = jnp.tril(jnp.ones((REC_CHUNK, REC_CHUNK), dtype=bool))[:, :, None]

    def step(state, xs):
        qt, kt, vt, bt = xs
        rel = jnp.where(causal, bt[:, :, :, None, :] - bt[:, :, None, :, :], MASK_VALUE)
        a = jnp.einsum('bhtd,bhsd,bhtsd->bhts', qt, kt, jnp.exp(rel))
        o = (jnp.einsum('bhts,bhse->bhte', a, vt)
             + jnp.einsum('bhtd,bhde->bhte', qt * jnp.exp(bt), state))
        b_last = bt[:, :, -1, :]
        state = (state * jnp.exp(b_last)[..., None]
                 + jnp.einsum('bhsd,bhse->bhde', kt * jnp.exp(b_last[:, :, None, :] - bt), vt))
        return state, o

    s0 = jnp.zeros((B, REC_HEADS, REC_KEY_DIM, REC_VAL_DIM), F32)
    _, o = lax.scan(step, s0, (qc, kc, vc, bc))
    o = o.transpose(1, 0, 3, 2, 4).reshape(B, S, REC_HEADS, REC_VAL_DIM)
    o = rms_norm(o, out_gain).reshape(B, S, REC_WIDTH)
    return (o * jax.nn.silu(og.astype(F32))).astype(og.dtype)


def hierarchical_moe(xn, w_router_group, w_router_expert, w_gate, w_up, w_down):
    B, S, D = xn.shape
    T = B * S
    A = T * TOP_K
    xt = xn.reshape(T, D)
    g_logits = (xt @ w_router_group).astype(F32)
    g_prob = jax.nn.softmax(g_logits, axis=-1)
    g_val, g_idx = lax.top_k(g_logits, 1)
    p_group = jnp.take_along_axis(g_prob, g_idx, axis=-1)
    e_logits = (xt @ w_router_expert).astype(F32).reshape(T, N_GROUPS, EXPERTS_PER_GROUP)
    e_logits = jnp.take_along_axis(e_logits, g_idx[:, :, None], axis=1)[:, 0]
    top_val, top_idx = lax.top_k(e_logits, TOP_K)
    weights = p_group * jax.nn.softmax(top_val, axis=-1)
    expert = g_idx * EXPERTS_PER_GROUP + top_idx

    e_flat = expert.reshape(A)
    w_flat = weights.reshape(A)
    tok_flat = jnp.arange(A, dtype=jnp.int32) // TOP_K
    order = jnp.argsort(e_flat)
    e_sorted = e_flat[order]
    tok_sorted = tok_flat[order]
    w_sorted = w_flat[order]
    counts = jnp.bincount(e_flat, length=N_EXPERTS)
    start = jnp.cumsum(counts) - counts
    padded = ((counts + MOE_BLOCK - 1) // MOE_BLOCK) * MOE_BLOCK
    pad_end = jnp.cumsum(padded)
    pad_start = pad_end - padded
    rank = jnp.arange(A, dtype=jnp.int32) - start[e_sorted]
    dest = pad_start[e_sorted] + rank
    n_rows = A + N_EXPERTS * MOE_BLOCK
    n_blocks = n_rows // MOE_BLOCK
    rows = jnp.zeros((n_rows, D), xt.dtype).at[dest].set(xt[tok_sorted])
    block_start = jnp.arange(n_blocks, dtype=jnp.int32) * MOE_BLOCK
    block_expert = jnp.clip(jnp.searchsorted(pad_end, block_start, side='right'), 0, N_EXPERTS - 1)

    def expert_block(args):
        xb, e = args
        hidden = jax.nn.silu(xb @ w_gate[e]) * (xb @ w_up[e])
        return hidden @ w_down[e]

    y_rows = lax.map(expert_block, (rows.reshape(n_blocks, MOE_BLOCK, D), block_expert))
    y_rows = y_rows.reshape(n_rows, D)
    contrib = y_rows[dest] * w_sorted[:, None].astype(y_rows.dtype)
    y = jax.ops.segment_sum(contrib, tok_sorted, num_segments=T)
    return y.reshape(B, S, D).astype(xn.dtype)


def setup_inputs(seed: int = 0) -> dict:
    key = jax.random.key(seed)
    ks = jax.random.split(key, 24)

    def nrm(k, shape, scale):
        return jax.random.normal(k, shape, F32) * scale

    def gain(k, shape):
        return 1.0 + 0.01 * jax.random.normal(k, shape, F32)

    x = nrm(ks[0], (BATCH, SEQ, D_MODEL), 1.0)
    p = nrm(ks[1], (DEPTH, BATCH, SEQ, PLE_DIM), 1.0)
    positions = (jnp.arange(SEQ, dtype=jnp.int32)[None, :]
                 + jax.random.randint(ks[2], (BATCH, 1), 0, 1024, dtype=jnp.int32))
    return {
        'x': x,
        'p': p,
        'positions': positions,
        'mix_norm': gain(ks[3], (DEPTH, D_MODEL)),
        'w_in': nrm(ks[4], (DEPTH, D_MODEL, IN_WIDTH), D_MODEL ** -0.5),
        'q_norm': gain(ks[5], (DEPTH, ATTN_HEAD_DIM)),
        'k_norm': gain(ks[6], (DEPTH, ATTN_HEAD_DIM)),
        'sinks': nrm(ks[7], (DEPTH, ATTN_HEADS), 0.5),
        'lb_logits': nrm(ks[8], (DEPTH, REC_KEY_WIDTH), 0.1),
        'rec_norm': gain(ks[9], (DEPTH, REC_VAL_DIM)),
        'w_out': nrm(ks[10], (DEPTH, MIX_WIDTH, D_MODEL), MIX_WIDTH ** -0.5),
        'ffn_norm': gain(ks[11], (DEPTH, D_MODEL)),
        'w_router_group': nrm(ks[12], (DEPTH, D_MODEL, N_GROUPS), D_MODEL ** -0.5),
        'w_router_expert': nrm(ks[13], (DEPTH, D_MODEL, N_EXPERTS), D_MODEL ** -0.5),
        'w_gate': nrm(ks[14], (DEPTH, N_EXPERTS, D_MODEL, EXPERT_FF), D_MODEL ** -0.5),
        'w_up': nrm(ks[15], (DEPTH, N_EXPERTS, D_MODEL, EXPERT_FF), D_MODEL ** -0.5),
        'w_down': nrm(ks[16], (DEPTH, N_EXPERTS, EXPERT_FF, D_MODEL), EXPERT_FF ** -0.5),
        'w_ple': nrm(ks[17], (DEPTH, PLE_DIM, D_MODEL), PLE_DIM ** -0.5),
        'ple_norm': gain(ks[18], (DEPTH, D_MODEL)),
        'ple_gate_norm': gain(ks[19], (DEPTH, D_MODEL)),
        'w_ple_gate': nrm(ks[20], (DEPTH, D_MODEL, D_MODEL), D_MODEL ** -0.5),
    }


def reference(x, p, positions, mix_norm, w_in, q_norm, k_norm, sinks, lb_logits, rec_norm,
              w_out, ffn_norm, w_router_group, w_router_expert, w_gate, w_up, w_down,
              w_ple, ple_norm, ple_gate_norm, w_ple_gate):
    B, S, _ = x.shape
    inv_freq = ROPE_THETA ** (-jnp.arange(0, ROPE_DIM, 2, dtype=F32) / ROPE_DIM)
    ang = positions.astype(F32)[..., None] * inv_freq
    cos = jnp.cos(ang)[:, :, None, :]
    sin = jnp.sin(ang)[:, :, None, :]
    lb_sm = jax.nn.softmax(lb_logits.astype(F32), axis=0)
    lower_bounds = jnp.cumsum(lb_sm, axis=0) - lb_sm[0:1]
    split_points = []
    acc = 0
    for sz in SPLIT_SIZES[:-1]:
        acc += sz
        split_points.append(acc)

    h = x
    for l in range(DEPTH):
        xn = rms_norm(h, mix_norm[l])
        z = xn @ w_in[l]
        qa, ka, va, qr, fr, ir, gr = jnp.split(z, split_points, axis=-1)
        qa = apply_partial_rope(rms_norm(qa.reshape(B, S, ATTN_HEADS, ATTN_HEAD_DIM), q_norm[l]), cos, sin)
        ka = apply_partial_rope(rms_norm(ka.reshape(B, S, ATTN_KV_HEADS, ATTN_HEAD_DIM), k_norm[l]), cos, sin)
        va = va.reshape(B, S, ATTN_KV_HEADS, ATTN_HEAD_DIM)
        attn = sliding_window_attention(qa, ka, va, sinks[l])
        rec = hgrn2(qr, fr, ir, gr, lower_bounds[l], rec_norm[l])
        mixed = jnp.concatenate([attn.astype(h.dtype), rec.astype(h.dtype)], axis=-1)
        h = h + (mixed @ w_out[l]).astype(h.dtype)
        h = h + hierarchical_moe(rms_norm(h, ffn_norm[l]), w_router_group[l], w_router_expert[l],
                                 w_gate[l], w_up[l], w_down[l]).astype(h.dtype)
        ple = rms_norm(p[l] @ w_ple[l], ple_norm[l]).astype(F32)
        gate = jax.nn.sigmoid((rms_norm(h, ple_gate_norm[l]) @ w_ple_gate[l]).astype(F32))
        h = h + (ple * gate).astype(h.dtype)
    return h
```

```python
import functools

import jax
import jax.numpy as jnp
import numpy as np
from jax import lax
from jax.experimental import pallas as pl
from jax.experimental.pallas import tpu as pltpu

F32 = jnp.float32
BF16 = jnp.bfloat16
I32 = jnp.int32

D_MODEL = 1024
ATTN_HEADS = 8
ATTN_KV_HEADS = 2
ATTN_HEAD_DIM = 64
ATTN_GROUP = ATTN_HEADS // ATTN_KV_HEADS
ATTN_WIDTH = ATTN_HEADS * ATTN_HEAD_DIM
KV_WIDTH = ATTN_KV_HEADS * ATTN_HEAD_DIM
QK_WIDTH = ATTN_WIDTH + KV_WIDTH
QKV_WIDTH = ATTN_WIDTH + 2 * KV_WIDTH
WINDOW = 128
ROPE_THETA = 500000.0
ROPE_DIM = ATTN_HEAD_DIM // 4
ROPE_HALF = ROPE_DIM // 2
REC_HEADS = 4
REC_DIM = 128
REC_WIDTH = REC_HEADS * REC_DIM
REC_CHUNK = 128
IN_WIDTH = QKV_WIDTH + 4 * REC_WIDTH
N_GROUPS = 4
EXPERTS_PER_GROUP = 8
N_EXPERTS = N_GROUPS * EXPERTS_PER_GROUP
EXPERT_FF = 512
PLE_DIM = 256
RMS_EPS = 1e-6
MASK_VALUE = -1e30
LANES = 128
ROUTE_LANES = LANES

TOKEN_TILE = 512
EXPERT_BLOCK = 256
COPY_TOKENS = 512


def _dot(a, b):
    return jnp.dot(a, b, preferred_element_type=F32)


def _dot_nt(a, b):
    return lax.dot_general(a, b, (((1,), (1,)), ((), ())), preferred_element_type=F32)


def _dot_tn(a, b):
    return lax.dot_general(a, b, (((0,), (0,)), ((), ())), preferred_element_type=F32)


def _sigmoid(x):
    return 1.0 / (1.0 + jnp.exp(-x))


def _rms(x, gain):
    ms = jnp.mean(x * x, axis=-1, keepdims=True)
    return x * lax.rsqrt(ms + RMS_EPS) * gain


def _mix_in_kernel(h_ref, gain_ref, w_ref, rc_ref, rs1_ref, rs2_ref, qkg_ref, seg_ref,
                   za_ref, zr_ref, zf_ref):
    xn = _rms(h_ref[...], gain_ref[...]).astype(BF16)
    z_a = _dot(xn, w_ref[:, 0:QKV_WIDTH])
    qk = z_a[:, 0:QK_WIDTH]
    seg = _dot((qk * qk).astype(BF16), seg_ref[...]) * (1.0 / ATTN_HEAD_DIM)
    qkn = qk * lax.rsqrt(seg + RMS_EPS) * qkg_ref[...]
    rc, rs1, rs2 = rc_ref[...], rs1_ref[...], rs2_ref[...]
    for c in range(QK_WIDTH // LANES):
        col = qkn[:, c * LANES:(c + 1) * LANES]
        rot = col * rc + pltpu.roll(col, LANES - ROPE_HALF, 1) * rs1 + pltpu.roll(col, ROPE_HALF, 1) * rs2
        za_ref[:, c * LANES:(c + 1) * LANES] = rot.astype(BF16)
    za_ref[:, QK_WIDTH:QKV_WIDTH] = z_a[:, QK_WIDTH:QKV_WIDTH].astype(BF16)
    zr_ref[...] = _dot(xn, w_ref[:, QKV_WIDTH:QKV_WIDTH + 3 * REC_WIDTH]).astype(BF16)
    zf_ref[...] = _dot(xn, w_ref[:, QKV_WIDTH + 3 * REC_WIDTH:IN_WIDTH])


def _mix_in(h, gain, w, rc, rs1, rs2, qkg, seg):
    T = h.shape[0]
    tm = min(TOKEN_TILE, T)
    row = lambda i: (i, 0)
    fixed = lambda i: (0, 0)
    return pl.pallas_call(
        _mix_in_kernel,
        grid=(T // tm,),
        in_specs=[
            pl.BlockSpec((tm, D_MODEL), row),
            pl.BlockSpec((1, D_MODEL), fixed),
            pl.BlockSpec((D_MODEL, IN_WIDTH), fixed),
            pl.BlockSpec((tm, LANES), row),
            pl.BlockSpec((tm, LANES), row),
            pl.BlockSpec((tm, LANES), row),
            pl.BlockSpec((1, QK_WIDTH), fixed),
            pl.BlockSpec((QK_WIDTH, QK_WIDTH), fixed),
        ],
        out_specs=[
            pl.BlockSpec((tm, QKV_WIDTH), row),
            pl.BlockSpec((tm, 3 * REC_WIDTH), row),
            pl.BlockSpec((tm, REC_WIDTH), row),
        ],
        out_shape=[
            jax.ShapeDtypeStruct((T, QKV_WIDTH), BF16),
            jax.ShapeDtypeStruct((T, 3 * REC_WIDTH), BF16),
            jax.ShapeDtypeStruct((T, REC_WIDTH), F32),
        ],
        compiler_params=pltpu.CompilerParams(dimension_semantics=("parallel",)),
        name="mix_in",
    )(h, gain, w, rc, rs1, rs2, qkg, seg)


def _attn_kernel(sink_ref, q_ref, kvc_ref, kvp_ref, o_ref):
    n = pl.program_id(1)
    q = q_ref[...]
    kvc = kvc_ref[...]
    kvp = kvp_ref[...]
    rows = lax.broadcasted_iota(I32, (ATTN_GROUP * WINDOW, 2 * WINDOW), 0)
    cols = lax.broadcasted_iota(I32, (ATTN_GROUP * WINDOW, 2 * WINDOW), 1)
    qi = rows & (WINDOW - 1)
    valid = (cols > qi) & (cols <= qi + WINDOW) & ((cols >= WINDOW) | (n > 0))
    grp = lax.broadcasted_iota(I32, (ATTN_GROUP * WINDOW, 1), 0) // WINDOW
    for j in range(ATTN_KV_HEADS):
        ks = slice(j * ATTN_HEAD_DIM, (j + 1) * ATTN_HEAD_DIM)
        vs = slice(KV_WIDTH + j * ATTN_HEAD_DIM, KV_WIDTH + (j + 1) * ATTN_HEAD_DIM)
        k = jnp.concatenate([kvp[:, ks], kvc[:, ks]], axis=0)
        v = jnp.concatenate([kvp[:, vs], kvc[:, vs]], axis=0)
        heads = [j * ATTN_GROUP + g for g in range(ATTN_GROUP)]
        q4 = jnp.concatenate([q[:, hh * ATTN_HEAD_DIM:(hh + 1) * ATTN_HEAD_DIM] for hh in heads], axis=0)
        s = jnp.where(valid, _dot_nt(q4, k), MASK_VALUE)
        sink = jnp.zeros((ATTN_GROUP * WINDOW, 1), F32)
        for g, hh in enumerate(heads):
            sink = jnp.where(grp == g, sink_ref[hh], sink)
        m = jnp.maximum(jnp.max(s, axis=-1, keepdims=True), sink)
        e = jnp.exp(s - m)
        denom = jnp.sum(e, axis=-1, keepdims=True) + jnp.exp(sink - m)
        p = (e * (1.0 / denom)).astype(BF16)
        o = _dot(p, v)
        for g, hh in enumerate(heads):
            o_ref[:, hh * ATTN_HEAD_DIM:(hh + 1) * ATTN_HEAD_DIM] = o[g * WINDOW:(g + 1) * WINDOW].astype(BF16)


def _attention(za, sinks, B, S):
    nb = S // WINDOW
    kvblk = ATTN_WIDTH // (2 * KV_WIDTH)
    return pl.pallas_call(
        _attn_kernel,
        grid=(B, nb),
        in_specs=[
            pl.BlockSpec(memory_space=pltpu.SMEM),
            pl.BlockSpec((WINDOW, ATTN_WIDTH), lambda b, n: (b * nb + n, 0)),
            pl.BlockSpec((WINDOW, 2 * KV_WIDTH), lambda b, n: (b * nb + n, kvblk)),
            pl.BlockSpec((WINDOW, 2 * KV_WIDTH), lambda b, n: (b * nb + jnp.maximum(n - 1, 0), kvblk)),
        ],
        out_specs=pl.BlockSpec((WINDOW, ATTN_WIDTH), lambda b, n: (b * nb + n, 0)),
        out_shape=jax.ShapeDtypeStruct((B * S, ATTN_WIDTH), BF16),
        compiler_params=pltpu.CompilerParams(dimension_semantics=("parallel", "arbitrary")),
        name="swa_attention",
    )(sinks, za, za, za)


def _hgrn_levels(C):
    out, s = [], C // 2
    while s >= 1:
        out.append(s)
        s //= 2
    return out


def _hgrn_constants(C):
    r = np.arange(C)[:, None]
    u = np.arange(C)[None, :]
    mats = [u <= r, u > r]
    masks = []
    for s in _hgrn_levels(C):
        mid = (r // (2 * s)) * 2 * s + s
        upper = r >= mid
        mats.append(np.where(upper, (u >= mid) & (u <= r), (u > r) & (u < mid)))
        masks.append((r // (2 * s)) == (u // (2 * s)))
    masks.append(r == u)
    return (np.concatenate(mats, axis=0).astype(np.float32), np.stack(masks).astype(np.float32))


def _hgrn_kernel(zr_ref, zf_ref, lb_ref, gain_ref, msum_ref, pmask_ref, o_ref, st_ref):
    C = zf_ref.shape[0]
    levels = _hgrn_levels(C)

    @pl.when(pl.program_id(1) == 0)
    def _():
        st_ref[...] = jnp.zeros_like(st_ref)

    row = lax.broadcasted_iota(I32, (C, REC_DIM), 0)
    for hd in range(REC_HEADS):
        cs = slice(hd * REC_DIM, (hd + 1) * REC_DIM)
        zq = zr_ref[:, cs].astype(F32)
        v = zr_ref[:, REC_WIDTH + hd * REC_DIM:REC_WIDTH + (hd + 1) * REC_DIM]
        og = zr_ref[:, 2 * REC_WIDTH + hd * REC_DIM:2 * REC_WIDTH + (hd + 1) * REC_DIM].astype(F32)
        z = zf_ref[:, cs]
        lb = lb_ref[:, cs]
        qp = zq * _sigmoid(zq)
        a = jnp.exp(-jnp.abs(z))
        r = 1.0 / (1.0 + a)
        pos = z >= 0
        g = jnp.log(lb + (1.0 - lb) * jnp.where(pos, r, a * r))
        k = (1.0 - lb) * jnp.where(pos, a * r, r)
        g_hi = g.astype(BF16)
        g_lo = (g - g_hi.astype(F32)).astype(BF16)
        x2 = _dot(msum_ref[...], jnp.concatenate([g_hi, g_lo], axis=1))
        e_all = jnp.exp(x2[:, :REC_DIM] + x2[:, REC_DIM:])
        e_b = e_all[0:C]
        e_e = e_all[C:2 * C]
        st = st_ref[hd]
        o = _dot_nt((qp * e_b).astype(BF16), st.astype(BF16))
        kb = k.astype(BF16)
        amat = pmask_ref[len(levels)] * _dot_nt(qp.astype(BF16), kb)
        for li, s in enumerate(levels):
            e_l = e_all[(2 + li) * C:(3 + li) * C]
            upper = (row & s) != 0
            ql = jnp.where(upper, qp * e_l, 0.0).astype(BF16)
            kl = jnp.where(upper, 0.0, k * e_l).astype(BF16)
            amat = amat + pmask_ref[li] * _dot_nt(ql, kl)
        o = o + _dot(amat.astype(BF16), v)
        st_ref[hd] = st * e_b[C - 1:C, :] + _dot_tn(v, (k * e_e).astype(BF16))
        og_act = og * _sigmoid(og)
        o_ref[:, cs] = (_rms(o, gain_ref[...]) * og_act).astype(BF16)


def _hgrn2(zr, zf, lb, gain, msum, pmask, B, S):
    C = min(REC_CHUNK, S)
    nc = S // C
    row = lambda b, c: (b * nc + c, 0)
    return pl.pallas_call(
        _hgrn_kernel,
        grid=(B, nc),
        in_specs=[
            pl.BlockSpec((C, 3 * REC_WIDTH), row),
            pl.BlockSpec((C, REC_WIDTH), row),
            pl.BlockSpec((1, REC_WIDTH), lambda b, c: (0, 0)),
            pl.BlockSpec((1, REC_DIM), lambda b, c: (0, 0)),
            pl.BlockSpec(msum.shape, lambda b, c: (0, 0)),
            pl.BlockSpec(pmask.shape, lambda b, c: (0, 0, 0)),
        ],
        out_specs=pl.BlockSpec((C, REC_WIDTH), row),
        out_shape=jax.ShapeDtypeStruct((B * S, REC_WIDTH), BF16),
        scratch_shapes=[pltpu.VMEM((REC_HEADS, REC_DIM, REC_DIM), F32)],
        compiler_params=pltpu.CompilerParams(dimension_semantics=("parallel", "arbitrary")),
        name="hgrn2",
    )(zr, zf, lb, gain, msum, pmask)


def _mix_out_kernel(attn_ref, rec_ref, h_ref, wo_ref, gain_ref, wrh_ref, wrl_ref, tril_ref,
                    h1_ref, xn_ref, ri_ref, rw_ref, cnt_ref, carry_ref):
    @pl.when(pl.program_id(0) == 0)
    def _():
        carry_ref[...] = jnp.zeros_like(carry_ref)

    h1 = (h_ref[...] + _dot(attn_ref[...], wo_ref[0:ATTN_WIDTH, :])
          + _dot(rec_ref[...], wo_ref[ATTN_WIDTH:ATTN_WIDTH + REC_WIDTH, :]))
    h1_ref[...] = h1
    xn = _rms(h1, gain_ref[...])
    xn_ref[...] = xn
    xh = xn.astype(BF16)
    xl = (xn - xh.astype(F32)).astype(BF16)
    logits = _dot(xh, wrh_ref[...]) + _dot(xl, wrh_ref[...]) + _dot(xh, wrl_ref[...])
    lane = lax.broadcasted_iota(I32, logits.shape, 1)
    lanef = lane.astype(F32)
    neg = jnp.float32(-jnp.inf)
    big = jnp.float32(1e9)
    gl = jnp.where(lane < N_GROUPS, logits, neg)
    gmax = jnp.max(gl, axis=-1, keepdims=True)
    gidx = jnp.min(jnp.where(gl == gmax, lanef, big), axis=-1, keepdims=True)
    p_group = 1.0 / jnp.sum(jnp.where(lane < N_GROUPS, jnp.exp(logits - gmax), 0.0), axis=-1, keepdims=True)
    lo = N_GROUPS + gidx * EXPERTS_PER_GROUP
    el = jnp.where((lanef >= lo) & (lanef < lo + EXPERTS_PER_GROUP), logits, neg)
    t1 = jnp.max(el, axis=-1, keepdims=True)
    i1 = jnp.min(jnp.where(el == t1, lanef, big), axis=-1, keepdims=True)
    el2 = jnp.where(lanef == i1, neg, el)
    t2 = jnp.max(el2, axis=-1, keepdims=True)
    i2 = jnp.min(jnp.where(el2 == t2, lanef, big), axis=-1, keepdims=True)
    r21 = jnp.exp(t2 - t1)
    w1 = p_group / (1.0 + r21)
    w2 = w1 * r21
    e1 = i1 - N_GROUPS
    e2 = i2 - N_GROUPS
    oh1 = (lanef == e1).astype(F32)
    oh2 = (lanef == e2).astype(F32)
    oh = oh1 + oh2
    prefix = _dot(tril_ref[...], oh.astype(BF16)) + carry_ref[...]
    rank1 = jnp.sum(prefix * oh1, axis=-1, keepdims=True)
    rank2 = jnp.sum(prefix * oh2, axis=-1, keepdims=True)
    carry = carry_ref[...] + jnp.sum(oh, axis=0, keepdims=True)
    carry_ref[...] = carry
    cnt_ref[...] = jnp.broadcast_to(carry, cnt_ref.shape)
    ri = jnp.where(lane == 0, e1, jnp.where(lane == 1, e2, jnp.where(lane == 2, rank1, jnp.where(lane == 3, rank2, 0.0))))
    ri_ref[...] = ri.astype(I32)
    rw_ref[...] = jnp.where(lane == 0, w1, jnp.where(lane == 1, w2, 0.0))


def _mix_out(attn, rec, h, wo, gain, wrh, wrl, tril):
    T = h.shape[0]
    tm = tril.shape[0]
    row = lambda i: (i, 0)
    fixed = lambda i: (0, 0)
    return pl.pallas_call(
        _mix_out_kernel,
        grid=(T // tm,),
        in_specs=[
            pl.BlockSpec((tm, ATTN_WIDTH), row),
            pl.BlockSpec((tm, REC_WIDTH), row),
            pl.BlockSpec((tm, D_MODEL), row),
            pl.BlockSpec((ATTN_WIDTH + REC_WIDTH, D_MODEL), fixed),
            pl.BlockSpec((1, D_MODEL), fixed),
            pl.BlockSpec((D_MODEL, ROUTE_LANES), fixed),
            pl.BlockSpec((D_MODEL, ROUTE_LANES), fixed),
            pl.BlockSpec((tm, tm), fixed),
        ],
        out_specs=[
            pl.BlockSpec((tm, D_MODEL), row),
            pl.BlockSpec((tm, D_MODEL), row),
            pl.BlockSpec((tm, ROUTE_LANES), row),
            pl.BlockSpec((tm, ROUTE_LANES), row),
            pl.BlockSpec((8, ROUTE_LANES), fixed),
        ],
        out_shape=[
            jax.ShapeDtypeStruct((T, D_MODEL), F32),
            jax.ShapeDtypeStruct((T, D_MODEL), F32),
            jax.ShapeDtypeStruct((T, ROUTE_LANES), I32),
            jax.ShapeDtypeStruct((T, ROUTE_LANES), F32),
            jax.ShapeDtypeStruct((8, ROUTE_LANES), F32),
        ],
        scratch_shapes=[pltpu.VMEM((1, ROUTE_LANES), F32)],
        compiler_params=pltpu.CompilerParams(dimension_semantics=("arbitrary",)),
        name="mix_out_router",
    )(attn, rec, h, wo, gain, wrh, wrl, tril)


def _row_copy_kernel(dest_ref, src_ref, *rest, tokens, scatter):
    out_ref, sem = rest[-2:]
    base = pl.program_id(0) * tokens

    def copy(a):
        if scatter:
            return pltpu.make_async_copy(src_ref.at[pl.ds(a // 2, 1)], out_ref.at[pl.ds(dest_ref[a], 1)], sem)
        return pltpu.make_async_copy(src_ref.at[pl.ds(dest_ref[a], 1)], out_ref.at[pl.ds(a, 1)], sem)

    def start(t, carry):
        copy(2 * (base + t)).start()
        copy(2 * (base + t) + 1).start()
        return carry

    def wait(t, carry):
        copy(2 * (base + t)).wait()
        copy(2 * (base + t) + 1).wait()
        return carry

    lax.fori_loop(0, tokens, start, 0)
    lax.fori_loop(0, tokens, wait, 0)


def _row_copy(dest, src, n_out_rows, *, scatter):
    n_assign = dest.shape[0]
    tokens = min(COPY_TOKENS, n_assign // 2)
    out_shape = jax.ShapeDtypeStruct((n_out_rows, src.shape[1]), src.dtype)
    operands = (dest, src, jnp.zeros(out_shape.shape, out_shape.dtype)) if scatter else (dest, src)
    return pl.pallas_call(
        functools.partial(_row_copy_kernel, tokens=tokens, scatter=scatter),
        grid_spec=pltpu.PrefetchScalarGridSpec(
            num_scalar_prefetch=1,
            grid=(n_assign // (2 * tokens),),
            in_specs=[pl.BlockSpec(memory_space=pl.ANY)] * (len(operands) - 1),
            out_specs=pl.BlockSpec(memory_space=pl.ANY),
            scratch_shapes=[pltpu.SemaphoreType.DMA],
        ),
        out_shape=out_shape,
        input_output_aliases={2: 0} if scatter else {},
        compiler_params=pltpu.CompilerParams(dimension_semantics=("arbitrary",)),
        name="moe_scatter_rows" if scatter else "moe_gather_rows",
    )(*operands)


def _expert_kernel(be_ref, nused_ref, x_ref, wg_ref, wu_ref, wd_ref, y_ref):
    del be_ref
    used = pl.program_id(0) < nused_ref[0]

    @pl.when(used)
    def _():
        x = x_ref[...].astype(BF16)
        gate = _dot(x, wg_ref[0])
        up = _dot(x, wu_ref[0])
        hidden = (gate * _sigmoid(gate) * up).astype(BF16)
        y_ref[...] = _dot(hidden, wd_ref[0])

    @pl.when(jnp.logical_not(used))
    def _():
        y_ref[...] = jnp.zeros_like(y_ref)


def _experts(block_expert, n_used, xs, wg, wu, wd, layer):
    n_rows = xs.shape[0]
    nblk = n_rows // EXPERT_BLOCK

    def xmap(i, be, nu):
        return (jnp.minimum(i, nu[0] - 1), 0)

    def wmap(i, be, nu):
        return (layer * N_EXPERTS + be[jnp.minimum(i, nu[0] - 1)], 0, 0)

    return pl.pallas_call(
        _expert_kernel,
        grid_spec=pltpu.PrefetchScalarGridSpec(
            num_scalar_prefetch=2,
            grid=(nblk,),
            in_specs=[
                pl.BlockSpec((EXPERT_BLOCK, D_MODEL), xmap),
                pl.BlockSpec((1, D_MODEL, EXPERT_FF), wmap),
                pl.BlockSpec((1, D_MODEL, EXPERT_FF), wmap),
                pl.BlockSpec((1, EXPERT_FF, D_MODEL), wmap),
            ],
            out_specs=pl.BlockSpec((EXPERT_BLOCK, D_MODEL), lambda i, be, nu: (i, 0)),
        ),
        out_shape=jax.ShapeDtypeStruct((n_rows, D_MODEL), F32),
        compiler_params=pltpu.CompilerParams(dimension_semantics=("arbitrary",)),
        name="moe_experts",
    )(block_expert, n_used, xs, wg, wu, wd)


def _ple_kernel(h1_ref, yg_ref, rw_ref, p_ref, wple_ref, pgain_ref, ggain_ref, wpg_ref, o_ref):
    rw = rw_ref[...]
    h2 = h1_ref[...] + rw[:, 0:1] * yg_ref[:, 0:D_MODEL] + rw[:, 1:2] * yg_ref[:, D_MODEL:2 * D_MODEL]
    ple = _rms(_dot(p_ref[...].astype(BF16), wple_ref[...]), pgain_ref[...])
    gate = _sigmoid(_dot(_rms(h2, ggain_ref[...]).astype(BF16), wpg_ref[...]))
    o_ref[...] = h2 + ple * gate


def _ple(h1, yg, rw, p, wple, pgain, ggain, wpg, layer):
    T = h1.shape[0]
    tm = min(TOKEN_TILE, T)
    nt = T // tm
    row = lambda i: (i, 0)
    fixed = lambda i: (0, 0)
    return pl.pallas_call(
        _ple_kernel,
        grid=(nt,),
        in_specs=[
            pl.BlockSpec((tm, D_MODEL), row),
            pl.BlockSpec((tm, 2 * D_MODEL), row),
            pl.BlockSpec((tm, ROUTE_LANES), row),
            pl.BlockSpec((tm, PLE_DIM), lambda i: (layer * nt + i, 0)),
            pl.BlockSpec((PLE_DIM, D_MODEL), fixed),
            pl.BlockSpec((1, D_MODEL), fixed),
            pl.BlockSpec((1, D_MODEL), fixed),
            pl.BlockSpec((D_MODEL, D_MODEL), fixed),
        ],
        out_specs=pl.BlockSpec((tm, D_MODEL), row),
        out_shape=jax.ShapeDtypeStruct((T, D_MODEL), F32),
        compiler_params=pltpu.CompilerParams(dimension_semantics=("parallel",)),
        name="combine_ple",
    )(h1, yg, rw, p, wple, pgain, ggain, wpg)


def _rope_tables(positions):
    inv_freq = ROPE_THETA ** (-jnp.arange(0, ROPE_DIM, 2, dtype=F32) / ROPE_DIM)
    ang = positions.astype(F32).reshape(-1, 1) * inv_freq
    cos, sin = jnp.cos(ang), jnp.sin(ang)
    T = ang.shape[0]
    rest = jnp.zeros((T, ATTN_HEAD_DIM - ROPE_DIM), F32)
    zero = jnp.zeros((T, ROPE_HALF), F32)
    rc = jnp.concatenate([cos, cos, rest + 1.0], axis=1)
    rs1 = jnp.concatenate([-sin, zero, rest], axis=1)
    rs2 = jnp.concatenate([zero, sin, rest], axis=1)
    reps = LANES // ATTN_HEAD_DIM
    return tuple(jnp.tile(t, (1, reps)) for t in (rc, rs1, rs2))


def kernel(x, p, positions, mix_norm, w_in, q_norm, k_norm, sinks, lb_logits, rec_norm, w_out, ffn_norm,
           w_router_group, w_router_expert, w_gate, w_up, w_down, w_ple, ple_norm, ple_gate_norm, w_ple_gate):
    B, S, D = x.shape
    depth = w_in.shape[0]
    T = B * S
    n_assign = 2 * T
    assert D == D_MODEL and S % WINDOW == 0 and T % min(TOKEN_TILE, T) == 0

    rc, rs1, rs2 = _rope_tables(positions)
    lb_sm = jax.nn.softmax(lb_logits.astype(F32), axis=0)
    lower_bounds = jnp.cumsum(lb_sm, axis=0) - lb_sm[0:1]

    a0 = QKV_WIDTH
    w_in_p = jnp.concatenate([w_in[:, :, :a0 + REC_WIDTH], w_in[:, :, a0 + 2 * REC_WIDTH:],
                              w_in[:, :, a0 + REC_WIDTH:a0 + 2 * REC_WIDTH]], axis=2).astype(BF16)
    w_out_b = w_out.astype(BF16)
    wg_b = w_gate.astype(BF16).reshape(depth * N_EXPERTS, D_MODEL, EXPERT_FF)
    wu_b = w_up.astype(BF16).reshape(depth * N_EXPERTS, D_MODEL, EXPERT_FF)
    wd_b = w_down.astype(BF16).reshape(depth * N_EXPERTS, EXPERT_FF, D_MODEL)
    w_ple_b = w_ple.astype(BF16)
    w_pg_b = w_ple_gate.astype(BF16)
    w_r = jnp.concatenate([w_router_group, w_router_expert,
                           jnp.zeros((depth, D_MODEL, ROUTE_LANES - N_GROUPS - N_EXPERTS), F32)], axis=2)
    w_r_hi = w_r.astype(BF16)
    w_r_lo = (w_r - w_r_hi.astype(F32)).astype(BF16)
    qk_gain = jnp.concatenate([jnp.tile(q_norm, (1, ATTN_HEADS)) * (ATTN_HEAD_DIM ** -0.5),
                               jnp.tile(k_norm, (1, ATTN_KV_HEADS))], axis=1)
    seg_id = np.arange(QK_WIDTH) // ATTN_HEAD_DIM
    seg = jnp.asarray(seg_id[:, None] == seg_id[None, :], BF16)
    msum_np, pmask_np = _hgrn_constants(min(REC_CHUNK, S))
    msum = jnp.asarray(msum_np, BF16)
    pmask = jnp.asarray(pmask_np, F32)
    tm = min(TOKEN_TILE, T)
    tril = jnp.asarray(np.tril(np.ones((tm, tm), np.float32), -1), BF16)

    n_rows = n_assign + N_EXPERTS * EXPERT_BLOCK
    nblk = n_rows // EXPERT_BLOCK
    eids = jnp.arange(N_EXPERTS, dtype=I32)
    p2 = p.reshape(depth * T, PLE_DIM)

    h = x.reshape(T, D)
    for l in range(depth):
        za, zr, zf = _mix_in(h, mix_norm[l][None], w_in_p[l], rc, rs1, rs2, qk_gain[l][None], seg)
        attn = _attention(za, sinks[l], B, S)
        rec = _hgrn2(zr, zf, lower_bounds[l][None], rec_norm[l][None], msum, pmask, B, S)
        h1, xn, ri, rw, cnt = _mix_out(attn, rec, h, w_out_b[l], ffn_norm[l][None], w_r_hi[l], w_r_lo[l], tril)
        counts = cnt[0, :N_EXPERTS].astype(I32)
        padded = ((counts + EXPERT_BLOCK - 1) // EXPERT_BLOCK) * EXPERT_BLOCK
        pad_end = jnp.cumsum(padded)
        pad_start = pad_end - padded
        expert = ri[:, 0:2]
        dest = ri[:, 2:4] + jnp.sum(jnp.where(expert[:, :, None] == eids, pad_start, 0), axis=-1)
        dest = dest.reshape(n_assign).astype(I32)
        n_used = (pad_end[-1] // EXPERT_BLOCK).astype(I32).reshape(1)
        blk_start = jnp.arange(nblk, dtype=I32) * EXPERT_BLOCK
        block_expert = jnp.minimum(jnp.sum(pad_end[None, :] <= blk_start[:, None], axis=1), N_EXPERTS - 1).astype(I32)
        xs = _row_copy(dest, xn, n_rows, scatter=True)
        ys = _experts(block_expert, n_used, xs, wg_b, wu_b, wd_b, l)
        yg = _row_copy(dest, ys, n_assign, scatter=False)
        h = _ple(h1, yg.reshape(T, 2 * D_MODEL), rw, p2, w_ple_b[l], ple_norm[l][None], ple_gate_norm[l][None],
                 w_pg_b[l], l)
    return h.reshape(B, S, D)
```

```python
import functools

import jax
import jax.numpy as jnp
import numpy as np
from jax import lax
from jax.experimental import pallas as pl
from jax.experimental.pallas import tpu as pltpu

F32 = jnp.float32
BF16 = jnp.bfloat16
I32 = jnp.int32

D_MODEL = 1024
ATTN_HEADS = 8
ATTN_KV_HEADS = 2
ATTN_HEAD_DIM = 64
ATTN_GROUP = ATTN_HEADS // ATTN_KV_HEADS
ATTN_WIDTH = ATTN_HEADS * ATTN_HEAD_DIM
KV_WIDTH = ATTN_KV_HEADS * ATTN_HEAD_DIM
QK_WIDTH = ATTN_WIDTH + KV_WIDTH
QKV_WIDTH = ATTN_WIDTH + 2 * KV_WIDTH
WINDOW = 128
ROPE_THETA = 500000.0
ROPE_DIM = ATTN_HEAD_DIM // 4
ROPE_HALF = ROPE_DIM // 2
REC_HEADS = 4
REC_DIM = 128
REC_WIDTH = REC_HEADS * REC_DIM
REC_CHUNK = 128
IN_WIDTH = QKV_WIDTH + 4 * REC_WIDTH
N_GROUPS = 4
EXPERTS_PER_GROUP = 8
N_EXPERTS = N_GROUPS * EXPERTS_PER_GROUP
EXPERT_FF = 512
PLE_DIM = 256
RMS_EPS = 1e-6
MASK_VALUE = -1e30
LANES = 128
ROUTE_LANES = LANES
ROW_SUBLANES = D_MODEL // LANES

TOKEN_TILE = 512
EXPERT_BLOCK = 256
COPY_TOKENS = 512
COPY_UNROLL = 8


def _dot(a, b):
    return jnp.dot(a, b, preferred_element_type=F32)


def _dot_nt(a, b):
    return lax.dot_general(a, b, (((1,), (1,)), ((), ())), preferred_element_type=F32)


def _dot_tn(a, b):
    return lax.dot_general(a, b, (((0,), (0,)), ((), ())), preferred_element_type=F32)


def _store_row_tiles(ref, x):
    for c in range(ROW_SUBLANES):
        ref[:, c, :] = x[:, c * LANES:(c + 1) * LANES]


def _load_row_tiles(ref, rows=None):
    rows = slice(None) if rows is None else rows
    return jnp.concatenate([ref[rows, c, :] for c in range(ROW_SUBLANES)], axis=1)


def _sigmoid(x):
    return 1.0 / (1.0 + jnp.exp(-x))


def _rms(x, gain):
    ms = jnp.mean(x * x, axis=-1, keepdims=True)
    return x * lax.rsqrt(ms + RMS_EPS) * gain


def _mix_in_kernel(h_ref, gain_ref, w_ref, rc_ref, rs1_ref, rs2_ref, qkg_ref, seg_ref,
                   za_ref, zr_ref, zf_ref):
    xn = _rms(h_ref[...], gain_ref[...]).astype(BF16)
    z_a = _dot(xn, w_ref[:, 0:QKV_WIDTH])
    qk = z_a[:, 0:QK_WIDTH]
    seg = _dot((qk * qk).astype(BF16), seg_ref[...]) * (1.0 / ATTN_HEAD_DIM)
    qkn = qk * lax.rsqrt(seg + RMS_EPS) * qkg_ref[...]
    rc, rs1, rs2 = rc_ref[...], rs1_ref[...], rs2_ref[...]
    for c in range(QK_WIDTH // LANES):
        col = qkn[:, c * LANES:(c + 1) * LANES]
        rot = col * rc + pltpu.roll(col, LANES - ROPE_HALF, 1) * rs1 + pltpu.roll(col, ROPE_HALF, 1) * rs2
        za_ref[:, c * LANES:(c + 1) * LANES] = rot.astype(BF16)
    za_ref[:, QK_WIDTH:QKV_WIDTH] = z_a[:, QK_WIDTH:QKV_WIDTH].astype(BF16)
    zr_ref[...] = _dot(xn, w_ref[:, QKV_WIDTH:QKV_WIDTH + 3 * REC_WIDTH]).astype(BF16)
    zf_ref[...] = _dot(xn, w_ref[:, QKV_WIDTH + 3 * REC_WIDTH:IN_WIDTH])


def _mix_in(h, gain, w, rc, rs1, rs2, qkg, seg):
    T = h.shape[0]
    tm = min(TOKEN_TILE, T)
    row = lambda i: (i, 0)
    fixed = lambda i: (0, 0)
    return pl.pallas_call(
        _mix_in_kernel,
        grid=(T // tm,),
        in_specs=[
            pl.BlockSpec((tm, D_MODEL), row),
            pl.BlockSpec((1, D_MODEL), fixed),
            pl.BlockSpec((D_MODEL, IN_WIDTH), fixed),
            pl.BlockSpec((tm, LANES), row),
            pl.BlockSpec((tm, LANES), row),
            pl.BlockSpec((tm, LANES), row),
            pl.BlockSpec((1, QK_WIDTH), fixed),
            pl.BlockSpec((QK_WIDTH, QK_WIDTH), fixed),
        ],
        out_specs=[
            pl.BlockSpec((tm, QKV_WIDTH), row),
            pl.BlockSpec((tm, 3 * REC_WIDTH), row),
            pl.BlockSpec((tm, REC_WIDTH), row),
        ],
        out_shape=[
            jax.ShapeDtypeStruct((T, QKV_WIDTH), BF16),
            jax.ShapeDtypeStruct((T, 3 * REC_WIDTH), BF16),
            jax.ShapeDtypeStruct((T, REC_WIDTH), F32),
        ],
        compiler_params=pltpu.CompilerParams(dimension_semantics=("parallel",)),
        name="mix_in",
    )(h, gain, w, rc, rs1, rs2, qkg, seg)


def _attn_kernel(sink_ref, q_ref, kvc_ref, kvp_ref, o_ref):
    n = pl.program_id(1)
    q = q_ref[...]
    kvc = kvc_ref[...]
    kvp = kvp_ref[...]
    rows = lax.broadcasted_iota(I32, (ATTN_GROUP * WINDOW, 2 * WINDOW), 0)
    cols = lax.broadcasted_iota(I32, (ATTN_GROUP * WINDOW, 2 * WINDOW), 1)
    qi = rows & (WINDOW - 1)
    valid = (cols > qi) & (cols <= qi + WINDOW) & ((cols >= WINDOW) | (n > 0))
    grp = lax.broadcasted_iota(I32, (ATTN_GROUP * WINDOW, 1), 0) // WINDOW
    for j in range(ATTN_KV_HEADS):
        ks = slice(j * ATTN_HEAD_DIM, (j + 1) * ATTN_HEAD_DIM)
        vs = slice(KV_WIDTH + j * ATTN_HEAD_DIM, KV_WIDTH + (j + 1) * ATTN_HEAD_DIM)
        k = jnp.concatenate([kvp[:, ks], kvc[:, ks]], axis=0)
        v = jnp.concatenate([kvp[:, vs], kvc[:, vs]], axis=0)
        heads = [j * ATTN_GROUP + g for g in range(ATTN_GROUP)]
        q4 = jnp.concatenate([q[:, hh * ATTN_HEAD_DIM:(hh + 1) * ATTN_HEAD_DIM] for hh in heads], axis=0)
        s = jnp.where(valid, _dot_nt(q4, k), MASK_VALUE)
        sink = jnp.zeros((ATTN_GROUP * WINDOW, 1), F32)
        for g, hh in enumerate(heads):
            sink = jnp.where(grp == g, sink_ref[hh], sink)
        m = jnp.maximum(jnp.max(s, axis=-1, keepdims=True), sink)
        e = jnp.exp(s - m)
        denom = jnp.sum(e, axis=-1, keepdims=True) + jnp.exp(sink - m)
        p = (e * (1.0 / denom)).astype(BF16)
        o = _dot(p, v)
        for g, hh in enumerate(heads):
            o_ref[:, hh * ATTN_HEAD_DIM:(hh + 1) * ATTN_HEAD_DIM] = o[g * WINDOW:(g + 1) * WINDOW].astype(BF16)


def _attention(za, sinks, B, S):
    nb = S // WINDOW
    kvblk = ATTN_WIDTH // (2 * KV_WIDTH)
    return pl.pallas_call(
        _attn_kernel,
        grid=(B, nb),
        in_specs=[
            pl.BlockSpec(memory_space=pltpu.SMEM),
            pl.BlockSpec((WINDOW, ATTN_WIDTH), lambda b, n: (b * nb + n, 0)),
            pl.BlockSpec((WINDOW, 2 * KV_WIDTH), lambda b, n: (b * nb + n, kvblk)),
            pl.BlockSpec((WINDOW, 2 * KV_WIDTH), lambda b, n: (b * nb + jnp.maximum(n - 1, 0), kvblk)),
        ],
        out_specs=pl.BlockSpec((WINDOW, ATTN_WIDTH), lambda b, n: (b * nb + n, 0)),
        out_shape=jax.ShapeDtypeStruct((B * S, ATTN_WIDTH), BF16),
        compiler_params=pltpu.CompilerParams(dimension_semantics=("parallel", "arbitrary")),
        name="swa_attention",
    )(sinks, za, za, za)


def _hgrn_levels(C):
    out, s = [], C // 2
    while s >= 1:
        out.append(s)
        s //= 2
    return out


def _hgrn_constants(C):
    r = np.arange(C)[:, None]
    u = np.arange(C)[None, :]
    mats = [u <= r, u > r]
    masks = []
    for s in _hgrn_levels(C):
        mid = (r // (2 * s)) * 2 * s + s
        upper = r >= mid
        mats.append(np.where(upper, (u >= mid) & (u <= r), (u > r) & (u < mid)))
        masks.append((r // (2 * s)) == (u // (2 * s)))
    masks.append(r == u)
    return (np.concatenate(mats, axis=0).astype(np.float32), np.stack(masks).astype(np.float32))


def _hgrn_kernel(zr_ref, zf_ref, lb_ref, gain_ref, msum_ref, pmask_ref, o_ref, st_ref):
    C = zf_ref.shape[0]
    levels = _hgrn_levels(C)

    @pl.when(pl.program_id(1) == 0)
    def _():
        st_ref[...] = jnp.zeros_like(st_ref)

    row = lax.broadcasted_iota(I32, (C, REC_DIM), 0)
    for hd in range(REC_HEADS):
        cs = slice(hd * REC_DIM, (hd + 1) * REC_DIM)
        zq = zr_ref[:, cs].astype(F32)
        v = zr_ref[:, REC_WIDTH + hd * REC_DIM:REC_WIDTH + (hd + 1) * REC_DIM]
        og = zr_ref[:, 2 * REC_WIDTH + hd * REC_DIM:2 * REC_WIDTH + (hd + 1) * REC_DIM].astype(F32)
        z = zf_ref[:, cs]
        lb = lb_ref[:, cs]
        qp = zq * _sigmoid(zq)
        a = jnp.exp(-jnp.abs(z))
        r = 1.0 / (1.0 + a)
        pos = z >= 0
        g = jnp.log(lb + (1.0 - lb) * jnp.where(pos, r, a * r))
        k = (1.0 - lb) * jnp.where(pos, a * r, r)
        g_hi = g.astype(BF16)
        g_lo = (g - g_hi.astype(F32)).astype(BF16)
        x2 = _dot(msum_ref[...], jnp.concatenate([g_hi, g_lo], axis=1))
        e_all = jnp.exp(x2[:, :REC_DIM] + x2[:, REC_DIM:])
        e_b = e_all[0:C]
        e_e = e_all[C:2 * C]
        st = st_ref[hd]
        o = _dot_nt((qp * e_b).astype(BF16), st.astype(BF16))
        kb = k.astype(BF16)
        amat = pmask_ref[len(levels)] * _dot_nt(qp.astype(BF16), kb)
        for li, s in enumerate(levels):
            e_l = e_all[(2 + li) * C:(3 + li) * C]
            upper = (row & s) != 0
            ql = jnp.where(upper, qp * e_l, 0.0).astype(BF16)
            kl = jnp.where(upper, 0.0, k * e_l).astype(BF16)
            amat = amat + pmask_ref[li] * _dot_nt(ql, kl)
        o = o + _dot(amat.astype(BF16), v)
        st_ref[hd] = st * e_b[C - 1:C, :] + _dot_tn(v, (k * e_e).astype(BF16))
        og_act = og * _sigmoid(og)
        o_ref[:, cs] = (_rms(o, gain_ref[...]) * og_act).astype(BF16)


def _hgrn2(zr, zf, lb, gain, msum, pmask, B, S):
    C = min(REC_CHUNK, S)
    nc = S // C
    row = lambda b, c: (b * nc + c, 0)
    return pl.pallas_call(
        _hgrn_kernel,
        grid=(B, nc),
        in_specs=[
            pl.BlockSpec((C, 3 * REC_WIDTH), row),
            pl.BlockSpec((C, REC_WIDTH), row),
            pl.BlockSpec((1, REC_WIDTH), lambda b, c: (0, 0)),
            pl.BlockSpec((1, REC_DIM), lambda b, c: (0, 0)),
            pl.BlockSpec(msum.shape, lambda b, c: (0, 0)),
            pl.BlockSpec(pmask.shape, lambda b, c: (0, 0, 0)),
        ],
        out_specs=pl.BlockSpec((C, REC_WIDTH), row),
        out_shape=jax.ShapeDtypeStruct((B * S, REC_WIDTH), BF16),
        scratch_shapes=[pltpu.VMEM((REC_HEADS, REC_DIM, REC_DIM), F32)],
        compiler_params=pltpu.CompilerParams(dimension_semantics=("parallel", "arbitrary")),
        name="hgrn2",
    )(zr, zf, lb, gain, msum, pmask)


def _mix_out_kernel(attn_ref, rec_ref, h_ref, wo_ref, gain_ref, wrh_ref, wrl_ref, tril_ref,
                    h1_ref, xn_ref, ri_ref, rw_ref, cnt_ref, carry_ref):
    @pl.when(pl.program_id(0) == 0)
    def _():
        carry_ref[...] = jnp.zeros_like(carry_ref)

    h1 = (h_ref[...] + _dot(attn_ref[...], wo_ref[0:ATTN_WIDTH, :])
          + _dot(rec_ref[...], wo_ref[ATTN_WIDTH:ATTN_WIDTH + REC_WIDTH, :]))
    h1_ref[...] = h1
    xn = _rms(h1, gain_ref[...])
    _store_row_tiles(xn_ref, xn)
    xh = xn.astype(BF16)
    xl = (xn - xh.astype(F32)).astype(BF16)
    logits = _dot(xh, wrh_ref[...]) + _dot(xl, wrh_ref[...]) + _dot(xh, wrl_ref[...])
    lane = lax.broadcasted_iota(I32, logits.shape, 1)
    lanef = lane.astype(F32)
    neg = jnp.float32(-jnp.inf)
    big = jnp.float32(1e9)
    gl = jnp.where(lane < N_GROUPS, logits, neg)
    gmax = jnp.max(gl, axis=-1, keepdims=True)
    gidx = jnp.min(jnp.where(gl == gmax, lanef, big), axis=-1, keepdims=True)
    p_group = 1.0 / jnp.sum(jnp.where(lane < N_GROUPS, jnp.exp(logits - gmax), 0.0), axis=-1, keepdims=True)
    lo = N_GROUPS + gidx * EXPERTS_PER_GROUP
    el = jnp.where((lanef >= lo) & (lanef < lo + EXPERTS_PER_GROUP), logits, neg)
    t1 = jnp.max(el, axis=-1, keepdims=True)
    i1 = jnp.min(jnp.where(el == t1, lanef, big), axis=-1, keepdims=True)
    el2 = jnp.where(lanef == i1, neg, el)
    t2 = jnp.max(el2, axis=-1, keepdims=True)
    i2 = jnp.min(jnp.where(el2 == t2, lanef, big), axis=-1, keepdims=True)
    r21 = jnp.exp(t2 - t1)
    w1 = p_group / (1.0 + r21)
    w2 = w1 * r21
    e1 = i1 - N_GROUPS
    e2 = i2 - N_GROUPS
    oh1 = (lanef == e1).astype(F32)
    oh2 = (lanef == e2).astype(F32)
    oh = oh1 + oh2
    prefix = _dot(tril_ref[...], oh.astype(BF16)) + carry_ref[...]
    rank1 = jnp.sum(prefix * oh1, axis=-1, keepdims=True)
    rank2 = jnp.sum(prefix * oh2, axis=-1, keepdims=True)
    carry = carry_ref[...] + jnp.sum(oh, axis=0, keepdims=True)
    carry_ref[...] = carry
    cnt_ref[...] = jnp.broadcast_to(carry, cnt_ref.shape)
    ri = jnp.where(lane == 0, e1, jnp.where(lane == 1, e2, jnp.where(lane == 2, rank1, jnp.where(lane == 3, rank2, 0.0))))
    ri_ref[...] = ri.astype(I32)
    rw_ref[...] = jnp.where(lane == 0, w1, jnp.where(lane == 1, w2, 0.0))


def _mix_out(attn, rec, h, wo, gain, wrh, wrl, tril):
    T = h.shape[0]
    tm = tril.shape[0]
    row = lambda i: (i, 0)
    fixed = lambda i: (0, 0)
    return pl.pallas_call(
        _mix_out_kernel,
        grid=(T // tm,),
        in_specs=[
            pl.BlockSpec((tm, ATTN_WIDTH), row),
            pl.BlockSpec((tm, REC_WIDTH), row),
            pl.BlockSpec((tm, D_MODEL), row),
            pl.BlockSpec((ATTN_WIDTH + REC_WIDTH, D_MODEL), fixed),
            pl.BlockSpec((1, D_MODEL), fixed),
            pl.BlockSpec((D_MODEL, ROUTE_LANES), fixed),
            pl.BlockSpec((D_MODEL, ROUTE_LANES), fixed),
            pl.BlockSpec((tm, tm), fixed),
        ],
        out_specs=[
            pl.BlockSpec((tm, D_MODEL), row),
            pl.BlockSpec((tm, ROW_SUBLANES, LANES), lambda i: (i, 0, 0)),
            pl.BlockSpec((tm, ROUTE_LANES), row),
            pl.BlockSpec((tm, ROUTE_LANES), row),
            pl.BlockSpec((8, ROUTE_LANES), fixed),
        ],
        out_shape=[
            jax.ShapeDtypeStruct((T, D_MODEL), F32),
            jax.ShapeDtypeStruct((T, ROW_SUBLANES, LANES), F32),
            jax.ShapeDtypeStruct((T, ROUTE_LANES), I32),
            jax.ShapeDtypeStruct((T, ROUTE_LANES), F32),
            jax.ShapeDtypeStruct((8, ROUTE_LANES), F32),
        ],
        scratch_shapes=[pltpu.VMEM((1, ROUTE_LANES), F32)],
        compiler_params=pltpu.CompilerParams(dimension_semantics=("arbitrary",)),
        name="mix_out_router",
    )(attn, rec, h, wo, gain, wrh, wrl, tril)


def _row_copy_kernel(dest_ref, src_ref, *rest, tokens, scatter):
    out_ref, sem = rest[-2:]
    base = pl.program_id(0) * tokens

    def copy(a):
        if scatter:
            return pltpu.make_async_copy(src_ref.at[a // 2], out_ref.at[dest_ref[a]], sem)
        return pltpu.make_async_copy(src_ref.at[dest_ref[a]], out_ref.at[a], sem)

    def start(t, carry):
        copy(2 * (base + t)).start()
        copy(2 * (base + t) + 1).start()
        return carry

    def wait(t, carry):
        copy(2 * (base + t)).wait()
        copy(2 * (base + t) + 1).wait()
        return carry

    lax.fori_loop(0, tokens, start, 0, unroll=COPY_UNROLL)
    lax.fori_loop(0, tokens, wait, 0, unroll=COPY_UNROLL)


def _row_copy(dest, src, n_out_rows, *, scatter):
    n_assign = dest.shape[0]
    tokens = min(COPY_TOKENS, n_assign // 2)
    out_shape = jax.ShapeDtypeStruct((n_out_rows,) + src.shape[1:], src.dtype)
    operands = (dest, src, jnp.zeros(out_shape.shape, out_shape.dtype)) if scatter else (dest, src)
    return pl.pallas_call(
        functools.partial(_row_copy_kernel, tokens=tokens, scatter=scatter),
        grid_spec=pltpu.PrefetchScalarGridSpec(
            num_scalar_prefetch=1,
            grid=(n_assign // (2 * tokens),),
            in_specs=[pl.BlockSpec(memory_space=pl.ANY)] * (len(operands) - 1),
            out_specs=pl.BlockSpec(memory_space=pl.ANY),
            scratch_shapes=[pltpu.SemaphoreType.DMA],
        ),
        out_shape=out_shape,
        input_output_aliases={2: 0} if scatter else {},
        compiler_params=pltpu.CompilerParams(dimension_semantics=("arbitrary",)),
        name="moe_scatter_rows" if scatter else "moe_gather_rows",
    )(*operands)


def _expert_kernel(be_ref, nused_ref, x_ref, wg_ref, wu_ref, wd_ref, y_ref):
    del be_ref
    used = pl.program_id(0) < nused_ref[0]

    @pl.when(used)
    def _():
        x = _load_row_tiles(x_ref).astype(BF16)
        gate = _dot(x, wg_ref[0])
        up = _dot(x, wu_ref[0])
        hidden = (gate * _sigmoid(gate) * up).astype(BF16)
        _store_row_tiles(y_ref, _dot(hidden, wd_ref[0]))

    @pl.when(jnp.logical_not(used))
    def _():
        y_ref[...] = jnp.zeros_like(y_ref)


def _experts(block_expert, n_used, xs, wg, wu, wd, layer):
    n_rows = xs.shape[0]
    nblk = n_rows // EXPERT_BLOCK

    def xmap(i, be, nu):
        return (jnp.minimum(i, nu[0] - 1), 0, 0)

    def wmap(i, be, nu):
        return (layer * N_EXPERTS + be[jnp.minimum(i, nu[0] - 1)], 0, 0)

    return pl.pallas_call(
        _expert_kernel,
        grid_spec=pltpu.PrefetchScalarGridSpec(
            num_scalar_prefetch=2,
            grid=(nblk,),
            in_specs=[
                pl.BlockSpec((EXPERT_BLOCK, ROW_SUBLANES, LANES), xmap),
                pl.BlockSpec((1, D_MODEL, EXPERT_FF), wmap),
                pl.BlockSpec((1, D_MODEL, EXPERT_FF), wmap),
                pl.BlockSpec((1, EXPERT_FF, D_MODEL), wmap),
            ],
            out_specs=pl.BlockSpec((EXPERT_BLOCK, ROW_SUBLANES, LANES), lambda i, be, nu: (i, 0, 0)),
        ),
        out_shape=jax.ShapeDtypeStruct(xs.shape, F32),
        compiler_params=pltpu.CompilerParams(dimension_semantics=("arbitrary",)),
        name="moe_experts",
    )(block_expert, n_used, xs, wg, wu, wd)


def _ple_kernel(h1_ref, yg_ref, rw_ref, p_ref, wple_ref, pgain_ref, ggain_ref, wpg_ref, o_ref):
    rw = rw_ref[...]
    tm = rw.shape[0]
    y1 = _load_row_tiles(yg_ref, pl.ds(0, tm, stride=2))
    y2 = _load_row_tiles(yg_ref, pl.ds(1, tm, stride=2))
    h2 = h1_ref[...] + rw[:, 0:1] * y1 + rw[:, 1:2] * y2
    ple = _rms(_dot(p_ref[...].astype(BF16), wple_ref[...]), pgain_ref[...])
    gate = _sigmoid(_dot(_rms(h2, ggain_ref[...]).astype(BF16), wpg_ref[...]))
    o_ref[...] = h2 + ple * gate


def _ple(h1, yg, rw, p, wple, pgain, ggain, wpg, layer):
    T = h1.shape[0]
    tm = min(TOKEN_TILE, T)
    nt = T // tm
    row = lambda i: (i, 0)
    fixed = lambda i: (0, 0)
    return pl.pallas_call(
        _ple_kernel,
        grid=(nt,),
        in_specs=[
            pl.BlockSpec((tm, D_MODEL), row),
            pl.BlockSpec((2 * tm, ROW_SUBLANES, LANES), lambda i: (i, 0, 0)),
            pl.BlockSpec((tm, ROUTE_LANES), row),
            pl.BlockSpec((tm, PLE_DIM), lambda i: (layer * nt + i, 0)),
            pl.BlockSpec((PLE_DIM, D_MODEL), fixed),
            pl.BlockSpec((1, D_MODEL), fixed),
            pl.BlockSpec((1, D_MODEL), fixed),
            pl.BlockSpec((D_MODEL, D_MODEL), fixed),
        ],
        out_specs=pl.BlockSpec((tm, D_MODEL), row),
        out_shape=jax.ShapeDtypeStruct((T, D_MODEL), F32),
        compiler_params=pltpu.CompilerParams(dimension_semantics=("parallel",)),
        name="combine_ple",
    )(h1, yg, rw, p, wple, pgain, ggain, wpg)


def _rope_tables(positions):
    inv_freq = ROPE_THETA ** (-jnp.arange(0, ROPE_DIM, 2, dtype=F32) / ROPE_DIM)
    ang = positions.astype(F32).reshape(-1, 1) * inv_freq
    cos, sin = jnp.cos(ang), jnp.sin(ang)
    T = ang.shape[0]
    rest = jnp.zeros((T, ATTN_HEAD_DIM - ROPE_DIM), F32)
    zero = jnp.zeros((T, ROPE_HALF), F32)
    rc = jnp.concatenate([cos, cos, rest + 1.0], axis=1)
    rs1 = jnp.concatenate([-sin, zero, rest], axis=1)
    rs2 = jnp.concatenate([zero, sin, rest], axis=1)
    reps = LANES // ATTN_HEAD_DIM
    return tuple(jnp.tile(t, (1, reps)) for t in (rc, rs1, rs2))


def kernel(x, p, positions, mix_norm, w_in, q_norm, k_norm, sinks, lb_logits, rec_norm, w_out, ffn_norm,
           w_router_group, w_router_expert, w_gate, w_up, w_down, w_ple, ple_norm, ple_gate_norm, w_ple_gate):
    B, S, D = x.shape
    depth = w_in.shape[0]
    T = B * S
    n_assign = 2 * T
    assert D == D_MODEL and S % WINDOW == 0 and T % min(TOKEN_TILE, T) == 0

    rc, rs1, rs2 = _rope_tables(positions)
    lb_sm = jax.nn.softmax(lb_logits.astype(F32), axis=0)
    lower_bounds = jnp.cumsum(lb_sm, axis=0) - lb_sm[0:1]

    a0 = QKV_WIDTH
    w_in_p = jnp.concatenate([w_in[:, :, :a0 + REC_WIDTH], w_in[:, :, a0 + 2 * REC_WIDTH:],
                              w_in[:, :, a0 + REC_WIDTH:a0 + 2 * REC_WIDTH]], axis=2).astype(BF16)
    w_out_b = w_out.astype(BF16)
    wg_b = w_gate.astype(BF16).reshape(depth * N_EXPERTS, D_MODEL, EXPERT_FF)
    wu_b = w_up.astype(BF16).reshape(depth * N_EXPERTS, D_MODEL, EXPERT_FF)
    wd_b = w_down.astype(BF16).reshape(depth * N_EXPERTS, EXPERT_FF, D_MODEL)
    w_ple_b = w_ple.astype(BF16)
    w_pg_b = w_ple_gate.astype(BF16)
    w_r = jnp.concatenate([w_router_group, w_router_expert,
                           jnp.zeros((depth, D_MODEL, ROUTE_LANES - N_GROUPS - N_EXPERTS), F32)], axis=2)
    w_r_hi = w_r.astype(BF16)
    w_r_lo = (w_r - w_r_hi.astype(F32)).astype(BF16)
    qk_gain = jnp.concatenate([jnp.tile(q_norm, (1, ATTN_HEADS)) * (ATTN_HEAD_DIM ** -0.5),
                               jnp.tile(k_norm, (1, ATTN_KV_HEADS))], axis=1)
    seg_id = np.arange(QK_WIDTH) // ATTN_HEAD_DIM
    seg = jnp.asarray(seg_id[:, None] == seg_id[None, :], BF16)
    msum_np, pmask_np = _hgrn_constants(min(REC_CHUNK, S))
    msum = jnp.asarray(msum_np, BF16)
    pmask = jnp.asarray(pmask_np, F32)
    tm = min(TOKEN_TILE, T)
    tril = jnp.asarray(np.tril(np.ones((tm, tm), np.float32), -1), BF16)

    n_rows = n_assign + N_EXPERTS * EXPERT_BLOCK
    nblk = n_rows // EXPERT_BLOCK
    eids = jnp.arange(N_EXPERTS, dtype=I32)
    p2 = p.reshape(depth * T, PLE_DIM)

    h = x.reshape(T, D)
    for l in range(depth):
        za, zr, zf = _mix_in(h, mix_norm[l][None], w_in_p[l], rc, rs1, rs2, qk_gain[l][None], seg)
        attn = _attention(za, sinks[l], B, S)
        rec = _hgrn2(zr, zf, lower_bounds[l][None], rec_norm[l][None], msum, pmask, B, S)
        h1, xn, ri, rw, cnt = _mix_out(attn, rec, h, w_out_b[l], ffn_norm[l][None], w_r_hi[l], w_r_lo[l], tril)
        counts = cnt[0, :N_EXPERTS].astype(I32)
        padded = ((counts + EXPERT_BLOCK - 1) // EXPERT_BLOCK) * EXPERT_BLOCK
        pad_end = jnp.cumsum(padded)
        pad_start = pad_end - padded
        expert = ri[:, 0:2]
        dest = ri[:, 2:4] + jnp.sum(jnp.where(expert[:, :, None] == eids, pad_start, 0), axis=-1)
        dest = dest.reshape(n_assign).astype(I32)
        n_used = (pad_end[-1] // EXPERT_BLOCK).astype(I32).reshape(1)
        blk_start = jnp.arange(nblk, dtype=I32) * EXPERT_BLOCK
        block_expert = jnp.minimum(jnp.sum(pad_end[None, :] <= blk_start[:, None], axis=1), N_EXPERTS - 1).astype(I32)
        xs = _row_copy(dest, xn, n_rows, scatter=True)
        ys = _experts(block_expert, n_used, xs, wg_b, wu_b, wd_b, l)
        yg = _row_copy(dest, ys, n_assign, scatter=False)
        h = _ple(h1, yg, rw, p2, w_ple_b[l], ple_norm[l][None], ple_gate_norm[l][None],
                 w_pg_b[l], l)
    return h.reshape(B, S, D)
```

```python
import functools

import jax
import jax.numpy as jnp
import numpy as np
from jax import lax
from jax.experimental import pallas as pl
from jax.experimental.pallas import tpu as pltpu

F32 = jnp.float32
BF16 = jnp.bfloat16
I32 = jnp.int32

D_MODEL = 1024
ATTN_HEADS = 8
ATTN_KV_HEADS = 2
ATTN_HEAD_DIM = 64
ATTN_GROUP = ATTN_HEADS // ATTN_KV_HEADS
ATTN_WIDTH = ATTN_HEADS * ATTN_HEAD_DIM
KV_WIDTH = ATTN_KV_HEADS * ATTN_HEAD_DIM
QK_WIDTH = ATTN_WIDTH + KV_WIDTH
QKV_WIDTH = ATTN_WIDTH + 2 * KV_WIDTH
WINDOW = 128
ROPE_THETA = 500000.0
ROPE_DIM = ATTN_HEAD_DIM // 4
ROPE_HALF = ROPE_DIM // 2
REC_HEADS = 4
REC_DIM = 128
REC_WIDTH = REC_HEADS * REC_DIM
REC_CHUNK = 128
IN_WIDTH = QKV_WIDTH + 4 * REC_WIDTH
N_GROUPS = 4
EXPERTS_PER_GROUP = 8
N_EXPERTS = N_GROUPS * EXPERTS_PER_GROUP
EXPERT_FF = 512
PLE_DIM = 256
RMS_EPS = 1e-6
MASK_VALUE = -1e30
LANES = 128
ROUTE_LANES = LANES
ROW_SUBLANES = D_MODEL // LANES

TOKEN_TILE = 512
EXPERT_BLOCK = 256
COPY_TOKENS = 512
COPY_UNROLL = 8


def _dot(a, b):
    return jnp.dot(a, b, preferred_element_type=F32)


def _dot_nt(a, b):
    return lax.dot_general(a, b, (((1,), (1,)), ((), ())), preferred_element_type=F32)


def _dot_tn(a, b):
    return lax.dot_general(a, b, (((0,), (0,)), ((), ())), preferred_element_type=F32)


def _store_row_tiles(ref, x):
    for c in range(ROW_SUBLANES):
        ref[:, c, :] = x[:, c * LANES:(c + 1) * LANES]


def _load_row_tiles(ref, rows=None):
    rows = slice(None) if rows is None else rows
    return jnp.concatenate([ref[rows, c, :] for c in range(ROW_SUBLANES)], axis=1)


def _sigmoid(x):
    return 1.0 / (1.0 + jnp.exp(-x))


def _rms(x, gain):
    ms = jnp.mean(x * x, axis=-1, keepdims=True)
    return x * lax.rsqrt(ms + RMS_EPS) * gain


def _mix_in_kernel(h_ref, gain_ref, w_ref, rc_ref, rs1_ref, rs2_ref, qkg_ref, seg_ref,
                   za_ref, zr_ref, zf_ref):
    xn = _rms(h_ref[...], gain_ref[...]).astype(BF16)
    z_a = _dot(xn, w_ref[:, 0:QKV_WIDTH])
    qk = z_a[:, 0:QK_WIDTH]
    seg = _dot((qk * qk).astype(BF16), seg_ref[...]) * (1.0 / ATTN_HEAD_DIM)
    qkn = qk * lax.rsqrt(seg + RMS_EPS) * qkg_ref[...]
    rc, rs1, rs2 = rc_ref[...], rs1_ref[...], rs2_ref[...]
    for c in range(QK_WIDTH // LANES):
        col = qkn[:, c * LANES:(c + 1) * LANES]
        rot = col * rc + pltpu.roll(col, LANES - ROPE_HALF, 1) * rs1 + pltpu.roll(col, ROPE_HALF, 1) * rs2
        za_ref[:, c * LANES:(c + 1) * LANES] = rot.astype(BF16)
    za_ref[:, QK_WIDTH:QKV_WIDTH] = z_a[:, QK_WIDTH:QKV_WIDTH].astype(BF16)
    zr_ref[...] = _dot(xn, w_ref[:, QKV_WIDTH:QKV_WIDTH + 3 * REC_WIDTH]).astype(BF16)
    zf_ref[...] = _dot(xn, w_ref[:, QKV_WIDTH + 3 * REC_WIDTH:IN_WIDTH])


def _mix_in(h, gain, w, rc, rs1, rs2, qkg, seg):
    T = h.shape[0]
    tm = min(TOKEN_TILE, T)
    row = lambda i: (i, 0)
    fixed = lambda i: (0, 0)
    return pl.pallas_call(
        _mix_in_kernel,
        grid=(T // tm,),
        in_specs=[
            pl.BlockSpec((tm, D_MODEL), row),
            pl.BlockSpec((1, D_MODEL), fixed),
            pl.BlockSpec((D_MODEL, IN_WIDTH), fixed),
            pl.BlockSpec((tm, LANES), row),
            pl.BlockSpec((tm, LANES), row),
            pl.BlockSpec((tm, LANES), row),
            pl.BlockSpec((1, QK_WIDTH), fixed),
            pl.BlockSpec((QK_WIDTH, QK_WIDTH), fixed),
        ],
        out_specs=[
            pl.BlockSpec((tm, QKV_WIDTH), row),
            pl.BlockSpec((tm, 3 * REC_WIDTH), row),
            pl.BlockSpec((tm, REC_WIDTH), row),
        ],
        out_shape=[
            jax.ShapeDtypeStruct((T, QKV_WIDTH), BF16),
            jax.ShapeDtypeStruct((T, 3 * REC_WIDTH), BF16),
            jax.ShapeDtypeStruct((T, REC_WIDTH), F32),
        ],
        compiler_params=pltpu.CompilerParams(dimension_semantics=("parallel",)),
        name="mix_in",
    )(h, gain, w, rc, rs1, rs2, qkg, seg)


def _attn_kernel(sink_ref, q_ref, kvc_ref, kvp_ref, o_ref):
    n = pl.program_id(1)
    q = q_ref[...]
    kvc = kvc_ref[...]
    kvp = kvp_ref[...]
    rows = lax.broadcasted_iota(I32, (ATTN_GROUP * WINDOW, 2 * WINDOW), 0)
    cols = lax.broadcasted_iota(I32, (ATTN_GROUP * WINDOW, 2 * WINDOW), 1)
    qi = rows & (WINDOW - 1)
    valid = (cols > qi) & (cols <= qi + WINDOW) & ((cols >= WINDOW) | (n > 0))
    grp = lax.broadcasted_iota(I32, (ATTN_GROUP * WINDOW, 1), 0) // WINDOW
    for j in range(ATTN_KV_HEADS):
        ks = slice(j * ATTN_HEAD_DIM, (j + 1) * ATTN_HEAD_DIM)
        vs = slice(KV_WIDTH + j * ATTN_HEAD_DIM, KV_WIDTH + (j + 1) * ATTN_HEAD_DIM)
        k = jnp.concatenate([kvp[:, ks], kvc[:, ks]], axis=0)
        v = jnp.concatenate([kvp[:, vs], kvc[:, vs]], axis=0)
        heads = [j * ATTN_GROUP + g for g in range(ATTN_GROUP)]
        q4 = jnp.concatenate([q[:, hh * ATTN_HEAD_DIM:(hh + 1) * ATTN_HEAD_DIM] for hh in heads], axis=0)
        s = jnp.where(valid, _dot_nt(q4, k), MASK_VALUE)
        sink = jnp.zeros((ATTN_GROUP * WINDOW, 1), F32)
        for g, hh in enumerate(heads):
            sink = jnp.where(grp == g, sink_ref[hh], sink)
        m = jnp.maximum(jnp.max(s, axis=-1, keepdims=True), sink)
        e = jnp.exp(s - m)
        denom = jnp.sum(e, axis=-1, keepdims=True) + jnp.exp(sink - m)
        p = (e * (1.0 / denom)).astype(BF16)
        o = _dot(p, v)
        for g, hh in enumerate(heads):
            o_ref[:, hh * ATTN_HEAD_DIM:(hh + 1) * ATTN_HEAD_DIM] = o[g * WINDOW:(g + 1) * WINDOW].astype(BF16)


def _attention(za, sinks, B, S):
    nb = S // WINDOW
    kvblk = ATTN_WIDTH // (2 * KV_WIDTH)
    return pl.pallas_call(
        _attn_kernel,
        grid=(B, nb),
        in_specs=[
            pl.BlockSpec(memory_space=pltpu.SMEM),
            pl.BlockSpec((WINDOW, ATTN_WIDTH), lambda b, n: (b * nb + n, 0)),
            pl.BlockSpec((WINDOW, 2 * KV_WIDTH), lambda b, n: (b * nb + n, kvblk)),
            pl.BlockSpec((WINDOW, 2 * KV_WIDTH), lambda b, n: (b * nb + jnp.maximum(n - 1, 0), kvblk)),
        ],
        out_specs=pl.BlockSpec((WINDOW, ATTN_WIDTH), lambda b, n: (b * nb + n, 0)),
        out_shape=jax.ShapeDtypeStruct((B * S, ATTN_WIDTH), BF16),
        compiler_params=pltpu.CompilerParams(dimension_semantics=("parallel", "arbitrary")),
        name="swa_attention",
    )(sinks, za, za, za)


def _hgrn_levels(C):
    out, s = [], C // 2
    while s >= 1:
        out.append(s)
        s //= 2
    return out


def _hgrn_constants(C):
    r = np.arange(C)[:, None]
    u = np.arange(C)[None, :]
    mats = [u <= r, u > r]
    masks = []
    for s in _hgrn_levels(C):
        mid = (r // (2 * s)) * 2 * s + s
        upper = r >= mid
        mats.append(np.where(upper, (u >= mid) & (u <= r), (u > r) & (u < mid)))
        masks.append((r // (2 * s)) == (u // (2 * s)))
    masks.append(r == u)
    return (np.concatenate(mats, axis=0).astype(np.float32), np.stack(masks).astype(np.float32))


def _hgrn_kernel(zr_ref, zf_ref, lb_ref, gain_ref, msum_ref, pmask_ref, o_ref, st_ref):
    C = zf_ref.shape[0]
    levels = _hgrn_levels(C)

    @pl.when(pl.program_id(1) == 0)
    def _():
        st_ref[...] = jnp.zeros_like(st_ref)

    row = lax.broadcasted_iota(I32, (C, REC_DIM), 0)
    for hd in range(REC_HEADS):
        cs = slice(hd * REC_DIM, (hd + 1) * REC_DIM)
        zq = zr_ref[:, cs].astype(F32)
        v = zr_ref[:, REC_WIDTH + hd * REC_DIM:REC_WIDTH + (hd + 1) * REC_DIM]
        og = zr_ref[:, 2 * REC_WIDTH + hd * REC_DIM:2 * REC_WIDTH + (hd + 1) * REC_DIM].astype(F32)
        z = zf_ref[:, cs]
        lb = lb_ref[:, cs]
        qp = zq * _sigmoid(zq)
        a = jnp.exp(-jnp.abs(z))
        r = 1.0 / (1.0 + a)
        pos = z >= 0
        g = jnp.log(lb + (1.0 - lb) * jnp.where(pos, r, a * r))
        k = (1.0 - lb) * jnp.where(pos, a * r, r)
        g_hi = g.astype(BF16)
        g_lo = (g - g_hi.astype(F32)).astype(BF16)
        x2 = _dot(msum_ref[...], jnp.concatenate([g_hi, g_lo], axis=1))
        e_all = jnp.exp(x2[:, :REC_DIM] + x2[:, REC_DIM:])
        e_b = e_all[0:C]
        e_e = e_all[C:2 * C]
        st = st_ref[hd]
        o = _dot_nt((qp * e_b).astype(BF16), st.astype(BF16))
        kb = k.astype(BF16)
        amat = pmask_ref[len(levels)] * _dot_nt(qp.astype(BF16), kb)
        for li, s in enumerate(levels):
            e_l = e_all[(2 + li) * C:(3 + li) * C]
            upper = (row & s) != 0
            ql = jnp.where(upper, qp * e_l, 0.0).astype(BF16)
            kl = jnp.where(upper, 0.0, k * e_l).astype(BF16)
            amat = amat + pmask_ref[li] * _dot_nt(ql, kl)
        o = o + _dot(amat.astype(BF16), v)
        st_ref[hd] = st * e_b[C - 1:C, :] + _dot_tn(v, (k * e_e).astype(BF16))
        og_act = og * _sigmoid(og)
        o_ref[:, cs] = (_rms(o, gain_ref[...]) * og_act).astype(BF16)


def _hgrn2(zr, zf, lb, gain, msum, pmask, B, S):
    C = min(REC_CHUNK, S)
    nc = S // C
    row = lambda b, c: (b * nc + c, 0)
    return pl.pallas_call(
        _hgrn_kernel,
        grid=(B, nc),
        in_specs=[
            pl.BlockSpec((C, 3 * REC_WIDTH), row),
            pl.BlockSpec((C, REC_WIDTH), row),
            pl.BlockSpec((1, REC_WIDTH), lambda b, c: (0, 0)),
            pl.BlockSpec((1, REC_DIM), lambda b, c: (0, 0)),
            pl.BlockSpec(msum.shape, lambda b, c: (0, 0)),
            pl.BlockSpec(pmask.shape, lambda b, c: (0, 0, 0)),
        ],
        out_specs=pl.BlockSpec((C, REC_WIDTH), row),
        out_shape=jax.ShapeDtypeStruct((B * S, REC_WIDTH), BF16),
        scratch_shapes=[pltpu.VMEM((REC_HEADS, REC_DIM, REC_DIM), F32)],
        compiler_params=pltpu.CompilerParams(dimension_semantics=("parallel", "arbitrary")),
        name="hgrn2",
    )(zr, zf, lb, gain, msum, pmask)


def _mix_out_kernel(attn_ref, rec_ref, h_ref, wo_ref, gain_ref, wrh_ref, wrl_ref, tril_ref,
                    h1_ref, xn_ref, ri_ref, rw_ref, cnt_ref, carry_ref):
    @pl.when(pl.program_id(0) == 0)
    def _():
        carry_ref[...] = jnp.zeros_like(carry_ref)

    h1 = (h_ref[...] + _dot(attn_ref[...], wo_ref[0:ATTN_WIDTH, :])
          + _dot(rec_ref[...], wo_ref[ATTN_WIDTH:ATTN_WIDTH + REC_WIDTH, :]))
    h1_ref[...] = h1
    xn = _rms(h1, gain_ref[...])
    _store_row_tiles(xn_ref, xn)
    xh = xn.astype(BF16)
    xl = (xn - xh.astype(F32)).astype(BF16)
    logits = _dot(xh, wrh_ref[...]) + _dot(xl, wrh_ref[...]) + _dot(xh, wrl_ref[...])
    lane = lax.broadcasted_iota(I32, logits.shape, 1)
    lanef = lane.astype(F32)
    neg = jnp.float32(-jnp.inf)
    big = jnp.float32(1e9)
    gl = jnp.where(lane < N_GROUPS, logits, neg)
    gmax = jnp.max(gl, axis=-1, keepdims=True)
    gidx = jnp.min(jnp.where(gl == gmax, lanef, big), axis=-1, keepdims=True)
    p_group = 1.0 / jnp.sum(jnp.where(lane < N_GROUPS, jnp.exp(logits - gmax), 0.0), axis=-1, keepdims=True)
    lo = N_GROUPS + gidx * EXPERTS_PER_GROUP
    el = jnp.where((lanef >= lo) & (lanef < lo + EXPERTS_PER_GROUP), logits, neg)
    t1 = jnp.max(el, axis=-1, keepdims=True)
    i1 = jnp.min(jnp.where(el == t1, lanef, big), axis=-1, keepdims=True)
    el2 = jnp.where(lanef == i1, neg, el)
    t2 = jnp.max(el2, axis=-1, keepdims=True)
    i2 = jnp.min(jnp.where(el2 == t2, lanef, big), axis=-1, keepdims=True)
    r21 = jnp.exp(t2 - t1)
    w1 = p_group / (1.0 + r21)
    w2 = w1 * r21
    e1 = i1 - N_GROUPS
    e2 = i2 - N_GROUPS
    oh1 = (lanef == e1).astype(F32)
    oh2 = (lanef == e2).astype(F32)
    oh = oh1 + oh2
    prefix = _dot(tril_ref[...], oh.astype(BF16)) + carry_ref[...]
    rank1 = jnp.sum(prefix * oh1, axis=-1, keepdims=True)
    rank2 = jnp.sum(prefix * oh2, axis=-1, keepdims=True)
    carry = carry_ref[...] + jnp.sum(oh, axis=0, keepdims=True)
    carry_ref[...] = carry
    cnt_ref[...] = jnp.broadcast_to(carry, cnt_ref.shape)
    ri = jnp.where(lane == 0, e1, jnp.where(lane == 1, e2, jnp.where(lane == 2, rank1, jnp.where(lane == 3, rank2, 0.0))))
    ri_ref[...] = ri.astype(I32)
    rw_ref[...] = jnp.where(lane == 0, w1, jnp.where(lane == 1, w2, 0.0))


def _mix_out(attn, rec, h, wo, gain, wrh, wrl, tril):
    T = h.shape[0]
    tm = tril.shape[0]
    row = lambda i: (i, 0)
    fixed = lambda i: (0, 0)
    return pl.pallas_call(
        _mix_out_kernel,
        grid=(T // tm,),
        in_specs=[
            pl.BlockSpec((tm, ATTN_WIDTH), row),
            pl.BlockSpec((tm, REC_WIDTH), row),
            pl.BlockSpec((tm, D_MODEL), row),
            pl.BlockSpec((ATTN_WIDTH + REC_WIDTH, D_MODEL), fixed),
            pl.BlockSpec((1, D_MODEL), fixed),
            pl.BlockSpec((D_MODEL, ROUTE_LANES), fixed),
            pl.BlockSpec((D_MODEL, ROUTE_LANES), fixed),
            pl.BlockSpec((tm, tm), fixed),
        ],
        out_specs=[
            pl.BlockSpec((tm, D_MODEL), row),
            pl.BlockSpec((tm, ROW_SUBLANES, LANES), lambda i: (i, 0, 0)),
            pl.BlockSpec((tm, ROUTE_LANES), row),
            pl.BlockSpec((tm, ROUTE_LANES), row),
            pl.BlockSpec((8, ROUTE_LANES), fixed),
        ],
        out_shape=[
            jax.ShapeDtypeStruct((T, D_MODEL), F32),
            jax.ShapeDtypeStruct((T, ROW_SUBLANES, LANES), F32),
            jax.ShapeDtypeStruct((T, ROUTE_LANES), I32),
            jax.ShapeDtypeStruct((T, ROUTE_LANES), F32),
            jax.ShapeDtypeStruct((8, ROUTE_LANES), F32),
        ],
        scratch_shapes=[pltpu.VMEM((1, ROUTE_LANES), F32)],
        compiler_params=pltpu.CompilerParams(dimension_semantics=("arbitrary",)),
        name="mix_out_router",
    )(attn, rec, h, wo, gain, wrh, wrl, tril)


def _row_move_loops(copy, tokens):
    def start(t, carry):
        copy(t, 0).start()
        copy(t, 1).start()
        return carry

    def wait(t, carry):
        copy(t, 0).wait()
        copy(t, 1).wait()
        return carry

    lax.fori_loop(0, tokens, start, 0, unroll=COPY_UNROLL)
    lax.fori_loop(0, tokens, wait, 0, unroll=COPY_UNROLL)


def _dispatch_kernel(dest_ref, x_ref, init_ref, out_ref, sem, *, tokens):
    del init_ref
    base = 2 * pl.program_id(0) * tokens

    def copy(t, j):
        return pltpu.make_async_copy(x_ref.at[t], out_ref.at[dest_ref[base + 2 * t + j]], sem)

    _row_move_loops(copy, tokens)


def _collect_kernel(dest_ref, y_ref, out_ref, sem, *, tokens):
    base = 2 * pl.program_id(0) * tokens

    def copy(t, j):
        return pltpu.make_async_copy(y_ref.at[dest_ref[base + 2 * t + j]], out_ref.at[2 * t + j], sem)

    _row_move_loops(copy, tokens)


def _dispatch(dest, x3, n_rows):
    tokens = min(COPY_TOKENS, x3.shape[0])
    tile = x3.shape[1:]
    return pl.pallas_call(
        functools.partial(_dispatch_kernel, tokens=tokens),
        grid_spec=pltpu.PrefetchScalarGridSpec(
            num_scalar_prefetch=1,
            grid=(x3.shape[0] // tokens,),
            in_specs=[pl.BlockSpec((tokens,) + tile, lambda i, d: (i, 0, 0)), pl.BlockSpec(memory_space=pl.ANY)],
            out_specs=pl.BlockSpec(memory_space=pl.ANY),
            scratch_shapes=[pltpu.SemaphoreType.DMA],
        ),
        out_shape=jax.ShapeDtypeStruct((n_rows,) + tile, x3.dtype),
        input_output_aliases={2: 0},
        compiler_params=pltpu.CompilerParams(dimension_semantics=("arbitrary",)),
        name="moe_dispatch_rows",
    )(dest, x3, jnp.zeros((n_rows,) + tile, x3.dtype))


def _collect(dest, y3):
    n_assign = dest.shape[0]
    tokens = min(COPY_TOKENS, n_assign // 2)
    tile = y3.shape[1:]
    return pl.pallas_call(
        functools.partial(_collect_kernel, tokens=tokens),
        grid_spec=pltpu.PrefetchScalarGridSpec(
            num_scalar_prefetch=1,
            grid=(n_assign // (2 * tokens),),
            in_specs=[pl.BlockSpec(memory_space=pl.ANY)],
            out_specs=pl.BlockSpec((2 * tokens,) + tile, lambda i, d: (i, 0, 0)),
            scratch_shapes=[pltpu.SemaphoreType.DMA],
        ),
        out_shape=jax.ShapeDtypeStruct((n_assign,) + tile, y3.dtype),
        compiler_params=pltpu.CompilerParams(dimension_semantics=("arbitrary",)),
        name="moe_collect_rows",
    )(dest, y3)


def _expert_kernel(be_ref, nused_ref, x_ref, wg_ref, wu_ref, wd_ref, y_ref):
    del be_ref
    used = pl.program_id(0) < nused_ref[0]

    @pl.when(used)
    def _():
        x = _load_row_tiles(x_ref).astype(BF16)
        gate = _dot(x, wg_ref[0])
        up = _dot(x, wu_ref[0])
        hidden = (gate * _sigmoid(gate) * up).astype(BF16)
        _store_row_tiles(y_ref, _dot(hidden, wd_ref[0]))

    @pl.when(jnp.logical_not(used))
    def _():
        y_ref[...] = jnp.zeros_like(y_ref)


def _experts(block_expert, n_used, xs, wg, wu, wd, layer):
    n_rows = xs.shape[0]
    nblk = n_rows // EXPERT_BLOCK

    def xmap(i, be, nu):
        return (jnp.minimum(i, nu[0] - 1), 0, 0)

    def wmap(i, be, nu):
        return (layer * N_EXPERTS + be[jnp.minimum(i, nu[0] - 1)], 0, 0)

    return pl.pallas_call(
        _expert_kernel,
        grid_spec=pltpu.PrefetchScalarGridSpec(
            num_scalar_prefetch=2,
            grid=(nblk,),
            in_specs=[
                pl.BlockSpec((EXPERT_BLOCK, ROW_SUBLANES, LANES), xmap),
                pl.BlockSpec((1, D_MODEL, EXPERT_FF), wmap),
                pl.BlockSpec((1, D_MODEL, EXPERT_FF), wmap),
                pl.BlockSpec((1, EXPERT_FF, D_MODEL), wmap),
            ],
            out_specs=pl.BlockSpec((EXPERT_BLOCK, ROW_SUBLANES, LANES), lambda i, be, nu: (i, 0, 0)),
        ),
        out_shape=jax.ShapeDtypeStruct(xs.shape, F32),
        compiler_params=pltpu.CompilerParams(dimension_semantics=("arbitrary",)),
        name="moe_experts",
    )(block_expert, n_used, xs, wg, wu, wd)


def _ple_kernel(h1_ref, yg_ref, rw_ref, p_ref, wple_ref, pgain_ref, ggain_ref, wpg_ref, o_ref):
    rw = rw_ref[...]
    tm = rw.shape[0]
    y1 = _load_row_tiles(yg_ref, pl.ds(0, tm, stride=2))
    y2 = _load_row_tiles(yg_ref, pl.ds(1, tm, stride=2))
    h2 = h1_ref[...] + rw[:, 0:1] * y1 + rw[:, 1:2] * y2
    ple = _rms(_dot(p_ref[...].astype(BF16), wple_ref[...]), pgain_ref[...])
    gate = _sigmoid(_dot(_rms(h2, ggain_ref[...]).astype(BF16), wpg_ref[...]))
    o_ref[...] = h2 + ple * gate


def _ple(h1, yg, rw, p, wple, pgain, ggain, wpg, layer):
    T = h1.shape[0]
    tm = min(TOKEN_TILE, T)
    nt = T // tm
    row = lambda i: (i, 0)
    fixed = lambda i: (0, 0)
    return pl.pallas_call(
        _ple_kernel,
        grid=(nt,),
        in_specs=[
            pl.BlockSpec((tm, D_MODEL), row),
            pl.BlockSpec((2 * tm, ROW_SUBLANES, LANES), lambda i: (i, 0, 0)),
            pl.BlockSpec((tm, ROUTE_LANES), row),
            pl.BlockSpec((tm, PLE_DIM), lambda i: (layer * nt + i, 0)),
            pl.BlockSpec((PLE_DIM, D_MODEL), fixed),
            pl.BlockSpec((1, D_MODEL), fixed),
            pl.BlockSpec((1, D_MODEL), fixed),
            pl.BlockSpec((D_MODEL, D_MODEL), fixed),
        ],
        out_specs=pl.BlockSpec((tm, D_MODEL), row),
        out_shape=jax.ShapeDtypeStruct((T, D_MODEL), F32),
        compiler_params=pltpu.CompilerParams(dimension_semantics=("parallel",)),
        name="combine_ple",
    )(h1, yg, rw, p, wple, pgain, ggain, wpg)


def _rope_tables(positions):
    inv_freq = ROPE_THETA ** (-jnp.arange(0, ROPE_DIM, 2, dtype=F32) / ROPE_DIM)
    ang = positions.astype(F32).reshape(-1, 1) * inv_freq
    cos, sin = jnp.cos(ang), jnp.sin(ang)
    T = ang.shape[0]
    rest = jnp.zeros((T, ATTN_HEAD_DIM - ROPE_DIM), F32)
    zero = jnp.zeros((T, ROPE_HALF), F32)
    rc = jnp.concatenate([cos, cos, rest + 1.0], axis=1)
    rs1 = jnp.concatenate([-sin, zero, rest], axis=1)
    rs2 = jnp.concatenate([zero, sin, rest], axis=1)
    reps = LANES // ATTN_HEAD_DIM
    return tuple(jnp.tile(t, (1, reps)) for t in (rc, rs1, rs2))


def kernel(x, p, positions, mix_norm, w_in, q_norm, k_norm, sinks, lb_logits, rec_norm, w_out, ffn_norm,
           w_router_group, w_router_expert, w_gate, w_up, w_down, w_ple, ple_norm, ple_gate_norm, w_ple_gate):
    B, S, D = x.shape
    depth = w_in.shape[0]
    T = B * S
    n_assign = 2 * T
    assert D == D_MODEL and S % WINDOW == 0 and T % min(TOKEN_TILE, T) == 0

    rc, rs1, rs2 = _rope_tables(positions)
    lb_sm = jax.nn.softmax(lb_logits.astype(F32), axis=0)
    lower_bounds = jnp.cumsum(lb_sm, axis=0) - lb_sm[0:1]

    a0 = QKV_WIDTH
    w_in_p = jnp.concatenate([w_in[:, :, :a0 + REC_WIDTH], w_in[:, :, a0 + 2 * REC_WIDTH:],
                              w_in[:, :, a0 + REC_WIDTH:a0 + 2 * REC_WIDTH]], axis=2).astype(BF16)
    w_out_b = w_out.astype(BF16)
    wg_b = w_gate.astype(BF16).reshape(depth * N_EXPERTS, D_MODEL, EXPERT_FF)
    wu_b = w_up.astype(BF16).reshape(depth * N_EXPERTS, D_MODEL, EXPERT_FF)
    wd_b = w_down.astype(BF16).reshape(depth * N_EXPERTS, EXPERT_FF, D_MODEL)
    w_ple_b = w_ple.astype(BF16)
    w_pg_b = w_ple_gate.astype(BF16)
    w_r = jnp.concatenate([w_router_group, w_router_expert,
                           jnp.zeros((depth, D_MODEL, ROUTE_LANES - N_GROUPS - N_EXPERTS), F32)], axis=2)
    w_r_hi = w_r.astype(BF16)
    w_r_lo = (w_r - w_r_hi.astype(F32)).astype(BF16)
    qk_gain = jnp.concatenate([jnp.tile(q_norm, (1, ATTN_HEADS)) * (ATTN_HEAD_DIM ** -0.5),
                               jnp.tile(k_norm, (1, ATTN_KV_HEADS))], axis=1)
    seg_id = np.arange(QK_WIDTH) // ATTN_HEAD_DIM
    seg = jnp.asarray(seg_id[:, None] == seg_id[None, :], BF16)
    msum_np, pmask_np = _hgrn_constants(min(REC_CHUNK, S))
    msum = jnp.asarray(msum_np, BF16)
    pmask = jnp.asarray(pmask_np, F32)
    tm = min(TOKEN_TILE, T)
    tril = jnp.asarray(np.tril(np.ones((tm, tm), np.float32), -1), BF16)

    n_rows = n_assign + N_EXPERTS * EXPERT_BLOCK
    nblk = n_rows // EXPERT_BLOCK
    eids = jnp.arange(N_EXPERTS, dtype=I32)
    p2 = p.reshape(depth * T, PLE_DIM)

    h = x.reshape(T, D)
    for l in range(depth):
        za, zr, zf = _mix_in(h, mix_norm[l][None], w_in_p[l], rc, rs1, rs2, qk_gain[l][None], seg)
        attn = _attention(za, sinks[l], B, S)
        rec = _hgrn2(zr, zf, lower_bounds[l][None], rec_norm[l][None], msum, pmask, B, S)
        h1, xn, ri, rw, cnt = _mix_out(attn, rec, h, w_out_b[l], ffn_norm[l][None], w_r_hi[l], w_r_lo[l], tril)
        counts = cnt[0, :N_EXPERTS].astype(I32)
        padded = ((counts + EXPERT_BLOCK - 1) // EXPERT_BLOCK) * EXPERT_BLOCK
        pad_end = jnp.cumsum(padded)
        pad_start = pad_end - padded
        expert = ri[:, 0:2]
        dest = ri[:, 2:4] + jnp.sum(jnp.where(expert[:, :, None] == eids, pad_start, 0), axis=-1)
        dest = dest.reshape(n_assign).astype(I32)
        n_used = (pad_end[-1] // EXPERT_BLOCK).astype(I32).reshape(1)
        blk_start = jnp.arange(nblk, dtype=I32) * EXPERT_BLOCK
        block_expert = jnp.minimum(jnp.sum(pad_end[None, :] <= blk_start[:, None], axis=1), N_EXPERTS - 1).astype(I32)
        xs = _dispatch(dest, xn, n_rows)
        ys = _experts(block_expert, n_used, xs, wg_b, wu_b, wd_b, l)
        yg = _collect(dest, ys)
        h = _ple(h1, yg, rw, p2, w_ple_b[l], ple_norm[l][None], ple_gate_norm[l][None],
                 w_pg_b[l], l)
    return h.reshape(B, S, D)
```

```python
import functools

import jax
import jax.numpy as jnp
import numpy as np
from jax import lax
from jax.experimental import pallas as pl
from jax.experimental.pallas import tpu as pltpu

F32 = jnp.float32
BF16 = jnp.bfloat16
I32 = jnp.int32

D_MODEL = 1024
ATTN_HEADS = 8
ATTN_KV_HEADS = 2
ATTN_HEAD_DIM = 64
ATTN_GROUP = ATTN_HEADS // ATTN_KV_HEADS
ATTN_WIDTH = ATTN_HEADS * ATTN_HEAD_DIM
KV_WIDTH = ATTN_KV_HEADS * ATTN_HEAD_DIM
QK_WIDTH = ATTN_WIDTH + KV_WIDTH
QKV_WIDTH = ATTN_WIDTH + 2 * KV_WIDTH
WINDOW = 128
ROPE_THETA = 500000.0
ROPE_DIM = ATTN_HEAD_DIM // 4
ROPE_HALF = ROPE_DIM // 2
REC_HEADS = 4
REC_DIM = 128
REC_WIDTH = REC_HEADS * REC_DIM
REC_CHUNK = 128
IN_WIDTH = QKV_WIDTH + 4 * REC_WIDTH
N_GROUPS = 4
EXPERTS_PER_GROUP = 8
N_EXPERTS = N_GROUPS * EXPERTS_PER_GROUP
EXPERT_FF = 512
PLE_DIM = 256
RMS_EPS = 1e-6
MASK_VALUE = -1e30
LANES = 128
ROUTE_LANES = LANES
ROW_SUBLANES = D_MODEL // LANES

TOKEN_TILE = 512
SUB_TILE = 256
EXPERT_BLOCK = 256
EXPERT_SUB = 128
COPY_TOKENS = 512
COPY_UNROLL = 8


def _dot(a, b):
    return jnp.dot(a, b, preferred_element_type=F32)


def _dot_nt(a, b):
    return lax.dot_general(a, b, (((1,), (1,)), ((), ())), preferred_element_type=F32)


def _dot_tn(a, b):
    return lax.dot_general(a, b, (((0,), (0,)), ((), ())), preferred_element_type=F32)


def _store_row_tiles(ref, x, first=0):
    m = x.shape[0]
    for c in range(ROW_SUBLANES):
        ref[pl.ds(first * ROW_SUBLANES + c, m, stride=ROW_SUBLANES), :] = x[:, c * LANES:(c + 1) * LANES]


def _load_row_tiles(ref, m, first=0, every=1):
    return jnp.concatenate(
        [ref[pl.ds(first * ROW_SUBLANES + c, m, stride=every * ROW_SUBLANES), :] for c in range(ROW_SUBLANES)], axis=1)


def _sigmoid(x):
    return 1.0 / (1.0 + jnp.exp(-x))


def _rms(x, gain):
    ms = jnp.mean(x * x, axis=-1, keepdims=True)
    return x * lax.rsqrt(ms + RMS_EPS) * gain


def _mix_in_kernel(h_ref, gain_ref, w_ref, rc_ref, rs1_ref, rs2_ref, qkg_ref, seg_ref,
                   za_ref, zr_ref, zf_ref):
    xn = _rms(h_ref[...], gain_ref[...]).astype(BF16)
    z_a = _dot(xn, w_ref[:, 0:QKV_WIDTH])
    qk = z_a[:, 0:QK_WIDTH]
    seg = _dot((qk * qk).astype(BF16), seg_ref[...]) * (1.0 / ATTN_HEAD_DIM)
    qkn = qk * lax.rsqrt(seg + RMS_EPS) * qkg_ref[...]
    rc, rs1, rs2 = rc_ref[...], rs1_ref[...], rs2_ref[...]
    for c in range(QK_WIDTH // LANES):
        col = qkn[:, c * LANES:(c + 1) * LANES]
        rot = col * rc + pltpu.roll(col, LANES - ROPE_HALF, 1) * rs1 + pltpu.roll(col, ROPE_HALF, 1) * rs2
        za_ref[:, c * LANES:(c + 1) * LANES] = rot.astype(BF16)
    za_ref[:, QK_WIDTH:QKV_WIDTH] = z_a[:, QK_WIDTH:QKV_WIDTH].astype(BF16)
    zr_ref[...] = _dot(xn, w_ref[:, QKV_WIDTH:QKV_WIDTH + 3 * REC_WIDTH]).astype(BF16)
    zf_ref[...] = _dot(xn, w_ref[:, QKV_WIDTH + 3 * REC_WIDTH:IN_WIDTH])


def _mix_in(h, gain, w, rc, rs1, rs2, qkg, seg):
    T = h.shape[0]
    tm = min(TOKEN_TILE, T)
    row = lambda i: (i, 0)
    fixed = lambda i: (0, 0)
    return pl.pallas_call(
        _mix_in_kernel,
        grid=(T // tm,),
        in_specs=[
            pl.BlockSpec((tm, D_MODEL), row),
            pl.BlockSpec((1, D_MODEL), fixed),
            pl.BlockSpec((D_MODEL, IN_WIDTH), fixed),
            pl.BlockSpec((tm, LANES), row),
            pl.BlockSpec((tm, LANES), row),
            pl.BlockSpec((tm, LANES), row),
            pl.BlockSpec((1, QK_WIDTH), fixed),
            pl.BlockSpec((QK_WIDTH, QK_WIDTH), fixed),
        ],
        out_specs=[
            pl.BlockSpec((tm, QKV_WIDTH), row),
            pl.BlockSpec((tm, 3 * REC_WIDTH), row),
            pl.BlockSpec((tm, REC_WIDTH), row),
        ],
        out_shape=[
            jax.ShapeDtypeStruct((T, QKV_WIDTH), BF16),
            jax.ShapeDtypeStruct((T, 3 * REC_WIDTH), BF16),
            jax.ShapeDtypeStruct((T, REC_WIDTH), F32),
        ],
        compiler_params=pltpu.CompilerParams(dimension_semantics=("parallel",)),
        name="mix_in",
    )(h, gain, w, rc, rs1, rs2, qkg, seg)


def _attn_kernel(sink_ref, q_ref, kvc_ref, kvp_ref, o_ref):
    n = pl.program_id(1)
    q = q_ref[...]
    kvc = kvc_ref[...]
    kvp = kvp_ref[...]
    rows = lax.broadcasted_iota(I32, (ATTN_GROUP * WINDOW, 2 * WINDOW), 0)
    cols = lax.broadcasted_iota(I32, (ATTN_GROUP * WINDOW, 2 * WINDOW), 1)
    qi = rows & (WINDOW - 1)
    valid = (cols > qi) & (cols <= qi + WINDOW) & ((cols >= WINDOW) | (n > 0))
    grp = lax.broadcasted_iota(I32, (ATTN_GROUP * WINDOW, 1), 0) // WINDOW
    for j in range(ATTN_KV_HEADS):
        ks = slice(j * ATTN_HEAD_DIM, (j + 1) * ATTN_HEAD_DIM)
        vs = slice(KV_WIDTH + j * ATTN_HEAD_DIM, KV_WIDTH + (j + 1) * ATTN_HEAD_DIM)
        k = jnp.concatenate([kvp[:, ks], kvc[:, ks]], axis=0)
        v = jnp.concatenate([kvp[:, vs], kvc[:, vs]], axis=0)
        heads = [j * ATTN_GROUP + g for g in range(ATTN_GROUP)]
        q4 = jnp.concatenate([q[:, hh * ATTN_HEAD_DIM:(hh + 1) * ATTN_HEAD_DIM] for hh in heads], axis=0)
        s = jnp.where(valid, _dot_nt(q4, k), MASK_VALUE)
        sink = jnp.zeros((ATTN_GROUP * WINDOW, 1), F32)
        for g, hh in enumerate(heads):
            sink = jnp.where(grp == g, sink_ref[hh], sink)
        m = jnp.maximum(jnp.max(s, axis=-1, keepdims=True), sink)
        e = jnp.exp(s - m)
        denom = jnp.sum(e, axis=-1, keepdims=True) + jnp.exp(sink - m)
        p = (e * (1.0 / denom)).astype(BF16)
        o = _dot(p, v)
        for g, hh in enumerate(heads):
            o_ref[:, hh * ATTN_HEAD_DIM:(hh + 1) * ATTN_HEAD_DIM] = o[g * WINDOW:(g + 1) * WINDOW].astype(BF16)


def _attention(za, sinks, B, S):
    nb = S // WINDOW
    kvblk = ATTN_WIDTH // (2 * KV_WIDTH)
    return pl.pallas_call(
        _attn_kernel,
        grid=(B, nb),
        in_specs=[
            pl.BlockSpec(memory_space=pltpu.SMEM),
            pl.BlockSpec((WINDOW, ATTN_WIDTH), lambda b, n: (b * nb + n, 0)),
            pl.BlockSpec((WINDOW, 2 * KV_WIDTH), lambda b, n: (b * nb + n, kvblk)),
            pl.BlockSpec((WINDOW, 2 * KV_WIDTH), lambda b, n: (b * nb + jnp.maximum(n - 1, 0), kvblk)),
        ],
        out_specs=pl.BlockSpec((WINDOW, ATTN_WIDTH), lambda b, n: (b * nb + n, 0)),
        out_shape=jax.ShapeDtypeStruct((B * S, ATTN_WIDTH), BF16),
        compiler_params=pltpu.CompilerParams(dimension_semantics=("parallel", "arbitrary")),
        name="swa_attention",
    )(sinks, za, za, za)


def _hgrn_levels(C):
    out, s = [], C // 2
    while s >= 1:
        out.append(s)
        s //= 2
    return out


def _hgrn_constants(C):
    r = np.arange(C)[:, None]
    u = np.arange(C)[None, :]
    mats = [u <= r, u > r]
    masks = []
    for s in _hgrn_levels(C):
        mid = (r // (2 * s)) * 2 * s + s
        upper = r >= mid
        mats.append(np.where(upper, (u >= mid) & (u <= r), (u > r) & (u < mid)))
        masks.append((r // (2 * s)) == (u // (2 * s)))
    masks.append(r == u)
    return (np.concatenate(mats, axis=0).astype(np.float32), np.stack(masks).astype(np.float32))


def _hgrn_kernel(zr_ref, zf_ref, lb_ref, gain_ref, msum_ref, pmask_ref, o_ref, st_ref):
    C = zf_ref.shape[0]
    levels = _hgrn_levels(C)

    @pl.when(pl.program_id(1) == 0)
    def _():
        st_ref[...] = jnp.zeros_like(st_ref)

    row = lax.broadcasted_iota(I32, (C, REC_DIM), 0)
    for hd in range(REC_HEADS):
        cs = slice(hd * REC_DIM, (hd + 1) * REC_DIM)
        zq = zr_ref[:, cs].astype(F32)
        v = zr_ref[:, REC_WIDTH + hd * REC_DIM:REC_WIDTH + (hd + 1) * REC_DIM]
        og = zr_ref[:, 2 * REC_WIDTH + hd * REC_DIM:2 * REC_WIDTH + (hd + 1) * REC_DIM].astype(F32)
        z = zf_ref[:, cs]
        lb = lb_ref[:, cs]
        qp = zq * _sigmoid(zq)
        a = jnp.exp(-jnp.abs(z))
        r = 1.0 / (1.0 + a)
        pos = z >= 0
        g = jnp.log(lb + (1.0 - lb) * jnp.where(pos, r, a * r))
        k = (1.0 - lb) * jnp.where(pos, a * r, r)
        g_hi = g.astype(BF16)
        g_lo = (g - g_hi.astype(F32)).astype(BF16)
        x2 = _dot(msum_ref[...], jnp.concatenate([g_hi, g_lo], axis=1))
        e_all = jnp.exp(x2[:, :REC_DIM] + x2[:, REC_DIM:])
        e_b = e_all[0:C]
        e_e = e_all[C:2 * C]
        st = st_ref[hd]
        o = _dot_nt((qp * e_b).astype(BF16), st.astype(BF16))
        kb = k.astype(BF16)
        amat = pmask_ref[len(levels)] * _dot_nt(qp.astype(BF16), kb)
        for li, s in enumerate(levels):
            e_l = e_all[(2 + li) * C:(3 + li) * C]
            upper = (row & s) != 0
            ql = jnp.where(upper, qp * e_l, 0.0).astype(BF16)
            kl = jnp.where(upper, 0.0, k * e_l).astype(BF16)
            amat = amat + pmask_ref[li] * _dot_nt(ql, kl)
        o = o + _dot(amat.astype(BF16), v)
        st_ref[hd] = st * e_b[C - 1:C, :] + _dot_tn(v, (k * e_e).astype(BF16))
        og_act = og * _sigmoid(og)
        o_ref[:, cs] = (_rms(o, gain_ref[...]) * og_act).astype(BF16)


def _hgrn2(zr, zf, lb, gain, msum, pmask, B, S):
    C = min(REC_CHUNK, S)
    nc = S // C
    row = lambda b, c: (b * nc + c, 0)
    return pl.pallas_call(
        _hgrn_kernel,
        grid=(B, nc),
        in_specs=[
            pl.BlockSpec((C, 3 * REC_WIDTH), row),
            pl.BlockSpec((C, REC_WIDTH), row),
            pl.BlockSpec((1, REC_WIDTH), lambda b, c: (0, 0)),
            pl.BlockSpec((1, REC_DIM), lambda b, c: (0, 0)),
            pl.BlockSpec(msum.shape, lambda b, c: (0, 0)),
            pl.BlockSpec(pmask.shape, lambda b, c: (0, 0, 0)),
        ],
        out_specs=pl.BlockSpec((C, REC_WIDTH), row),
        out_shape=jax.ShapeDtypeStruct((B * S, REC_WIDTH), BF16),
        scratch_shapes=[pltpu.VMEM((REC_HEADS, REC_DIM, REC_DIM), F32)],
        compiler_params=pltpu.CompilerParams(dimension_semantics=("parallel", "arbitrary")),
        name="hgrn2",
    )(zr, zf, lb, gain, msum, pmask)


def _mix_out_kernel(attn_ref, rec_ref, h_ref, wo_ref, gain_ref, wrh_ref, wrl_ref, tril_ref,
                    h1_ref, xn_ref, ri_ref, rw_ref, cnt_ref, carry_ref):
    @pl.when(pl.program_id(0) == 0)
    def _():
        carry_ref[...] = jnp.zeros_like(carry_ref)

    sub = tril_ref.shape[0]
    carry = carry_ref[...]
    for s in range(h_ref.shape[0] // sub):
        carry = _mix_out_rows(slice(s * sub, (s + 1) * sub), s * sub, carry, attn_ref, rec_ref, h_ref, wo_ref,
                              gain_ref, wrh_ref, wrl_ref, tril_ref, h1_ref, xn_ref, ri_ref, rw_ref)
    carry_ref[...] = carry
    cnt_ref[...] = jnp.broadcast_to(carry, cnt_ref.shape)


def _mix_out_rows(rows, first, carry, attn_ref, rec_ref, h_ref, wo_ref, gain_ref, wrh_ref, wrl_ref, tril_ref,
                  h1_ref, xn_ref, ri_ref, rw_ref):
    mixed = jnp.concatenate([attn_ref[rows, :], rec_ref[rows, :]], axis=1)
    h1 = h_ref[rows, :] + _dot(mixed, wo_ref[...])
    h1_ref[rows, :] = h1
    xn = _rms(h1, gain_ref[...])
    _store_row_tiles(xn_ref, xn, first=first)
    xh = xn.astype(BF16)
    xl = (xn - xh.astype(F32)).astype(BF16)
    logits = _dot(xh, wrh_ref[...]) + _dot(xl, wrh_ref[...]) + _dot(xh, wrl_ref[...])
    lane = lax.broadcasted_iota(I32, logits.shape, 1)
    lanef = lane.astype(F32)
    neg = jnp.float32(-jnp.inf)
    big = jnp.float32(1e9)
    gl = jnp.where(lane < N_GROUPS, logits, neg)
    gmax = jnp.max(gl, axis=-1, keepdims=True)
    gidx = jnp.min(jnp.where(gl == gmax, lanef, big), axis=-1, keepdims=True)
    p_group = 1.0 / jnp.sum(jnp.where(lane < N_GROUPS, jnp.exp(logits - gmax), 0.0), axis=-1, keepdims=True)
    lo = N_GROUPS + gidx * EXPERTS_PER_GROUP
    el = jnp.where((lanef >= lo) & (lanef < lo + EXPERTS_PER_GROUP), logits, neg)
    t1 = jnp.max(el, axis=-1, keepdims=True)
    i1 = jnp.min(jnp.where(el == t1, lanef, big), axis=-1, keepdims=True)
    el2 = jnp.where(lanef == i1, neg, el)
    t2 = jnp.max(el2, axis=-1, keepdims=True)
    i2 = jnp.min(jnp.where(el2 == t2, lanef, big), axis=-1, keepdims=True)
    r21 = jnp.exp(t2 - t1)
    w1 = p_group / (1.0 + r21)
    w2 = w1 * r21
    e1 = i1 - N_GROUPS
    e2 = i2 - N_GROUPS
    oh1 = (lanef == e1).astype(F32)
    oh2 = (lanef == e2).astype(F32)
    oh = oh1 + oh2
    prefix = _dot(tril_ref[...], oh.astype(BF16)) + carry
    rank1 = jnp.sum(prefix * oh1, axis=-1, keepdims=True)
    rank2 = jnp.sum(prefix * oh2, axis=-1, keepdims=True)
    ri = jnp.where(lane == 0, e1, jnp.where(lane == 1, e2, jnp.where(lane == 2, rank1, jnp.where(lane == 3, rank2, 0.0))))
    ri_ref[rows, :] = ri.astype(I32)
    rw_ref[rows, :] = jnp.where(lane == 0, w1, jnp.where(lane == 1, w2, 0.0))
    return carry + jnp.sum(oh, axis=0, keepdims=True)


def _mix_out(attn, rec, h, wo, gain, wrh, wrl, tril):
    T = h.shape[0]
    tm = min(TOKEN_TILE, T)
    row = lambda i: (i, 0)
    fixed = lambda i: (0, 0)
    return pl.pallas_call(
        _mix_out_kernel,
        grid=(T // tm,),
        in_specs=[
            pl.BlockSpec((tm, ATTN_WIDTH), row),
            pl.BlockSpec((tm, REC_WIDTH), row),
            pl.BlockSpec((tm, D_MODEL), row),
            pl.BlockSpec((ATTN_WIDTH + REC_WIDTH, D_MODEL), fixed),
            pl.BlockSpec((1, D_MODEL), fixed),
            pl.BlockSpec((D_MODEL, ROUTE_LANES), fixed),
            pl.BlockSpec((D_MODEL, ROUTE_LANES), fixed),
            pl.BlockSpec(tril.shape, fixed),
        ],
        out_specs=[
            pl.BlockSpec((tm, D_MODEL), row),
            pl.BlockSpec((tm * ROW_SUBLANES, LANES), row),
            pl.BlockSpec((tm, ROUTE_LANES), row),
            pl.BlockSpec((tm, ROUTE_LANES), row),
            pl.BlockSpec((8, ROUTE_LANES), fixed),
        ],
        out_shape=[
            jax.ShapeDtypeStruct((T, D_MODEL), F32),
            jax.ShapeDtypeStruct((T * ROW_SUBLANES, LANES), F32),
            jax.ShapeDtypeStruct((T, ROUTE_LANES), I32),
            jax.ShapeDtypeStruct((T, ROUTE_LANES), F32),
            jax.ShapeDtypeStruct((8, ROUTE_LANES), F32),
        ],
        scratch_shapes=[pltpu.VMEM((1, ROUTE_LANES), F32)],
        compiler_params=pltpu.CompilerParams(dimension_semantics=("arbitrary",)),
        name="mix_out_router",
    )(attn, rec, h, wo, gain, wrh, wrl, tril)


def _row_move_loops(copy, tokens):
    def start(t, carry):
        copy(t, 0).start()
        copy(t, 1).start()
        return carry

    def wait(t, carry):
        copy(t, 0).wait()
        copy(t, 1).wait()
        return carry

    lax.fori_loop(0, tokens, start, 0, unroll=COPY_UNROLL)
    lax.fori_loop(0, tokens, wait, 0, unroll=COPY_UNROLL)


def _dispatch_kernel(dest_ref, bound_ref, x_ref, out_ref, zero_ref, sem, *, tokens):
    @pl.when(pl.program_id(0) == 0)
    def _():
        zero_ref[...] = jnp.zeros_like(zero_ref)

        def fill(e):
            return pltpu.make_async_copy(zero_ref, out_ref.at[pl.ds(bound_ref[e + 1] - EXPERT_BLOCK, EXPERT_BLOCK)], sem)

        for e in range(N_EXPERTS):
            pl.when(bound_ref[e + 1] > bound_ref[e])(lambda e=e: fill(e).start())
        for e in range(N_EXPERTS):
            pl.when(bound_ref[e + 1] > bound_ref[e])(lambda e=e: fill(e).wait())

        def tail(i):
            return pltpu.make_async_copy(zero_ref, out_ref.at[pl.ds(i * EXPERT_BLOCK, EXPERT_BLOCK)], sem)

        first_unused = bound_ref[N_EXPERTS] // EXPERT_BLOCK
        n_blocks = out_ref.shape[0] // EXPERT_BLOCK
        lax.fori_loop(first_unused, n_blocks, lambda i, c: (tail(i).start(), c)[1], 0)
        lax.fori_loop(first_unused, n_blocks, lambda i, c: (tail(i).wait(), c)[1], 0)

    base = 2 * pl.program_id(0) * tokens

    def copy(t, j):
        return pltpu.make_async_copy(x_ref.at[t], out_ref.at[dest_ref[base + 2 * t + j]], sem)

    _row_move_loops(copy, tokens)


def _collect_kernel(dest_ref, y_ref, out_ref, sem, *, tokens):
    base = 2 * pl.program_id(0) * tokens

    def copy(t, j):
        return pltpu.make_async_copy(y_ref.at[dest_ref[base + 2 * t + j]], out_ref.at[2 * t + j], sem)

    _row_move_loops(copy, tokens)


def _dispatch(dest, bounds, x3, n_rows):
    tokens = min(COPY_TOKENS, x3.shape[0])
    tile = x3.shape[1:]
    return pl.pallas_call(
        functools.partial(_dispatch_kernel, tokens=tokens),
        grid_spec=pltpu.PrefetchScalarGridSpec(
            num_scalar_prefetch=2,
            grid=(x3.shape[0] // tokens,),
            in_specs=[pl.BlockSpec((tokens,) + tile, lambda i, d, b: (i, 0, 0))],
            out_specs=pl.BlockSpec(memory_space=pl.ANY),
            scratch_shapes=[pltpu.VMEM((EXPERT_BLOCK,) + tile, x3.dtype), pltpu.SemaphoreType.DMA],
        ),
        out_shape=jax.ShapeDtypeStruct((n_rows,) + tile, x3.dtype),
        compiler_params=pltpu.CompilerParams(dimension_semantics=("arbitrary",)),
        name="moe_dispatch_rows",
    )(dest, bounds, x3)


def _collect(dest, y3):
    n_assign = dest.shape[0]
    tokens = min(COPY_TOKENS, n_assign // 2)
    tile = y3.shape[1:]
    return pl.pallas_call(
        functools.partial(_collect_kernel, tokens=tokens),
        grid_spec=pltpu.PrefetchScalarGridSpec(
            num_scalar_prefetch=1,
            grid=(n_assign // (2 * tokens),),
            in_specs=[pl.BlockSpec(memory_space=pl.ANY)],
            out_specs=pl.BlockSpec((2 * tokens,) + tile, lambda i, d: (i, 0, 0)),
            scratch_shapes=[pltpu.SemaphoreType.DMA],
        ),
        out_shape=jax.ShapeDtypeStruct((n_assign,) + tile, y3.dtype),
        compiler_params=pltpu.CompilerParams(dimension_semantics=("arbitrary",)),
        name="moe_collect_rows",
    )(dest, y3)


def _expert_kernel(be_ref, nused_ref, x_ref, wg_ref, wu_ref, wd_ref, y_ref, wgb_ref, wub_ref, wdb_ref):
    i = pl.program_id(0)
    used = i < nused_ref[0]

    @pl.when(used & ((i == 0) | (be_ref[i] != be_ref[jnp.maximum(i - 1, 0)])))
    def _():
        wgb_ref[...] = wg_ref[0].astype(BF16)
        wub_ref[...] = wu_ref[0].astype(BF16)
        wdb_ref[...] = wd_ref[0].astype(BF16)

    @pl.when(used)
    def _():
        for s in range(EXPERT_BLOCK // EXPERT_SUB):
            x = _load_row_tiles(x_ref, EXPERT_SUB, first=s * EXPERT_SUB).astype(BF16)
            gate = _dot(x, wgb_ref[...])
            up = _dot(x, wub_ref[...])
            hidden = (gate * _sigmoid(gate) * up).astype(BF16)
            _store_row_tiles(y_ref, _dot(hidden, wdb_ref[...]), first=s * EXPERT_SUB)

    @pl.when(jnp.logical_not(used))
    def _():
        y_ref[...] = jnp.zeros_like(y_ref)


def _experts(block_expert, n_used, xs, wg, wu, wd, layer):
    blk = EXPERT_BLOCK * ROW_SUBLANES
    nblk = xs.shape[0] // blk

    def xmap(i, be, nu):
        return (jnp.minimum(i, nu[0] - 1), 0)

    def wmap(i, be, nu):
        return (layer * N_EXPERTS + be[jnp.minimum(i, nu[0] - 1)], 0, 0)

    return pl.pallas_call(
        _expert_kernel,
        grid_spec=pltpu.PrefetchScalarGridSpec(
            num_scalar_prefetch=2,
            grid=(nblk,),
            in_specs=[
                pl.BlockSpec((blk, LANES), xmap),
                pl.BlockSpec((1, D_MODEL, EXPERT_FF), wmap),
                pl.BlockSpec((1, D_MODEL, EXPERT_FF), wmap),
                pl.BlockSpec((1, EXPERT_FF, D_MODEL), wmap),
            ],
            out_specs=pl.BlockSpec((blk, LANES), lambda i, be, nu: (i, 0)),
            scratch_shapes=[
                pltpu.VMEM((D_MODEL, EXPERT_FF), BF16),
                pltpu.VMEM((D_MODEL, EXPERT_FF), BF16),
                pltpu.VMEM((EXPERT_FF, D_MODEL), BF16),
            ],
        ),
        out_shape=jax.ShapeDtypeStruct(xs.shape, F32),
        compiler_params=pltpu.CompilerParams(dimension_semantics=("arbitrary",)),
        name="moe_experts",
    )(block_expert, n_used, xs, wg, wu, wd)


def _ple_kernel(h1_ref, yg_ref, rw_ref, p_ref, wple_ref, pgain_ref, ggain_ref, wpg_ref, o_ref):
    tm = h1_ref.shape[0]
    sub = min(SUB_TILE, tm)
    for s in range(tm // sub):
        rows = slice(s * sub, (s + 1) * sub)
        rw = rw_ref[rows, :]
        y1 = _load_row_tiles(yg_ref, sub, first=2 * s * sub, every=2)
        y2 = _load_row_tiles(yg_ref, sub, first=2 * s * sub + 1, every=2)
        h2 = h1_ref[rows, :] + rw[:, 0:1] * y1 + rw[:, 1:2] * y2
        ple = _rms(_dot(p_ref[rows, :].astype(BF16), wple_ref[...]), pgain_ref[...])
        gate = _sigmoid(_dot(_rms(h2, ggain_ref[...]).astype(BF16), wpg_ref[...]))
        o_ref[rows, :] = h2 + ple * gate


def _ple(h1, yg, rw, p, wple, pgain, ggain, wpg, layer):
    T = h1.shape[0]
    tm = min(TOKEN_TILE, T)
    nt = T // tm
    row = lambda i: (i, 0)
    fixed = lambda i: (0, 0)
    return pl.pallas_call(
        _ple_kernel,
        grid=(nt,),
        in_specs=[
            pl.BlockSpec((tm, D_MODEL), row),
            pl.BlockSpec((2 * tm * ROW_SUBLANES, LANES), row),
            pl.BlockSpec((tm, ROUTE_LANES), row),
            pl.BlockSpec((tm, PLE_DIM), lambda i: (layer * nt + i, 0)),
            pl.BlockSpec((PLE_DIM, D_MODEL), fixed),
            pl.BlockSpec((1, D_MODEL), fixed),
            pl.BlockSpec((1, D_MODEL), fixed),
            pl.BlockSpec((D_MODEL, D_MODEL), fixed),
        ],
        out_specs=pl.BlockSpec((tm, D_MODEL), row),
        out_shape=jax.ShapeDtypeStruct((T, D_MODEL), F32),
        compiler_params=pltpu.CompilerParams(dimension_semantics=("parallel",)),
        name="combine_ple",
    )(h1, yg, rw, p, wple, pgain, ggain, wpg)


def _rope_tables(positions):
    inv_freq = ROPE_THETA ** (-jnp.arange(0, ROPE_DIM, 2, dtype=F32) / ROPE_DIM)
    ang = positions.astype(F32).reshape(-1, 1) * inv_freq
    cos, sin = jnp.cos(ang), jnp.sin(ang)
    T = ang.shape[0]
    rest = jnp.zeros((T, ATTN_HEAD_DIM - ROPE_DIM), F32)
    zero = jnp.zeros((T, ROPE_HALF), F32)
    rc = jnp.concatenate([cos, cos, rest + 1.0], axis=1)
    rs1 = jnp.concatenate([-sin, zero, rest], axis=1)
    rs2 = jnp.concatenate([zero, sin, rest], axis=1)
    reps = LANES // ATTN_HEAD_DIM
    return tuple(jnp.tile(t, (1, reps)) for t in (rc, rs1, rs2))


def kernel(x, p, positions, mix_norm, w_in, q_norm, k_norm, sinks, lb_logits, rec_norm, w_out, ffn_norm,
           w_router_group, w_router_expert, w_gate, w_up, w_down, w_ple, ple_norm, ple_gate_norm, w_ple_gate):
    B, S, D = x.shape
    depth = w_in.shape[0]
    T = B * S
    n_assign = 2 * T
    assert D == D_MODEL and S % WINDOW == 0 and T % min(TOKEN_TILE, T) == 0

    rc, rs1, rs2 = _rope_tables(positions)
    lb_sm = jax.nn.softmax(lb_logits.astype(F32), axis=0)
    lower_bounds = jnp.cumsum(lb_sm, axis=0) - lb_sm[0:1]

    a0 = QKV_WIDTH
    w_in_p = jnp.concatenate([w_in[:, :, :a0 + REC_WIDTH], w_in[:, :, a0 + 2 * REC_WIDTH:],
                              w_in[:, :, a0 + REC_WIDTH:a0 + 2 * REC_WIDTH]], axis=2).astype(BF16)
    w_out_b = w_out.astype(BF16)
    wg_all = w_gate.reshape(depth * N_EXPERTS, D_MODEL, EXPERT_FF)
    wu_all = w_up.reshape(depth * N_EXPERTS, D_MODEL, EXPERT_FF)
    wd_all = w_down.reshape(depth * N_EXPERTS, EXPERT_FF, D_MODEL)
    w_ple_b = w_ple.astype(BF16)
    w_pg_b = w_ple_gate.astype(BF16)
    w_r = jnp.concatenate([w_router_group, w_router_expert,
                           jnp.zeros((depth, D_MODEL, ROUTE_LANES - N_GROUPS - N_EXPERTS), F32)], axis=2)
    w_r_hi = w_r.astype(BF16)
    w_r_lo = (w_r - w_r_hi.astype(F32)).astype(BF16)
    qk_gain = jnp.concatenate([jnp.tile(q_norm, (1, ATTN_HEADS)) * (ATTN_HEAD_DIM ** -0.5),
                               jnp.tile(k_norm, (1, ATTN_KV_HEADS))], axis=1)
    seg_id = np.arange(QK_WIDTH) // ATTN_HEAD_DIM
    seg = jnp.asarray(seg_id[:, None] == seg_id[None, :], BF16)
    msum_np, pmask_np = _hgrn_constants(min(REC_CHUNK, S))
    msum = jnp.asarray(msum_np, BF16)
    pmask = jnp.asarray(pmask_np, F32)
    sub = min(SUB_TILE, T)
    tril = jnp.asarray(np.tril(np.ones((sub, sub), np.float32), -1), BF16)

    n_rows = n_assign + N_EXPERTS * EXPERT_BLOCK
    nblk = n_rows // EXPERT_BLOCK
    eids = jnp.arange(N_EXPERTS, dtype=I32)
    p2 = p.reshape(depth * T, PLE_DIM)

    h = x.reshape(T, D)
    for l in range(depth):
        za, zr, zf = _mix_in(h, mix_norm[l][None], w_in_p[l], rc, rs1, rs2, qk_gain[l][None], seg)
        attn = _attention(za, sinks[l], B, S)
        rec = _hgrn2(zr, zf, lower_bounds[l][None], rec_norm[l][None], msum, pmask, B, S)
        h1, xn, ri, rw, cnt = _mix_out(attn, rec, h, w_out_b[l], ffn_norm[l][None], w_r_hi[l], w_r_lo[l], tril)
        counts = cnt[0, :N_EXPERTS].astype(I32)
        padded = ((counts + EXPERT_BLOCK - 1) // EXPERT_BLOCK) * EXPERT_BLOCK
        pad_end = jnp.cumsum(padded)
        pad_start = pad_end - padded
        expert = ri[:, 0:2]
        dest = ri[:, 2:4] + jnp.sum(jnp.where(expert[:, :, None] == eids, pad_start, 0), axis=-1)
        dest = dest.reshape(n_assign).astype(I32)
        n_used = (pad_end[-1] // EXPERT_BLOCK).astype(I32).reshape(1)
        blk_start = jnp.arange(nblk, dtype=I32) * EXPERT_BLOCK
        block_expert = jnp.minimum(jnp.sum(pad_end[None, :] <= blk_start[:, None], axis=1), N_EXPERTS - 1).astype(I32)
        bounds = jnp.concatenate([jnp.zeros((1,), I32), pad_end.astype(I32)])
        xs = _dispatch(dest, bounds, xn.reshape(T, ROW_SUBLANES, LANES), n_rows)
        ys = _experts(block_expert, n_used, xs.reshape(n_rows * ROW_SUBLANES, LANES), wg_all, wu_all, wd_all, l)
        yg = _collect(dest, ys.reshape(n_rows, ROW_SUBLANES, LANES))
        h = _ple(h1, yg.reshape(n_assign * ROW_SUBLANES, LANES), rw, p2, w_ple_b[l], ple_norm[l][None],
                 ple_gate_norm[l][None], w_pg_b[l], l)
    return h.reshape(B, S, D)
```

```python
import functools

import jax
import jax.numpy as jnp
import numpy as np
from jax import lax
from jax.experimental import pallas as pl
from jax.experimental.pallas import tpu as pltpu

F32 = jnp.float32
BF16 = jnp.bfloat16
I32 = jnp.int32
U32 = jnp.uint32

D_MODEL = 1024
ATTN_HEADS = 8
ATTN_KV_HEADS = 2
ATTN_HEAD_DIM = 64
ATTN_GROUP = ATTN_HEADS // ATTN_KV_HEADS
ATTN_WIDTH = ATTN_HEADS * ATTN_HEAD_DIM
KV_WIDTH = ATTN_KV_HEADS * ATTN_HEAD_DIM
QK_WIDTH = ATTN_WIDTH + KV_WIDTH
QKV_WIDTH = ATTN_WIDTH + 2 * KV_WIDTH
WINDOW = 128
ROPE_THETA = 500000.0
ROPE_DIM = ATTN_HEAD_DIM // 4
ROPE_HALF = ROPE_DIM // 2
REC_HEADS = 4
REC_DIM = 128
REC_WIDTH = REC_HEADS * REC_DIM
REC_CHUNK = 128
IN_WIDTH = QKV_WIDTH + 4 * REC_WIDTH
N_GROUPS = 4
EXPERTS_PER_GROUP = 8
N_EXPERTS = N_GROUPS * EXPERTS_PER_GROUP
EXPERT_FF = 512
PLE_DIM = 256
RMS_EPS = 1e-6
MASK_VALUE = -1e30
LANES = 128
ROUTE_LANES = LANES
ROW_WORDS = D_MODEL // 2
ROW_SUBLANES = ROW_WORDS // LANES
HIGH_HALF = np.uint32(0xFFFF0000)

TOKEN_TILE = 512
SUB_TILE = 256
EXPERT_BLOCK = 256
EXPERT_SUB = 128
COPY_TOKENS = 512
COPY_UNROLL = 8


def _dot(a, b):
    return jnp.dot(a, b, preferred_element_type=F32)


def _dot_nt(a, b):
    return lax.dot_general(a, b, (((1,), (1,)), ((), ())), preferred_element_type=F32)


def _dot_tn(a, b):
    return lax.dot_general(a, b, (((0,), (0,)), ((), ())), preferred_element_type=F32)


def _store_row_tiles(ref, x, first=0):
    m = x.shape[0]
    lo = lax.bitcast_convert_type(x[:, :ROW_WORDS].astype(BF16).astype(F32), U32) >> 16
    hi = lax.bitcast_convert_type(x[:, ROW_WORDS:].astype(BF16).astype(F32), U32) & HIGH_HALF
    words = lo | hi
    for c in range(ROW_SUBLANES):
        ref[pl.ds(first * ROW_SUBLANES + c, m, stride=ROW_SUBLANES), :] = words[:, c * LANES:(c + 1) * LANES]


def _load_row_tiles(ref, m, first=0, every=1):
    words = jnp.concatenate(
        [ref[pl.ds(first * ROW_SUBLANES + c, m, stride=every * ROW_SUBLANES), :] for c in range(ROW_SUBLANES)], axis=1)
    lo = lax.bitcast_convert_type(words << 16, F32)
    hi = lax.bitcast_convert_type(words & HIGH_HALF, F32)
    return jnp.concatenate([lo, hi], axis=1)


def _sigmoid(x):
    return 1.0 / (1.0 + jnp.exp(-x))


def _rms(x, gain):
    ms = jnp.mean(x * x, axis=-1, keepdims=True)
    return x * lax.rsqrt(ms + RMS_EPS) * gain


def _mix_in_kernel(h_ref, gain_ref, w_ref, rc_ref, rs1_ref, rs2_ref, qkg_ref, seg_ref,
                   za_ref, zr_ref, zf_ref):
    xn = _rms(h_ref[...], gain_ref[...]).astype(BF16)
    z_a = _dot(xn, w_ref[:, 0:QKV_WIDTH])
    qk = z_a[:, 0:QK_WIDTH]
    seg = _dot((qk * qk).astype(BF16), seg_ref[...]) * (1.0 / ATTN_HEAD_DIM)
    qkn = qk * lax.rsqrt(seg + RMS_EPS) * qkg_ref[...]
    rc, rs1, rs2 = rc_ref[...], rs1_ref[...], rs2_ref[...]
    for c in range(QK_WIDTH // LANES):
        col = qkn[:, c * LANES:(c + 1) * LANES]
        rot = col * rc + pltpu.roll(col, LANES - ROPE_HALF, 1) * rs1 + pltpu.roll(col, ROPE_HALF, 1) * rs2
        za_ref[:, c * LANES:(c + 1) * LANES] = rot.astype(BF16)
    za_ref[:, QK_WIDTH:QKV_WIDTH] = z_a[:, QK_WIDTH:QKV_WIDTH].astype(BF16)
    zr_ref[...] = _dot(xn, w_ref[:, QKV_WIDTH:QKV_WIDTH + 3 * REC_WIDTH]).astype(BF16)
    zf_ref[...] = _dot(xn, w_ref[:, QKV_WIDTH + 3 * REC_WIDTH:IN_WIDTH])


def _mix_in(h, gain, w, rc, rs1, rs2, qkg, seg):
    T = h.shape[0]
    tm = min(TOKEN_TILE, T)
    row = lambda i: (i, 0)
    fixed = lambda i: (0, 0)
    return pl.pallas_call(
        _mix_in_kernel,
        grid=(T // tm,),
        in_specs=[
            pl.BlockSpec((tm, D_MODEL), row),
            pl.BlockSpec((1, D_MODEL), fixed),
            pl.BlockSpec((D_MODEL, IN_WIDTH), fixed),
            pl.BlockSpec((tm, LANES), row),
            pl.BlockSpec((tm, LANES), row),
            pl.BlockSpec((tm, LANES), row),
            pl.BlockSpec((1, QK_WIDTH), fixed),
            pl.BlockSpec((QK_WIDTH, QK_WIDTH), fixed),
        ],
        out_specs=[
            pl.BlockSpec((tm, QKV_WIDTH), row),
            pl.BlockSpec((tm, 3 * REC_WIDTH), row),
            pl.BlockSpec((tm, REC_WIDTH), row),
        ],
        out_shape=[
            jax.ShapeDtypeStruct((T, QKV_WIDTH), BF16),
            jax.ShapeDtypeStruct((T, 3 * REC_WIDTH), BF16),
            jax.ShapeDtypeStruct((T, REC_WIDTH), F32),
        ],
        compiler_params=pltpu.CompilerParams(dimension_semantics=("parallel",)),
        name="mix_in",
    )(h, gain, w, rc, rs1, rs2, qkg, seg)


def _attn_kernel(sink_ref, q_ref, kvc_ref, kvp_ref, o_ref):
    n = pl.program_id(1)
    q = q_ref[...]
    kvc = kvc_ref[...]
    kvp = kvp_ref[...]
    rows = lax.broadcasted_iota(I32, (ATTN_GROUP * WINDOW, 2 * WINDOW), 0)
    cols = lax.broadcasted_iota(I32, (ATTN_GROUP * WINDOW, 2 * WINDOW), 1)
    qi = rows & (WINDOW - 1)
    valid = (cols > qi) & (cols <= qi + WINDOW) & ((cols >= WINDOW) | (n > 0))
    grp = lax.broadcasted_iota(I32, (ATTN_GROUP * WINDOW, 1), 0) // WINDOW
    for j in range(ATTN_KV_HEADS):
        ks = slice(j * ATTN_HEAD_DIM, (j + 1) * ATTN_HEAD_DIM)
        vs = slice(KV_WIDTH + j * ATTN_HEAD_DIM, KV_WIDTH + (j + 1) * ATTN_HEAD_DIM)
        k = jnp.concatenate([kvp[:, ks], kvc[:, ks]], axis=0)
        v = jnp.concatenate([kvp[:, vs], kvc[:, vs]], axis=0)
        heads = [j * ATTN_GROUP + g for g in range(ATTN_GROUP)]
        q4 = jnp.concatenate([q[:, hh * ATTN_HEAD_DIM:(hh + 1) * ATTN_HEAD_DIM] for hh in heads], axis=0)
        s = jnp.where(valid, _dot_nt(q4, k), MASK_VALUE)
        sink = jnp.zeros((ATTN_GROUP * WINDOW, 1), F32)
        for g, hh in enumerate(heads):
            sink = jnp.where(grp == g, sink_ref[hh], sink)
        m = jnp.maximum(jnp.max(s, axis=-1, keepdims=True), sink)
        e = jnp.exp(s - m)
        denom = jnp.sum(e, axis=-1, keepdims=True) + jnp.exp(sink - m)
        p = (e * (1.0 / denom)).astype(BF16)
        o = _dot(p, v)
        for g, hh in enumerate(heads):
            o_ref[:, hh * ATTN_HEAD_DIM:(hh + 1) * ATTN_HEAD_DIM] = o[g * WINDOW:(g + 1) * WINDOW].astype(BF16)


def _attention(za, sinks, B, S):
    nb = S // WINDOW
    kvblk = ATTN_WIDTH // (2 * KV_WIDTH)
    return pl.pallas_call(
        _attn_kernel,
        grid=(B, nb),
        in_specs=[
            pl.BlockSpec(memory_space=pltpu.SMEM),
            pl.BlockSpec((WINDOW, ATTN_WIDTH), lambda b, n: (b * nb + n, 0)),
            pl.BlockSpec((WINDOW, 2 * KV_WIDTH), lambda b, n: (b * nb + n, kvblk)),
            pl.BlockSpec((WINDOW, 2 * KV_WIDTH), lambda b, n: (b * nb + jnp.maximum(n - 1, 0), kvblk)),
        ],
        out_specs=pl.BlockSpec((WINDOW, ATTN_WIDTH), lambda b, n: (b * nb + n, 0)),
        out_shape=jax.ShapeDtypeStruct((B * S, ATTN_WIDTH), BF16),
        compiler_params=pltpu.CompilerParams(dimension_semantics=("parallel", "arbitrary")),
        name="swa_attention",
    )(sinks, za, za, za)


def _hgrn_levels(C):
    out, s = [], C // 2
    while s >= 1:
        out.append(s)
        s //= 2
    return out


def _hgrn_constants(C):
    r = np.arange(C)[:, None]
    u = np.arange(C)[None, :]
    mats = [u <= r, u > r]
    masks = []
    for s in _hgrn_levels(C):
        mid = (r // (2 * s)) * 2 * s + s
        upper = r >= mid
        mats.append(np.where(upper, (u >= mid) & (u <= r), (u > r) & (u < mid)))
        masks.append((r // (2 * s)) == (u // (2 * s)))
    masks.append(r == u)
    return (np.concatenate(mats, axis=0).astype(np.float32), np.stack(masks).astype(np.float32))


def _hgrn_kernel(zr_ref, zf_ref, lb_ref, gain_ref, msum_ref, pmask_ref, o_ref, st_ref):
    C = zf_ref.shape[0]
    levels = _hgrn_levels(C)

    @pl.when(pl.program_id(1) == 0)
    def _():
        st_ref[...] = jnp.zeros_like(st_ref)

    row = lax.broadcasted_iota(I32, (C, REC_DIM), 0)
    for hd in range(REC_HEADS):
        cs = slice(hd * REC_DIM, (hd + 1) * REC_DIM)
        zq = zr_ref[:, cs].astype(F32)
        v = zr_ref[:, REC_WIDTH + hd * REC_DIM:REC_WIDTH + (hd + 1) * REC_DIM]
        og = zr_ref[:, 2 * REC_WIDTH + hd * REC_DIM:2 * REC_WIDTH + (hd + 1) * REC_DIM].astype(F32)
        z = zf_ref[:, cs]
        lb = lb_ref[:, cs]
        qp = zq * _sigmoid(zq)
        a = jnp.exp(-jnp.abs(z))
        r = 1.0 / (1.0 + a)
        pos = z >= 0
        g = jnp.log(lb + (1.0 - lb) * jnp.where(pos, r, a * r))
        k = (1.0 - lb) * jnp.where(pos, a * r, r)
        g_hi = g.astype(BF16)
        g_lo = (g - g_hi.astype(F32)).astype(BF16)
        x2 = _dot(msum_ref[...], jnp.concatenate([g_hi, g_lo], axis=1))
        e_all = jnp.exp(x2[:, :REC_DIM] + x2[:, REC_DIM:])
        e_b = e_all[0:C]
        e_e = e_all[C:2 * C]
        st = st_ref[hd]
        o = _dot_nt((qp * e_b).astype(BF16), st.astype(BF16))
        kb = k.astype(BF16)
        amat = pmask_ref[len(levels)] * _dot_nt(qp.astype(BF16), kb)
        for li, s in enumerate(levels):
            e_l = e_all[(2 + li) * C:(3 + li) * C]
            upper = (row & s) != 0
            ql = jnp.where(upper, qp * e_l, 0.0).astype(BF16)
            kl = jnp.where(upper, 0.0, k * e_l).astype(BF16)
            amat = amat + pmask_ref[li] * _dot_nt(ql, kl)
        o = o + _dot(amat.astype(BF16), v)
        st_ref[hd] = st * e_b[C - 1:C, :] + _dot_tn(v, (k * e_e).astype(BF16))
        og_act = og * _sigmoid(og)
        o_ref[:, cs] = (_rms(o, gain_ref[...]) * og_act).astype(BF16)


def _hgrn2(zr, zf, lb, gain, msum, pmask, B, S):
    C = min(REC_CHUNK, S)
    nc = S // C
    row = lambda b, c: (b * nc + c, 0)
    return pl.pallas_call(
        _hgrn_kernel,
        grid=(B, nc),
        in_specs=[
            pl.BlockSpec((C, 3 * REC_WIDTH), row),
            pl.BlockSpec((C, REC_WIDTH), row),
            pl.BlockSpec((1, REC_WIDTH), lambda b, c: (0, 0)),
            pl.BlockSpec((1, REC_DIM), lambda b, c: (0, 0)),
            pl.BlockSpec(msum.shape, lambda b, c: (0, 0)),
            pl.BlockSpec(pmask.shape, lambda b, c: (0, 0, 0)),
        ],
        out_specs=pl.BlockSpec((C, REC_WIDTH), row),
        out_shape=jax.ShapeDtypeStruct((B * S, REC_WIDTH), BF16),
        scratch_shapes=[pltpu.VMEM((REC_HEADS, REC_DIM, REC_DIM), F32)],
        compiler_params=pltpu.CompilerParams(dimension_semantics=("parallel", "arbitrary")),
        name="hgrn2",
    )(zr, zf, lb, gain, msum, pmask)


def _mix_out_kernel(attn_ref, rec_ref, h_ref, wo_ref, gain_ref, wrh_ref, wrl_ref, tril_ref,
                    h1_ref, xn_ref, ri_ref, rw_ref, cnt_ref, carry_ref):
    @pl.when(pl.program_id(0) == 0)
    def _():
        carry_ref[...] = jnp.zeros_like(carry_ref)

    sub = tril_ref.shape[0]
    carry = carry_ref[...]
    for s in range(h_ref.shape[0] // sub):
        carry = _mix_out_rows(slice(s * sub, (s + 1) * sub), s * sub, carry, attn_ref, rec_ref, h_ref, wo_ref,
                              gain_ref, wrh_ref, wrl_ref, tril_ref, h1_ref, xn_ref, ri_ref, rw_ref)
    carry_ref[...] = carry
    cnt_ref[...] = jnp.broadcast_to(carry, cnt_ref.shape)


def _mix_out_rows(rows, first, carry, attn_ref, rec_ref, h_ref, wo_ref, gain_ref, wrh_ref, wrl_ref, tril_ref,
                  h1_ref, xn_ref, ri_ref, rw_ref):
    mixed = jnp.concatenate([attn_ref[rows, :], rec_ref[rows, :]], axis=1)
    h1 = h_ref[rows, :] + _dot(mixed, wo_ref[...])
    h1_ref[rows, :] = h1
    xn = _rms(h1, gain_ref[...])
    _store_row_tiles(xn_ref, xn, first=first)
    xh = xn.astype(BF16)
    xl = (xn - xh.astype(F32)).astype(BF16)
    logits = _dot(xh, wrh_ref[...]) + _dot(xl, wrh_ref[...]) + _dot(xh, wrl_ref[...])
    lane = lax.broadcasted_iota(I32, logits.shape, 1)
    lanef = lane.astype(F32)
    neg = jnp.float32(-jnp.inf)
    big = jnp.float32(1e9)
    gl = jnp.where(lane < N_GROUPS, logits, neg)
    gmax = jnp.max(gl, axis=-1, keepdims=True)
    gidx = jnp.min(jnp.where(gl == gmax, lanef, big), axis=-1, keepdims=True)
    p_group = 1.0 / jnp.sum(jnp.where(lane < N_GROUPS, jnp.exp(logits - gmax), 0.0), axis=-1, keepdims=True)
    lo = N_GROUPS + gidx * EXPERTS_PER_GROUP
    el = jnp.where((lanef >= lo) & (lanef < lo + EXPERTS_PER_GROUP), logits, neg)
    t1 = jnp.max(el, axis=-1, keepdims=True)
    i1 = jnp.min(jnp.where(el == t1, lanef, big), axis=-1, keepdims=True)
    el2 = jnp.where(lanef == i1, neg, el)
    t2 = jnp.max(el2, axis=-1, keepdims=True)
    i2 = jnp.min(jnp.where(el2 == t2, lanef, big), axis=-1, keepdims=True)
    r21 = jnp.exp(t2 - t1)
    w1 = p_group / (1.0 + r21)
    w2 = w1 * r21
    e1 = i1 - N_GROUPS
    e2 = i2 - N_GROUPS
    oh1 = (lanef == e1).astype(F32)
    oh2 = (lanef == e2).astype(F32)
    oh = oh1 + oh2
    prefix = _dot(tril_ref[...], oh.astype(BF16)) + carry
    rank1 = jnp.sum(prefix * oh1, axis=-1, keepdims=True)
    rank2 = jnp.sum(prefix * oh2, axis=-1, keepdims=True)
    ri = jnp.where(lane == 0, e1, jnp.where(lane == 1, e2, jnp.where(lane == 2, rank1, jnp.where(lane == 3, rank2, 0.0))))
    ri_ref[rows, :] = ri.astype(I32)
    rw_ref[rows, :] = jnp.where(lane == 0, w1, jnp.where(lane == 1, w2, 0.0))
    return carry + jnp.sum(oh, axis=0, keepdims=True)


def _mix_out(attn, rec, h, wo, gain, wrh, wrl, tril):
    T = h.shape[0]
    tm = min(TOKEN_TILE, T)
    row = lambda i: (i, 0)
    fixed = lambda i: (0, 0)
    return pl.pallas_call(
        _mix_out_kernel,
        grid=(T // tm,),
        in_specs=[
            pl.BlockSpec((tm, ATTN_WIDTH), row),
            pl.BlockSpec((tm, REC_WIDTH), row),
            pl.BlockSpec((tm, D_MODEL), row),
            pl.BlockSpec((ATTN_WIDTH + REC_WIDTH, D_MODEL), fixed),
            pl.BlockSpec((1, D_MODEL), fixed),
            pl.BlockSpec((D_MODEL, ROUTE_LANES), fixed),
            pl.BlockSpec((D_MODEL, ROUTE_LANES), fixed),
            pl.BlockSpec(tril.shape, fixed),
        ],
        out_specs=[
            pl.BlockSpec((tm, D_MODEL), row),
            pl.BlockSpec((tm * ROW_SUBLANES, LANES), row),
            pl.BlockSpec((tm, ROUTE_LANES), row),
            pl.BlockSpec((tm, ROUTE_LANES), row),
            pl.BlockSpec((8, ROUTE_LANES), fixed),
        ],
        out_shape=[
            jax.ShapeDtypeStruct((T, D_MODEL), F32),
            jax.ShapeDtypeStruct((T * ROW_SUBLANES, LANES), U32),
            jax.ShapeDtypeStruct((T, ROUTE_LANES), I32),
            jax.ShapeDtypeStruct((T, ROUTE_LANES), F32),
            jax.ShapeDtypeStruct((8, ROUTE_LANES), F32),
        ],
        scratch_shapes=[pltpu.VMEM((1, ROUTE_LANES), F32)],
        compiler_params=pltpu.CompilerParams(dimension_semantics=("arbitrary",)),
        name="mix_out_router",
    )(attn, rec, h, wo, gain, wrh, wrl, tril)


def _row_move_loops(copy, tokens):
    def start(t, carry):
        copy(t, 0).start()
        copy(t, 1).start()
        return carry

    def wait(t, carry):
        copy(t, 0).wait()
        copy(t, 1).wait()
        return carry

    lax.fori_loop(0, tokens, start, 0, unroll=COPY_UNROLL)
    lax.fori_loop(0, tokens, wait, 0, unroll=COPY_UNROLL)


def _dispatch_kernel(dest_ref, bound_ref, x_ref, out_ref, zero_ref, sem, *, tokens):
    @pl.when(pl.program_id(0) == 0)
    def _():
        zero_ref[...] = jnp.zeros_like(zero_ref)

        def fill(e):
            return pltpu.make_async_copy(zero_ref, out_ref.at[pl.ds(bound_ref[e + 1] - EXPERT_BLOCK, EXPERT_BLOCK)], sem)

        for e in range(N_EXPERTS):
            pl.when(bound_ref[e + 1] > bound_ref[e])(lambda e=e: fill(e).start())
        for e in range(N_EXPERTS):
            pl.when(bound_ref[e + 1] > bound_ref[e])(lambda e=e: fill(e).wait())

        def tail(i):
            return pltpu.make_async_copy(zero_ref, out_ref.at[pl.ds(i * EXPERT_BLOCK, EXPERT_BLOCK)], sem)

        first_unused = bound_ref[N_EXPERTS] // EXPERT_BLOCK
        n_blocks = out_ref.shape[0] // EXPERT_BLOCK
        lax.fori_loop(first_unused, n_blocks, lambda i, c: (tail(i).start(), c)[1], 0)
        lax.fori_loop(first_unused, n_blocks, lambda i, c: (tail(i).wait(), c)[1], 0)

    base = 2 * pl.program_id(0) * tokens

    def copy(t, j):
        return pltpu.make_async_copy(x_ref.at[t], out_ref.at[dest_ref[base + 2 * t + j]], sem)

    _row_move_loops(copy, tokens)


def _dispatch(dest, bounds, x3, n_rows):
    tokens = min(COPY_TOKENS, x3.shape[0])
    tile = x3.shape[1:]
    return pl.pallas_call(
        functools.partial(_dispatch_kernel, tokens=tokens),
        grid_spec=pltpu.PrefetchScalarGridSpec(
            num_scalar_prefetch=2,
            grid=(x3.shape[0] // tokens,),
            in_specs=[pl.BlockSpec((tokens,) + tile, lambda i, d, b: (i, 0, 0))],
            out_specs=pl.BlockSpec(memory_space=pl.ANY),
            scratch_shapes=[pltpu.VMEM((EXPERT_BLOCK,) + tile, x3.dtype), pltpu.SemaphoreType.DMA],
        ),
        out_shape=jax.ShapeDtypeStruct((n_rows,) + tile, x3.dtype),
        compiler_params=pltpu.CompilerParams(dimension_semantics=("arbitrary",)),
        name="moe_dispatch_rows",
    )(dest, bounds, x3)


def _expert_kernel(be_ref, nused_ref, x_ref, wg_ref, wu_ref, wd_ref, y_ref, wgb_ref, wub_ref, wdb_ref):
    i = pl.program_id(0)
    used = i < nused_ref[0]

    @pl.when(used & ((i == 0) | (be_ref[i] != be_ref[jnp.maximum(i - 1, 0)])))
    def _():
        wgb_ref[...] = wg_ref[0].astype(BF16)
        wub_ref[...] = wu_ref[0].astype(BF16)
        wdb_ref[...] = wd_ref[0].astype(BF16)

    @pl.when(used)
    def _():
        for s in range(EXPERT_BLOCK // EXPERT_SUB):
            x = _load_row_tiles(x_ref, EXPERT_SUB, first=s * EXPERT_SUB).astype(BF16)
            gate = _dot(x, wgb_ref[...])
            up = _dot(x, wub_ref[...])
            hidden = (gate * _sigmoid(gate) * up).astype(BF16)
            _store_row_tiles(y_ref, _dot(hidden, wdb_ref[...]), first=s * EXPERT_SUB)

    @pl.when(jnp.logical_not(used))
    def _():
        y_ref[...] = jnp.zeros_like(y_ref)


def _experts(block_expert, n_used, xs, wg, wu, wd, layer):
    blk = EXPERT_BLOCK * ROW_SUBLANES
    nblk = xs.shape[0] // blk

    def xmap(i, be, nu):
        return (jnp.minimum(i, nu[0] - 1), 0)

    def wmap(i, be, nu):
        return (layer * N_EXPERTS + be[jnp.minimum(i, nu[0] - 1)], 0, 0)

    return pl.pallas_call(
        _expert_kernel,
        grid_spec=pltpu.PrefetchScalarGridSpec(
            num_scalar_prefetch=2,
            grid=(nblk,),
            in_specs=[
                pl.BlockSpec((blk, LANES), xmap),
                pl.BlockSpec((1, D_MODEL, EXPERT_FF), wmap),
                pl.BlockSpec((1, D_MODEL, EXPERT_FF), wmap),
                pl.BlockSpec((1, EXPERT_FF, D_MODEL), wmap),
            ],
            out_specs=pl.BlockSpec((blk, LANES), lambda i, be, nu: (i, 0)),
            scratch_shapes=[
                pltpu.VMEM((D_MODEL, EXPERT_FF), BF16),
                pltpu.VMEM((D_MODEL, EXPERT_FF), BF16),
                pltpu.VMEM((EXPERT_FF, D_MODEL), BF16),
            ],
        ),
        out_shape=jax.ShapeDtypeStruct(xs.shape, xs.dtype),
        compiler_params=pltpu.CompilerParams(dimension_semantics=("arbitrary",)),
        name="moe_experts",
    )(block_expert, n_used, xs, wg, wu, wd)


def _ple_kernel(dest_ref, h1_ref, ys_ref, rw_ref, p_ref, wple_ref, pgain_ref, ggain_ref, wpg_ref, o_ref,
                ybuf_ref, sem):
    i = pl.program_id(0)
    tm = h1_ref.shape[0]
    slot = i % 2

    def row_copy(tile, buf, t, j):
        a = 2 * t + j
        dst = ybuf_ref.at[buf, pl.ds(pl.multiple_of(a * ROW_SUBLANES, ROW_SUBLANES), ROW_SUBLANES), :]
        return pltpu.make_async_copy(ys_ref.at[dest_ref[2 * tile * tm + a]], dst, sem.at[buf])

    def start_tile(tile, buf):
        def body(t, carry):
            row_copy(tile, buf, t, 0).start()
            row_copy(tile, buf, t, 1).start()
            return carry
        lax.fori_loop(0, tm, body, 0, unroll=COPY_UNROLL)

    def wait_tile(tile, buf):
        def body(t, carry):
            row_copy(tile, buf, t, 0).wait()
            row_copy(tile, buf, t, 1).wait()
            return carry
        lax.fori_loop(0, tm, body, 0, unroll=COPY_UNROLL)

    pl.when(i == 0)(lambda: start_tile(0, 0))
    pl.when(i + 1 < pl.num_programs(0))(lambda: start_tile(i + 1, 1 - slot))
    wait_tile(i, slot)

    yg_ref = ybuf_ref.at[slot]
    sub = min(SUB_TILE, tm)
    for s in range(tm // sub):
        rows = slice(s * sub, (s + 1) * sub)
        rw = rw_ref[rows, :]
        y1 = _load_row_tiles(yg_ref, sub, first=2 * s * sub, every=2)
        y2 = _load_row_tiles(yg_ref, sub, first=2 * s * sub + 1, every=2)
        h2 = h1_ref[rows, :] + rw[:, 0:1] * y1 + rw[:, 1:2] * y2
        ple = _rms(_dot(p_ref[rows, :].astype(BF16), wple_ref[...]), pgain_ref[...])
        gate = _sigmoid(_dot(_rms(h2, ggain_ref[...]).astype(BF16), wpg_ref[...]))
        o_ref[rows, :] = h2 + ple * gate


def _ple(dest, h1, ys3, rw, p, wple, pgain, ggain, wpg, layer):
    T = h1.shape[0]
    tm = min(TOKEN_TILE, T)
    nt = T // tm
    row = lambda i, d: (i, 0)
    fixed = lambda i, d: (0, 0)
    return pl.pallas_call(
        _ple_kernel,
        grid_spec=pltpu.PrefetchScalarGridSpec(
            num_scalar_prefetch=1,
            grid=(nt,),
            in_specs=[
                pl.BlockSpec((tm, D_MODEL), row),
                pl.BlockSpec(memory_space=pl.ANY),
                pl.BlockSpec((tm, ROUTE_LANES), row),
                pl.BlockSpec((tm, PLE_DIM), lambda i, d: (layer * nt + i, 0)),
                pl.BlockSpec((PLE_DIM, D_MODEL), fixed),
                pl.BlockSpec((1, D_MODEL), fixed),
                pl.BlockSpec((1, D_MODEL), fixed),
                pl.BlockSpec((D_MODEL, D_MODEL), fixed),
            ],
            out_specs=pl.BlockSpec((tm, D_MODEL), row),
            scratch_shapes=[pltpu.VMEM((2, 2 * tm * ROW_SUBLANES, LANES), ys3.dtype), pltpu.SemaphoreType.DMA((2,))],
        ),
        out_shape=jax.ShapeDtypeStruct((T, D_MODEL), F32),
        compiler_params=pltpu.CompilerParams(dimension_semantics=("arbitrary",)),
        name="combine_ple",
    )(dest, h1, ys3, rw, p, wple, pgain, ggain, wpg)


def _rope_tables(positions):
    inv_freq = ROPE_THETA ** (-jnp.arange(0, ROPE_DIM, 2, dtype=F32) / ROPE_DIM)
    ang = positions.astype(F32).reshape(-1, 1) * inv_freq
    cos, sin = jnp.cos(ang), jnp.sin(ang)
    T = ang.shape[0]
    rest = jnp.zeros((T, ATTN_HEAD_DIM - ROPE_DIM), F32)
    zero = jnp.zeros((T, ROPE_HALF), F32)
    rc = jnp.concatenate([cos, cos, rest + 1.0], axis=1)
    rs1 = jnp.concatenate([-sin, zero, rest], axis=1)
    rs2 = jnp.concatenate([zero, sin, rest], axis=1)
    reps = LANES // ATTN_HEAD_DIM
    return tuple(jnp.tile(t, (1, reps)) for t in (rc, rs1, rs2))


def kernel(x, p, positions, mix_norm, w_in, q_norm, k_norm, sinks, lb_logits, rec_norm, w_out, ffn_norm,
           w_router_group, w_router_expert, w_gate, w_up, w_down, w_ple, ple_norm, ple_gate_norm, w_ple_gate):
    B, S, D = x.shape
    depth = w_in.shape[0]
    T = B * S
    n_assign = 2 * T
    assert D == D_MODEL and S % WINDOW == 0 and T % min(TOKEN_TILE, T) == 0

    rc, rs1, rs2 = _rope_tables(positions)
    lb_sm = jax.nn.softmax(lb_logits.astype(F32), axis=0)
    lower_bounds = jnp.cumsum(lb_sm, axis=0) - lb_sm[0:1]

    a0 = QKV_WIDTH
    w_in_p = jnp.concatenate([w_in[:, :, :a0 + REC_WIDTH], w_in[:, :, a0 + 2 * REC_WIDTH:],
                              w_in[:, :, a0 + REC_WIDTH:a0 + 2 * REC_WIDTH]], axis=2).astype(BF16)
    w_out_b = w_out.astype(BF16)
    wg_all = w_gate.reshape(depth * N_EXPERTS, D_MODEL, EXPERT_FF)
    wu_all = w_up.reshape(depth * N_EXPERTS, D_MODEL, EXPERT_FF)
    wd_all = w_down.reshape(depth * N_EXPERTS, EXPERT_FF, D_MODEL)
    w_ple_b = w_ple.astype(BF16)
    w_pg_b = w_ple_gate.astype(BF16)
    w_r = jnp.concatenate([w_router_group, w_router_expert,
                           jnp.zeros((depth, D_MODEL, ROUTE_LANES - N_GROUPS - N_EXPERTS), F32)], axis=2)
    w_r_hi = w_r.astype(BF16)
    w_r_lo = (w_r - w_r_hi.astype(F32)).astype(BF16)
    qk_gain = jnp.concatenate([jnp.tile(q_norm, (1, ATTN_HEADS)) * (ATTN_HEAD_DIM ** -0.5),
                               jnp.tile(k_norm, (1, ATTN_KV_HEADS))], axis=1)
    seg_id = np.arange(QK_WIDTH) // ATTN_HEAD_DIM
    seg = jnp.asarray(seg_id[:, None] == seg_id[None, :], BF16)
    msum_np, pmask_np = _hgrn_constants(min(REC_CHUNK, S))
    msum = jnp.asarray(msum_np, BF16)
    pmask = jnp.asarray(pmask_np, F32)
    sub = min(SUB_TILE, T)
    tril = jnp.asarray(np.tril(np.ones((sub, sub), np.float32), -1), BF16)

    n_rows = n_assign + N_EXPERTS * EXPERT_BLOCK
    nblk = n_rows // EXPERT_BLOCK
    eids = jnp.arange(N_EXPERTS, dtype=I32)
    p2 = p.reshape(depth * T, PLE_DIM)

    h = x.reshape(T, D)
    for l in range(depth):
        za, zr, zf = _mix_in(h, mix_norm[l][None], w_in_p[l], rc, rs1, rs2, qk_gain[l][None], seg)
        attn = _attention(za, sinks[l], B, S)
        rec = _hgrn2(zr, zf, lower_bounds[l][None], rec_norm[l][None], msum, pmask, B, S)
        h1, xn, ri, rw, cnt = _mix_out(attn, rec, h, w_out_b[l], ffn_norm[l][None], w_r_hi[l], w_r_lo[l], tril)
        counts = cnt[0, :N_EXPERTS].astype(I32)
        padded = ((counts + EXPERT_BLOCK - 1) // EXPERT_BLOCK) * EXPERT_BLOCK
        pad_end = jnp.cumsum(padded)
        pad_start = pad_end - padded
        expert = ri[:, 0:2]
        dest = ri[:, 2:4] + jnp.sum(jnp.where(expert[:, :, None] == eids, pad_start, 0), axis=-1)
        dest = dest.reshape(n_assign).astype(I32)
        n_used = (pad_end[-1] // EXPERT_BLOCK).astype(I32).reshape(1)
        blk_start = jnp.arange(nblk, dtype=I32) * EXPERT_BLOCK
        block_expert = jnp.minimum(jnp.sum(pad_end[None, :] <= blk_start[:, None], axis=1), N_EXPERTS - 1).astype(I32)
        bounds = jnp.concatenate([jnp.zeros((1,), I32), pad_end.astype(I32)])
        xs = _dispatch(dest, bounds, xn.reshape(T, ROW_SUBLANES, LANES), n_rows)
        ys = _experts(block_expert, n_used, xs.reshape(n_rows * ROW_SUBLANES, LANES), wg_all, wu_all, wd_all, l)
        h = _ple(dest, h1, ys.reshape(n_rows, ROW_SUBLANES, LANES), rw, p2, w_ple_b[l], ple_norm[l][None],
                 ple_gate_norm[l][None], w_pg_b[l], l)
    return h.reshape(B, S, D)
```

```python
import functools

import jax
import jax.numpy as jnp
import numpy as np
from jax import lax
from jax.experimental import pallas as pl
from jax.experimental.pallas import tpu as pltpu

F32 = jnp.float32
BF16 = jnp.bfloat16
I32 = jnp.int32
U32 = jnp.uint32

D_MODEL = 1024
ATTN_HEADS = 8
ATTN_KV_HEADS = 2
ATTN_HEAD_DIM = 64
ATTN_GROUP = ATTN_HEADS // ATTN_KV_HEADS
ATTN_WIDTH = ATTN_HEADS * ATTN_HEAD_DIM
KV_WIDTH = ATTN_KV_HEADS * ATTN_HEAD_DIM
QK_WIDTH = ATTN_WIDTH + KV_WIDTH
QKV_WIDTH = ATTN_WIDTH + 2 * KV_WIDTH
WINDOW = 128
ATTN_STEP_BLOCKS = 4
ROPE_THETA = 500000.0
ROPE_DIM = ATTN_HEAD_DIM // 4
ROPE_HALF = ROPE_DIM // 2
REC_HEADS = 4
REC_DIM = 128
REC_WIDTH = REC_HEADS * REC_DIM
REC_CHUNK = 128
HGRN_BROADCAST_MIN = 8
HGRN_BATCH_ROWS = 2
IN_WIDTH = QKV_WIDTH + 4 * REC_WIDTH
N_GROUPS = 4
EXPERTS_PER_GROUP = 8
N_EXPERTS = N_GROUPS * EXPERTS_PER_GROUP
EXPERT_FF = 512
PLE_DIM = 256
RMS_EPS = 1e-6
MASK_VALUE = -1e30
LANES = 128
ROUTE_LANES = LANES
ROW_WORDS = D_MODEL // 2
ROW_SUBLANES = ROW_WORDS // LANES
HIGH_HALF = np.uint32(0xFFFF0000)

TOKEN_TILE = 512
SUB_TILE = 256
EXPERT_BLOCK = 256
EXPERT_SUB = 128
COPY_TOKENS = 512
COPY_UNROLL = 8


def _dot(a, b):
    return jnp.dot(a, b, preferred_element_type=F32)


def _dot_nt(a, b):
    return lax.dot_general(a, b, (((1,), (1,)), ((), ())), preferred_element_type=F32)


def _dot_tn(a, b):
    return lax.dot_general(a, b, (((0,), (0,)), ((), ())), preferred_element_type=F32)


def _store_row_tiles(ref, x, first=0):
    m = x.shape[0]
    lo = lax.bitcast_convert_type(x[:, :ROW_WORDS].astype(BF16).astype(F32), U32) >> 16
    hi = lax.bitcast_convert_type(x[:, ROW_WORDS:].astype(BF16).astype(F32), U32) & HIGH_HALF
    words = lo | hi
    for c in range(ROW_SUBLANES):
        ref[pl.ds(first * ROW_SUBLANES + c, m, stride=ROW_SUBLANES), :] = words[:, c * LANES:(c + 1) * LANES]


def _load_row_tiles(ref, m, first=0, every=1):
    words = jnp.concatenate(
        [ref[pl.ds(first * ROW_SUBLANES + c, m, stride=every * ROW_SUBLANES), :] for c in range(ROW_SUBLANES)], axis=1)
    lo = lax.bitcast_convert_type(words << 16, F32)
    hi = lax.bitcast_convert_type(words & HIGH_HALF, F32)
    return jnp.concatenate([lo, hi], axis=1)


def _sigmoid(x):
    return 1.0 / (1.0 + jnp.exp(-x))


def _rms(x, gain):
    ms = jnp.mean(x * x, axis=-1, keepdims=True)
    return x * lax.rsqrt(ms + RMS_EPS) * gain


def _mix_in_kernel(h_ref, gain_ref, w_ref, rc_ref, rs1_ref, rs2_ref, qkg_ref, seg_ref,
                   za_ref, zr_ref, zf_ref):
    xn = _rms(h_ref[...], gain_ref[...]).astype(BF16)
    z_a = _dot(xn, w_ref[:, 0:QKV_WIDTH])
    qk = z_a[:, 0:QK_WIDTH]
    seg = _dot((qk * qk).astype(BF16), seg_ref[...]) * (1.0 / ATTN_HEAD_DIM)
    qkn = qk * lax.rsqrt(seg + RMS_EPS) * qkg_ref[...]
    rc, rs1, rs2 = rc_ref[...], rs1_ref[...], rs2_ref[...]
    for c in range(QK_WIDTH // LANES):
        col = qkn[:, c * LANES:(c + 1) * LANES]
        rot = col * rc + pltpu.roll(col, LANES - ROPE_HALF, 1) * rs1 + pltpu.roll(col, ROPE_HALF, 1) * rs2
        za_ref[:, c * LANES:(c + 1) * LANES] = rot.astype(BF16)
    za_ref[:, QK_WIDTH:QKV_WIDTH] = z_a[:, QK_WIDTH:QKV_WIDTH].astype(BF16)
    zr_ref[...] = _dot(xn, w_ref[:, QKV_WIDTH:QKV_WIDTH + 3 * REC_WIDTH]).astype(BF16)
    zf_ref[...] = _dot(xn, w_ref[:, QKV_WIDTH + 3 * REC_WIDTH:IN_WIDTH])


def _mix_in(h, gain, w, rc, rs1, rs2, qkg, seg):
    T = h.shape[0]
    tm = min(TOKEN_TILE, T)
    row = lambda i: (i, 0)
    fixed = lambda i: (0, 0)
    return pl.pallas_call(
        _mix_in_kernel,
        grid=(T // tm,),
        in_specs=[
            pl.BlockSpec((tm, D_MODEL), row),
            pl.BlockSpec((1, D_MODEL), fixed),
            pl.BlockSpec((D_MODEL, IN_WIDTH), fixed),
            pl.BlockSpec((tm, LANES), row),
            pl.BlockSpec((tm, LANES), row),
            pl.BlockSpec((tm, LANES), row),
            pl.BlockSpec((1, QK_WIDTH), fixed),
            pl.BlockSpec((QK_WIDTH, QK_WIDTH), fixed),
        ],
        out_specs=[
            pl.BlockSpec((tm, QKV_WIDTH), row),
            pl.BlockSpec((tm, 3 * REC_WIDTH), row),
            pl.BlockSpec((tm, REC_WIDTH), row),
        ],
        out_shape=[
            jax.ShapeDtypeStruct((T, QKV_WIDTH), BF16),
            jax.ShapeDtypeStruct((T, 3 * REC_WIDTH), BF16),
            jax.ShapeDtypeStruct((T, REC_WIDTH), F32),
        ],
        compiler_params=pltpu.CompilerParams(dimension_semantics=("parallel",)),
        name="mix_in",
    )(h, gain, w, rc, rs1, rs2, qkg, seg)


def _attn_kernel(sink_ref, q_ref, kvc_ref, kvp_ref, o_ref):
    n = pl.program_id(1)
    rows = lax.broadcasted_iota(I32, (ATTN_GROUP * WINDOW, 2 * WINDOW), 0)
    cols = lax.broadcasted_iota(I32, (ATTN_GROUP * WINDOW, 2 * WINDOW), 1)
    qi = rows & (WINDOW - 1)
    in_window = (cols > qi) & (cols <= qi + WINDOW)
    grp = lax.broadcasted_iota(I32, (ATTN_GROUP * WINDOW, 1), 0) // WINDOW
    for qb in range(q_ref.shape[0] // WINDOW):
        blk = slice(qb * WINDOW, (qb + 1) * WINDOW)
        q = q_ref[blk, :]
        kvc = kvc_ref[blk, :]
        kvp = kvp_ref[...] if qb == 0 else kvc_ref[(qb - 1) * WINDOW:qb * WINDOW, :]
        valid = in_window & ((cols >= WINDOW) | (n > 0)) if qb == 0 else in_window
        for j in range(ATTN_KV_HEADS):
            ks = slice(j * ATTN_HEAD_DIM, (j + 1) * ATTN_HEAD_DIM)
            vs = slice(KV_WIDTH + j * ATTN_HEAD_DIM, KV_WIDTH + (j + 1) * ATTN_HEAD_DIM)
            k = jnp.concatenate([kvp[:, ks], kvc[:, ks]], axis=0)
            v = jnp.concatenate([kvp[:, vs], kvc[:, vs]], axis=0)
            heads = [j * ATTN_GROUP + g for g in range(ATTN_GROUP)]
            q4 = jnp.concatenate([q[:, hh * ATTN_HEAD_DIM:(hh + 1) * ATTN_HEAD_DIM] for hh in heads], axis=0)
            s = jnp.where(valid, _dot_nt(q4, k), MASK_VALUE)
            sink = jnp.zeros((ATTN_GROUP * WINDOW, 1), F32)
            for g, hh in enumerate(heads):
                sink = jnp.where(grp == g, sink_ref[hh], sink)
            m = jnp.maximum(jnp.max(s, axis=-1, keepdims=True), sink)
            e = jnp.exp(s - m)
            denom = jnp.sum(e, axis=-1, keepdims=True) + jnp.exp(sink - m)
            p = (e * (1.0 / denom)).astype(BF16)
            o = _dot(p, v)
            for g, hh in enumerate(heads):
                o_ref[blk, hh * ATTN_HEAD_DIM:(hh + 1) * ATTN_HEAD_DIM] = o[g * WINDOW:(g + 1) * WINDOW].astype(BF16)


def _attention(za, sinks, B, S):
    nq = ATTN_STEP_BLOCKS if (S // WINDOW) % ATTN_STEP_BLOCKS == 0 else 1
    ns = S // (nq * WINDOW)
    kvblk = ATTN_WIDTH // (2 * KV_WIDTH)
    return pl.pallas_call(
        _attn_kernel,
        grid=(B, ns),
        in_specs=[
            pl.BlockSpec(memory_space=pltpu.SMEM),
            pl.BlockSpec((nq * WINDOW, ATTN_WIDTH), lambda b, n: (b * ns + n, 0)),
            pl.BlockSpec((nq * WINDOW, 2 * KV_WIDTH), lambda b, n: (b * ns + n, kvblk)),
            pl.BlockSpec((WINDOW, 2 * KV_WIDTH), lambda b, n: ((b * ns + n) * nq - jnp.minimum(n, 1), kvblk)),
        ],
        out_specs=pl.BlockSpec((nq * WINDOW, ATTN_WIDTH), lambda b, n: (b * ns + n, 0)),
        out_shape=jax.ShapeDtypeStruct((B * S, ATTN_WIDTH), BF16),
        compiler_params=pltpu.CompilerParams(dimension_semantics=("parallel", "arbitrary")),
        name="swa_attention",
    )(sinks, za, za, za)


def _hgrn_levels(C):
    out, s = [], C // 2
    while s >= 1:
        out.append(s)
        s //= 2
    return out


def _hgrn_constants(C):
    r = np.arange(C)[:, None]
    u = np.arange(C)[None, :]
    mats = [u <= r]
    masks = []
    for s in _hgrn_levels(C):
        mid = (r // (2 * s)) * 2 * s + s
        if s < HGRN_BROADCAST_MIN:
            mats.append(np.where(r >= mid, (u >= mid) & (u <= r), (u > r) & (u < mid)))
        masks.append(((r // (2 * s)) == (u // (2 * s))) & ((r & s) != 0) & ((u & s) == 0))
    masks.append(r == u)
    return (np.concatenate(mats, axis=0).astype(np.float32), np.stack(masks).astype(np.float32))


def _hgrn_head(hd, zr_ref, zf_ref, lb_ref, gain_ref, msum_ref, pmask_ref, o_ref, st_ref):
    C = zf_ref.shape[0]
    levels = _hgrn_levels(C)
    cs = slice(hd * REC_DIM, (hd + 1) * REC_DIM)
    zq = zr_ref[:, cs].astype(F32)
    v = zr_ref[:, REC_WIDTH + hd * REC_DIM:REC_WIDTH + (hd + 1) * REC_DIM]
    og = zr_ref[:, 2 * REC_WIDTH + hd * REC_DIM:2 * REC_WIDTH + (hd + 1) * REC_DIM].astype(F32)
    z = zf_ref[:, cs]
    lb = lb_ref[:, cs]
    qp = zq * _sigmoid(zq)
    a = jnp.exp(-jnp.abs(z))
    r = 1.0 / (1.0 + a)
    pos = z >= 0
    g = jnp.log2(lb + (1.0 - lb) * jnp.where(pos, r, a * r))
    k = (1.0 - lb) * jnp.where(pos, a * r, r)
    g_hi = g.astype(BF16)
    g_lo = (g - g_hi.astype(F32)).astype(BF16)
    x2 = _dot(msum_ref[...], jnp.concatenate([g_hi, g_lo], axis=1))
    sums = x2[:, :REC_DIM] + x2[:, REC_DIM:]
    b = sums[0:C]
    b_last = b[C - 1:C, :]
    e_b = jnp.exp2(b)
    st = st_ref[...]
    o = _dot_nt((qp * e_b).astype(BF16), st.astype(BF16))
    amat = pmask_ref[len(levels)] * _dot_nt(qp.astype(BF16), k.astype(BF16))
    n_matrix_levels = 0
    for li, s in enumerate(levels):
        if s >= HGRN_BROADCAST_MIN:
            ref_rows = [jnp.broadcast_to(b[j + s - 1:j + s, :], (2 * s, REC_DIM)) for j in range(0, C, 2 * s)]
            expo = -jnp.abs(b - jnp.concatenate(ref_rows, axis=0))
        else:
            n_matrix_levels += 1
            expo = sums[n_matrix_levels * C:(n_matrix_levels + 1) * C]
        e_l = jnp.exp2(expo)
        amat = amat + pmask_ref[li] * _dot_nt((qp * e_l).astype(BF16), (k * e_l).astype(BF16))
    o = o + _dot(amat.astype(BF16), v)
    st_ref[...] = st * e_b[C - 1:C, :] + _dot_tn(v, (k * jnp.exp2(b_last - b)).astype(BF16))
    og_act = og * _sigmoid(og)
    o_ref[:, cs] = (_rms(o, gain_ref[...]) * og_act).astype(BF16)


def _hgrn_kernel(zr_ref, zf_ref, lb_ref, gain_ref, msum_ref, pmask_ref, o_ref, st_ref):
    @pl.when(pl.program_id(1) == 0)
    def _():
        st_ref[...] = jnp.zeros_like(st_ref)

    for bb in range(zf_ref.shape[0]):
        for hd in range(REC_HEADS):
            _hgrn_head(hd, zr_ref.at[bb], zf_ref.at[bb], lb_ref, gain_ref, msum_ref, pmask_ref, o_ref.at[bb],
                       st_ref.at[bb * REC_HEADS + hd])


def _hgrn2(zr, zf, lb, gain, msum, pmask, B, S):
    C = min(REC_CHUNK, S)
    nb = HGRN_BATCH_ROWS if B % HGRN_BATCH_ROWS == 0 else 1
    blk = lambda b, c: (b, c, 0)
    out = pl.pallas_call(
        _hgrn_kernel,
        grid=(B // nb, S // C),
        in_specs=[
            pl.BlockSpec((nb, C, 3 * REC_WIDTH), blk),
            pl.BlockSpec((nb, C, REC_WIDTH), blk),
            pl.BlockSpec((1, REC_WIDTH), lambda b, c: (0, 0)),
            pl.BlockSpec((1, REC_DIM), lambda b, c: (0, 0)),
            pl.BlockSpec(msum.shape, lambda b, c: (0, 0)),
            pl.BlockSpec(pmask.shape, lambda b, c: (0, 0, 0)),
        ],
        out_specs=pl.BlockSpec((nb, C, REC_WIDTH), blk),
        out_shape=jax.ShapeDtypeStruct((B, S, REC_WIDTH), BF16),
        scratch_shapes=[pltpu.VMEM((nb * REC_HEADS, REC_DIM, REC_DIM), F32)],
        compiler_params=pltpu.CompilerParams(dimension_semantics=("parallel", "arbitrary")),
        name="hgrn2",
    )(zr.reshape(B, S, 3 * REC_WIDTH), zf.reshape(B, S, REC_WIDTH), lb, gain, msum, pmask)
    return out.reshape(B * S, REC_WIDTH)


def _mix_out_kernel(attn_ref, rec_ref, h_ref, wo_ref, gain_ref, wrh_ref, wrl_ref, tril_ref,
                    h1_ref, xn_ref, ri_ref, rw_ref, cnt_ref, carry_ref):
    @pl.when(pl.program_id(0) == 0)
    def _():
        carry_ref[...] = jnp.zeros_like(carry_ref)

    sub = tril_ref.shape[0]
    carry = carry_ref[...]
    for s in range(h_ref.shape[0] // sub):
        carry = _mix_out_rows(slice(s * sub, (s + 1) * sub), s * sub, carry, attn_ref, rec_ref, h_ref, wo_ref,
                              gain_ref, wrh_ref, wrl_ref, tril_ref, h1_ref, xn_ref, ri_ref, rw_ref)
    carry_ref[...] = carry
    cnt_ref[...] = jnp.broadcast_to(carry, cnt_ref.shape)


def _mix_out_rows(rows, first, carry, attn_ref, rec_ref, h_ref, wo_ref, gain_ref, wrh_ref, wrl_ref, tril_ref,
                  h1_ref, xn_ref, ri_ref, rw_ref):
    mixed = jnp.concatenate([attn_ref[rows, :], rec_ref[rows, :]], axis=1)
    h1 = h_ref[rows, :] + _dot(mixed, wo_ref[...])
    h1_ref[rows, :] = h1
    xn = _rms(h1, gain_ref[...])
    _store_row_tiles(xn_ref, xn, first=first)
    xh = xn.astype(BF16)
    xl = (xn - xh.astype(F32)).astype(BF16)
    logits = _dot(xh, wrh_ref[...]) + _dot(xl, wrh_ref[...]) + _dot(xh, wrl_ref[...])
    lane = lax.broadcasted_iota(I32, logits.shape, 1)
    lanef = lane.astype(F32)
    neg = jnp.float32(-jnp.inf)
    big = jnp.float32(1e9)
    gl = jnp.where(lane < N_GROUPS, logits, neg)
    gmax = jnp.max(gl, axis=-1, keepdims=True)
    gidx = jnp.min(jnp.where(gl == gmax, lanef, big), axis=-1, keepdims=True)
    p_group = 1.0 / jnp.sum(jnp.where(lane < N_GROUPS, jnp.exp(logits - gmax), 0.0), axis=-1, keepdims=True)
    lo = N_GROUPS + gidx * EXPERTS_PER_GROUP
    el = jnp.where((lanef >= lo) & (lanef < lo + EXPERTS_PER_GROUP), logits, neg)
    t1 = jnp.max(el, axis=-1, keepdims=True)
    i1 = jnp.min(jnp.where(el == t1, lanef, big), axis=-1, keepdims=True)
    el2 = jnp.where(lanef == i1, neg, el)
    t2 = jnp.max(el2, axis=-1, keepdims=True)
    i2 = jnp.min(jnp.where(el2 == t2, lanef, big), axis=-1, keepdims=True)
    r21 = jnp.exp(t2 - t1)
    w1 = p_group / (1.0 + r21)
    w2 = w1 * r21
    e1 = i1 - N_GROUPS
    e2 = i2 - N_GROUPS
    oh1 = (lanef == e1).astype(F32)
    oh2 = (lanef == e2).astype(F32)
    oh = oh1 + oh2
    prefix = _dot(tril_ref[...], oh.astype(BF16)) + carry
    rank1 = jnp.sum(prefix * oh1, axis=-1, keepdims=True)
    rank2 = jnp.sum(prefix * oh2, axis=-1, keepdims=True)
    ri = jnp.where(lane == 0, e1, jnp.where(lane == 1, e2, jnp.where(lane == 2, rank1, jnp.where(lane == 3, rank2, 0.0))))
    ri_ref[rows, :] = ri.astype(I32)
    rw_ref[rows, :] = jnp.where(lane == 0, w1, jnp.where(lane == 1, w2, 0.0))
    return carry + jnp.sum(oh, axis=0, keepdims=True)


def _mix_out(attn, rec, h, wo, gain, wrh, wrl, tril):
    T = h.shape[0]
    tm = min(TOKEN_TILE, T)
    row = lambda i: (i, 0)
    fixed = lambda i: (0, 0)
    return pl.pallas_call(
        _mix_out_kernel,
        grid=(T // tm,),
        in_specs=[
            pl.BlockSpec((tm, ATTN_WIDTH), row),
            pl.BlockSpec((tm, REC_WIDTH), row),
            pl.BlockSpec((tm, D_MODEL), row),
            pl.BlockSpec((ATTN_WIDTH + REC_WIDTH, D_MODEL), fixed),
            pl.BlockSpec((1, D_MODEL), fixed),
            pl.BlockSpec((D_MODEL, ROUTE_LANES), fixed),
            pl.BlockSpec((D_MODEL, ROUTE_LANES), fixed),
            pl.BlockSpec(tril.shape, fixed),
        ],
        out_specs=[
            pl.BlockSpec((tm, D_MODEL), row),
            pl.BlockSpec((tm * ROW_SUBLANES, LANES), row),
            pl.BlockSpec((tm, ROUTE_LANES), row),
            pl.BlockSpec((tm, ROUTE_LANES), row),
            pl.BlockSpec((8, ROUTE_LANES), fixed),
        ],
        out_shape=[
            jax.ShapeDtypeStruct((T, D_MODEL), F32),
            jax.ShapeDtypeStruct((T * ROW_SUBLANES, LANES), U32),
            jax.ShapeDtypeStruct((T, ROUTE_LANES), I32),
            jax.ShapeDtypeStruct((T, ROUTE_LANES), F32),
            jax.ShapeDtypeStruct((8, ROUTE_LANES), F32),
        ],
        scratch_shapes=[pltpu.VMEM((1, ROUTE_LANES), F32)],
        compiler_params=pltpu.CompilerParams(dimension_semantics=("arbitrary",)),
        name="mix_out_router",
    )(attn, rec, h, wo, gain, wrh, wrl, tril)


def _row_move_loops(copy, tokens):
    def start(t, carry):
        copy(t, 0).start()
        copy(t, 1).start()
        return carry

    def wait(t, carry):
        copy(t, 0).wait()
        copy(t, 1).wait()
        return carry

    lax.fori_loop(0, tokens, start, 0, unroll=COPY_UNROLL)
    lax.fori_loop(0, tokens, wait, 0, unroll=COPY_UNROLL)


def _dispatch_kernel(dest_ref, bound_ref, x_ref, out_ref, zero_ref, sem, *, tokens):
    @pl.when(pl.program_id(0) == 0)
    def _():
        zero_ref[...] = jnp.zeros_like(zero_ref)

        def fill(e):
            return pltpu.make_async_copy(zero_ref, out_ref.at[pl.ds(bound_ref[e + 1] - EXPERT_BLOCK, EXPERT_BLOCK)], sem)

        for e in range(N_EXPERTS):
            pl.when(bound_ref[e + 1] > bound_ref[e])(lambda e=e: fill(e).start())
        for e in range(N_EXPERTS):
            pl.when(bound_ref[e + 1] > bound_ref[e])(lambda e=e: fill(e).wait())

        def tail(i):
            return pltpu.make_async_copy(zero_ref, out_ref.at[pl.ds(i * EXPERT_BLOCK, EXPERT_BLOCK)], sem)

        first_unused = bound_ref[N_EXPERTS] // EXPERT_BLOCK
        n_blocks = out_ref.shape[0] // EXPERT_BLOCK
        lax.fori_loop(first_unused, n_blocks, lambda i, c: (tail(i).start(), c)[1], 0)
        lax.fori_loop(first_unused, n_blocks, lambda i, c: (tail(i).wait(), c)[1], 0)

    base = 2 * pl.program_id(0) * tokens

    def copy(t, j):
        return pltpu.make_async_copy(x_ref.at[t], out_ref.at[dest_ref[base + 2 * t + j]], sem)

    _row_move_loops(copy, tokens)


def _dispatch(dest, bounds, x3, n_rows):
    tokens = min(COPY_TOKENS, x3.shape[0])
    tile = x3.shape[1:]
    return pl.pallas_call(
        functools.partial(_dispatch_kernel, tokens=tokens),
        grid_spec=pltpu.PrefetchScalarGridSpec(
            num_scalar_prefetch=2,
            grid=(x3.shape[0] // tokens,),
            in_specs=[pl.BlockSpec((tokens,) + tile, lambda i, d, b: (i, 0, 0))],
            out_specs=pl.BlockSpec(memory_space=pl.ANY),
            scratch_shapes=[pltpu.VMEM((EXPERT_BLOCK,) + tile, x3.dtype), pltpu.SemaphoreType.DMA],
        ),
        out_shape=jax.ShapeDtypeStruct((n_rows,) + tile, x3.dtype),
        compiler_params=pltpu.CompilerParams(dimension_semantics=("arbitrary",)),
        name="moe_dispatch_rows",
    )(dest, bounds, x3)


def _expert_kernel(be_ref, sched_ref, x_ref, wg_ref, wu_ref, wd_ref, y_ref,
                   wgf_ref, wuf_ref, wdf_ref, wgb_ref, wub_ref, wdb_ref, sem, *, layer):
    i = pl.program_id(0)
    nblk = pl.num_programs(0)
    used = i < sched_ref[0]
    buf = sched_ref[1 + i]
    next_expert = sched_ref[1 + nblk + i]
    first_of_run = (i == 0) | (be_ref[i] != be_ref[jnp.maximum(i - 1, 0)])

    def fetch(expert, slot):
        e = layer * N_EXPERTS + expert
        return [pltpu.make_async_copy(src.at[e], dst.at[slot], sem.at[slot, n])
                for n, (src, dst) in enumerate(((wg_ref, wgf_ref), (wu_ref, wuf_ref), (wd_ref, wdf_ref)))]

    @pl.when(i == 0)
    def _():
        for c in fetch(be_ref[0], buf):
            c.start()

    @pl.when(used & first_of_run & (next_expert >= 0))
    def _():
        for c in fetch(next_expert, 1 - buf):
            c.start()

    @pl.when(used & first_of_run)
    def _():
        for c in fetch(be_ref[i], buf):
            c.wait()
        wgb_ref[...] = wgf_ref[buf].astype(BF16)
        wub_ref[...] = wuf_ref[buf].astype(BF16)
        wdb_ref[...] = wdf_ref[buf].astype(BF16)

    @pl.when(used)
    def _():
        for s in range(EXPERT_BLOCK // EXPERT_SUB):
            x = _load_row_tiles(x_ref, EXPERT_SUB, first=s * EXPERT_SUB).astype(BF16)
            gate = _dot(x, wgb_ref[...])
            up = _dot(x, wub_ref[...])
            hidden = (gate * _sigmoid(gate) * up).astype(BF16)
            _store_row_tiles(y_ref, _dot(hidden, wdb_ref[...]), first=s * EXPERT_SUB)

    @pl.when(jnp.logical_not(used))
    def _():
        y_ref[...] = jnp.zeros_like(y_ref)


def _experts(block_expert, sched, xs, wg, wu, wd, layer):
    blk = EXPERT_BLOCK * ROW_SUBLANES
    nblk = xs.shape[0] // blk
    hbm = pl.BlockSpec(memory_space=pl.ANY)
    return pl.pallas_call(
        functools.partial(_expert_kernel, layer=layer),
        grid_spec=pltpu.PrefetchScalarGridSpec(
            num_scalar_prefetch=2,
            grid=(nblk,),
            in_specs=[pl.BlockSpec((blk, LANES), lambda i, be, sc: (jnp.minimum(i, sc[0] - 1), 0)), hbm, hbm, hbm],
            out_specs=pl.BlockSpec((blk, LANES), lambda i, be, sc: (i, 0)),
            scratch_shapes=[
                pltpu.VMEM((2, D_MODEL, EXPERT_FF), F32),
                pltpu.VMEM((2, D_MODEL, EXPERT_FF), F32),
                pltpu.VMEM((2, EXPERT_FF, D_MODEL), F32),
                pltpu.VMEM((D_MODEL, EXPERT_FF), BF16),
                pltpu.VMEM((D_MODEL, EXPERT_FF), BF16),
                pltpu.VMEM((EXPERT_FF, D_MODEL), BF16),
                pltpu.SemaphoreType.DMA((2, 3)),
            ],
        ),
        out_shape=jax.ShapeDtypeStruct(xs.shape, xs.dtype),
        compiler_params=pltpu.CompilerParams(dimension_semantics=("arbitrary",)),
        name="moe_experts",
    )(block_expert, sched, xs, wg, wu, wd)


def _ple_kernel(dest_ref, h1_ref, ys_ref, rw_ref, p_ref, wple_ref, pgain_ref, ggain_ref, wpg_ref, o_ref,
                ybuf_ref, sem):
    i = pl.program_id(0)
    tm = h1_ref.shape[0]
    slot = i % 2

    def row_copy(tile, buf, t, j):
        a = 2 * t + j
        dst = ybuf_ref.at[buf, pl.ds(pl.multiple_of(a * ROW_SUBLANES, ROW_SUBLANES), ROW_SUBLANES), :]
        return pltpu.make_async_copy(ys_ref.at[dest_ref[2 * tile * tm + a]], dst, sem.at[buf])

    def start_tile(tile, buf):
        def body(t, carry):
            row_copy(tile, buf, t, 0).start()
            row_copy(tile, buf, t, 1).start()
            return carry
        lax.fori_loop(0, tm, body, 0, unroll=COPY_UNROLL)

    def wait_tile(tile, buf):
        def body(t, carry):
            row_copy(tile, buf, t, 0).wait()
            row_copy(tile, buf, t, 1).wait()
            return carry
        lax.fori_loop(0, tm, body, 0, unroll=COPY_UNROLL)

    pl.when(i == 0)(lambda: start_tile(0, 0))
    pl.when(i + 1 < pl.num_programs(0))(lambda: start_tile(i + 1, 1 - slot))
    wait_tile(i, slot)

    yg_ref = ybuf_ref.at[slot]
    sub = min(SUB_TILE, tm)
    for s in range(tm // sub):
        rows = slice(s * sub, (s + 1) * sub)
        rw = rw_ref[rows, :]
        y1 = _load_row_tiles(yg_ref, sub, first=2 * s * sub, every=2)
        y2 = _load_row_tiles(yg_ref, sub, first=2 * s * sub + 1, every=2)
        h2 = h1_ref[rows, :] + rw[:, 0:1] * y1 + rw[:, 1:2] * y2
        ple = _rms(_dot(p_ref[rows, :].astype(BF16), wple_ref[...]), pgain_ref[...])
        gate = _sigmoid(_dot(_rms(h2, ggain_ref[...]).astype(BF16), wpg_ref[...]))
        o_ref[rows, :] = h2 + ple * gate


def _ple(dest, h1, ys3, rw, p, wple, pgain, ggain, wpg, layer):
    T = h1.shape[0]
    tm = min(TOKEN_TILE, T)
    nt = T // tm
    row = lambda i, d: (i, 0)
    fixed = lambda i, d: (0, 0)
    return pl.pallas_call(
        _ple_kernel,
        grid_spec=pltpu.PrefetchScalarGridSpec(
            num_scalar_prefetch=1,
            grid=(nt,),
            in_specs=[
                pl.BlockSpec((tm, D_MODEL), row),
                pl.BlockSpec(memory_space=pl.ANY),
                pl.BlockSpec((tm, ROUTE_LANES), row),
                pl.BlockSpec((tm, PLE_DIM), lambda i, d: (layer * nt + i, 0)),
                pl.BlockSpec((PLE_DIM, D_MODEL), fixed),
                pl.BlockSpec((1, D_MODEL), fixed),
                pl.BlockSpec((1, D_MODEL), fixed),
                pl.BlockSpec((D_MODEL, D_MODEL), fixed),
            ],
            out_specs=pl.BlockSpec((tm, D_MODEL), row),
            scratch_shapes=[pltpu.VMEM((2, 2 * tm * ROW_SUBLANES, LANES), ys3.dtype), pltpu.SemaphoreType.DMA((2,))],
        ),
        out_shape=jax.ShapeDtypeStruct((T, D_MODEL), F32),
        compiler_params=pltpu.CompilerParams(dimension_semantics=("arbitrary",)),
        name="combine_ple",
    )(dest, h1, ys3, rw, p, wple, pgain, ggain, wpg)


def _rope_tables(positions):
    inv_freq = ROPE_THETA ** (-jnp.arange(0, ROPE_DIM, 2, dtype=F32) / ROPE_DIM)
    ang = positions.astype(F32).reshape(-1, 1) * inv_freq
    cos, sin = jnp.cos(ang), jnp.sin(ang)
    T = ang.shape[0]
    rest = jnp.zeros((T, ATTN_HEAD_DIM - ROPE_DIM), F32)
    zero = jnp.zeros((T, ROPE_HALF), F32)
    rc = jnp.concatenate([cos, cos, rest + 1.0], axis=1)
    rs1 = jnp.concatenate([-sin, zero, rest], axis=1)
    rs2 = jnp.concatenate([zero, sin, rest], axis=1)
    reps = LANES // ATTN_HEAD_DIM
    return tuple(jnp.tile(t, (1, reps)) for t in (rc, rs1, rs2))


def kernel(x, p, positions, mix_norm, w_in, q_norm, k_norm, sinks, lb_logits, rec_norm, w_out, ffn_norm,
           w_router_group, w_router_expert, w_gate, w_up, w_down, w_ple, ple_norm, ple_gate_norm, w_ple_gate):
    B, S, D = x.shape
    depth = w_in.shape[0]
    T = B * S
    n_assign = 2 * T
    assert D == D_MODEL and S % WINDOW == 0 and T % min(TOKEN_TILE, T) == 0

    rc, rs1, rs2 = _rope_tables(positions)
    lb_sm = jax.nn.softmax(lb_logits.astype(F32), axis=0)
    lower_bounds = jnp.cumsum(lb_sm, axis=0) - lb_sm[0:1]

    a0 = QKV_WIDTH
    w_in_p = jnp.concatenate([w_in[:, :, :a0 + REC_WIDTH], w_in[:, :, a0 + 2 * REC_WIDTH:],
                              w_in[:, :, a0 + REC_WIDTH:a0 + 2 * REC_WIDTH]], axis=2).astype(BF16)
    w_out_b = w_out.astype(BF16)
    wg_all = w_gate.reshape(depth * N_EXPERTS, D_MODEL, EXPERT_FF)
    wu_all = w_up.reshape(depth * N_EXPERTS, D_MODEL, EXPERT_FF)
    wd_all = w_down.reshape(depth * N_EXPERTS, EXPERT_FF, D_MODEL)
    w_ple_b = w_ple.astype(BF16)
    w_pg_b = w_ple_gate.astype(BF16)
    w_r = jnp.concatenate([w_router_group, w_router_expert,
                           jnp.zeros((depth, D_MODEL, ROUTE_LANES - N_GROUPS - N_EXPERTS), F32)], axis=2)
    w_r_hi = w_r.astype(BF16)
    w_r_lo = (w_r - w_r_hi.astype(F32)).astype(BF16)
    qk_gain = jnp.concatenate([jnp.tile(q_norm, (1, ATTN_HEADS)) * (ATTN_HEAD_DIM ** -0.5),
                               jnp.tile(k_norm, (1, ATTN_KV_HEADS))], axis=1)
    seg_id = np.arange(QK_WIDTH) // ATTN_HEAD_DIM
    seg = jnp.asarray(seg_id[:, None] == seg_id[None, :], BF16)
    msum_np, pmask_np = _hgrn_constants(min(REC_CHUNK, S))
    msum = jnp.asarray(msum_np, BF16)
    pmask = jnp.asarray(pmask_np, F32)
    sub = min(SUB_TILE, T)
    tril = jnp.asarray(np.tril(np.ones((sub, sub), np.float32), -1), BF16)

    n_rows = n_assign + N_EXPERTS * EXPERT_BLOCK
    nblk = n_rows // EXPERT_BLOCK
    eids = jnp.arange(N_EXPERTS, dtype=I32)
    p2 = p.reshape(depth * T, PLE_DIM)

    h = x.reshape(T, D)
    for l in range(depth):
        za, zr, zf = _mix_in(h, mix_norm[l][None], w_in_p[l], rc, rs1, rs2, qk_gain[l][None], seg)
        attn = _attention(za, sinks[l], B, S)
        rec = _hgrn2(zr, zf, lower_bounds[l][None], rec_norm[l][None], msum, pmask, B, S)
        h1, xn, ri, rw, cnt = _mix_out(attn, rec, h, w_out_b[l], ffn_norm[l][None], w_r_hi[l], w_r_lo[l], tril)
        counts = cnt[0, :N_EXPERTS].astype(I32)
        padded = ((counts + EXPERT_BLOCK - 1) // EXPERT_BLOCK) * EXPERT_BLOCK
        pad_end = jnp.cumsum(padded)
        pad_start = pad_end - padded
        expert = ri[:, 0:2]
        dest = ri[:, 2:4] + jnp.sum(jnp.where(expert[:, :, None] == eids, pad_start, 0), axis=-1)
        dest = dest.reshape(n_assign).astype(I32)
        n_used = (pad_end[-1] // EXPERT_BLOCK).astype(I32).reshape(1)
        blk_start = jnp.arange(nblk, dtype=I32) * EXPERT_BLOCK
        block_expert = jnp.minimum(jnp.sum(pad_end[None, :] <= blk_start[:, None], axis=1), N_EXPERTS - 1).astype(I32)
        bounds = jnp.concatenate([jnp.zeros((1,), I32), pad_end.astype(I32)])
        xs = _dispatch(dest, bounds, xn.reshape(T, ROW_SUBLANES, LANES), n_rows)
        run_start = jnp.concatenate([jnp.ones((1,), I32), (block_expert[1:] != block_expert[:-1]).astype(I32)])
        run_buf = (jnp.cumsum(run_start) - 1) % 2
        later = jnp.where(padded > 0, eids, N_EXPERTS)
        next_used = jnp.concatenate([lax.cummin(later[::-1])[::-1][1:], jnp.full((1,), N_EXPERTS, I32)])
        next_used = jnp.where(next_used < N_EXPERTS, next_used, -1)
        sched = jnp.concatenate([n_used, run_buf, next_used[block_expert]]).astype(I32)
        ys = _experts(block_expert, sched, xs.reshape(n_rows * ROW_SUBLANES, LANES), wg_all, wu_all, wd_all, l)
        h = _ple(dest, h1, ys.reshape(n_rows, ROW_SUBLANES, LANES), rw, p2, w_ple_b[l], ple_norm[l][None],
                 ple_gate_norm[l][None], w_pg_b[l], l)
    return h.reshape(B, S, D)
```

```python
import functools

import jax
import jax.numpy as jnp
import numpy as np
from jax import lax
from jax.experimental import pallas as pl
from jax.experimental.pallas import tpu as pltpu

F32 = jnp.float32
BF16 = jnp.bfloat16
I32 = jnp.int32
U32 = jnp.uint32

D_MODEL = 1024
ATTN_HEADS = 8
ATTN_KV_HEADS = 2
ATTN_HEAD_DIM = 64
ATTN_GROUP = ATTN_HEADS // ATTN_KV_HEADS
ATTN_WIDTH = ATTN_HEADS * ATTN_HEAD_DIM
KV_WIDTH = ATTN_KV_HEADS * ATTN_HEAD_DIM
KV_DUP = 2
KV_COLS = KV_DUP * KV_WIDTH
QK_WIDTH = ATTN_WIDTH + KV_COLS
QKV_WIDTH = ATTN_WIDTH + 2 * KV_COLS
WINDOW = 128
ATTN_STEP_BLOCKS = 4
ROPE_THETA = 500000.0
ROPE_DIM = ATTN_HEAD_DIM // 4
ROPE_HALF = ROPE_DIM // 2
REC_HEADS = 4
REC_DIM = 128
REC_WIDTH = REC_HEADS * REC_DIM
REC_CHUNK = 128
HGRN_BROADCAST_MIN = 8
HGRN_BATCH_ROWS = 4
IN_WIDTH = QKV_WIDTH + 4 * REC_WIDTH
N_GROUPS = 4
EXPERTS_PER_GROUP = 8
N_EXPERTS = N_GROUPS * EXPERTS_PER_GROUP
EXPERT_FF = 512
PLE_DIM = 256
RMS_EPS = 1e-6
MASK_VALUE = -1e30
LANES = 128
ROUTE_LANES = LANES
ROW_WORDS = D_MODEL // 2
ROW_SUBLANES = ROW_WORDS // LANES
HIGH_HALF = np.uint32(0xFFFF0000)

TOKEN_TILE = 512
SUB_TILE = 256
EXPERT_BLOCK = 256
EXPERT_SUB = 128
COPY_TOKENS = 512
COPY_UNROLL = 8


def _dot(a, b):
    return jnp.dot(a, b, preferred_element_type=F32)


def _dot_nt(a, b):
    return lax.dot_general(a, b, (((1,), (1,)), ((), ())), preferred_element_type=F32)


def _dot_tn(a, b):
    return lax.dot_general(a, b, (((0,), (0,)), ((), ())), preferred_element_type=F32)


def _store_row_tiles(ref, x, first=0):
    m = x.shape[0]
    lo = lax.bitcast_convert_type(x[:, :ROW_WORDS].astype(BF16).astype(F32), U32) >> 16
    hi = lax.bitcast_convert_type(x[:, ROW_WORDS:].astype(BF16).astype(F32), U32) & HIGH_HALF
    words = lo | hi
    for c in range(ROW_SUBLANES):
        ref[pl.ds(first * ROW_SUBLANES + c, m, stride=ROW_SUBLANES), :] = words[:, c * LANES:(c + 1) * LANES]


def _load_row_tiles(ref, m, first=0, every=1):
    words = jnp.concatenate(
        [ref[pl.ds(first * ROW_SUBLANES + c, m, stride=every * ROW_SUBLANES), :] for c in range(ROW_SUBLANES)], axis=1)
    lo = lax.bitcast_convert_type(words << 16, F32)
    hi = lax.bitcast_convert_type(words & HIGH_HALF, F32)
    return jnp.concatenate([lo, hi], axis=1)


def _sigmoid(x):
    return 1.0 / (1.0 + jnp.exp(-x))


def _rms(x, gain):
    ms = jnp.mean(x * x, axis=-1, keepdims=True)
    return x * lax.rsqrt(ms + RMS_EPS) * gain


def _mix_in_kernel(h_ref, gain_ref, w_ref, rc_ref, rs1_ref, rs2_ref, qkg_ref, seg_ref,
                   za_ref, zr_ref, zf_ref):
    xn = _rms(h_ref[...], gain_ref[...]).astype(BF16)
    z_a = _dot(xn, w_ref[:, 0:QKV_WIDTH])
    qk = z_a[:, 0:QK_WIDTH]
    seg = _dot((qk * qk).astype(BF16), seg_ref[...]) * (1.0 / ATTN_HEAD_DIM)
    qkn = qk * lax.rsqrt(seg + RMS_EPS) * qkg_ref[...]
    rc, rs1, rs2 = rc_ref[...], rs1_ref[...], rs2_ref[...]
    for c in range(QK_WIDTH // LANES):
        col = qkn[:, c * LANES:(c + 1) * LANES]
        rot = col * rc + pltpu.roll(col, LANES - ROPE_HALF, 1) * rs1 + pltpu.roll(col, ROPE_HALF, 1) * rs2
        za_ref[:, c * LANES:(c + 1) * LANES] = rot.astype(BF16)
    za_ref[:, QK_WIDTH:QKV_WIDTH] = z_a[:, QK_WIDTH:QKV_WIDTH].astype(BF16)
    zr_ref[...] = _dot(xn, w_ref[:, QKV_WIDTH:QKV_WIDTH + 3 * REC_WIDTH]).astype(BF16)
    zf_ref[...] = _dot(xn, w_ref[:, QKV_WIDTH + 3 * REC_WIDTH:IN_WIDTH])


def _mix_in(h, gain, w, rc, rs1, rs2, qkg, seg):
    T = h.shape[0]
    tm = min(TOKEN_TILE, T)
    row = lambda i: (i, 0)
    fixed = lambda i: (0, 0)
    return pl.pallas_call(
        _mix_in_kernel,
        grid=(T // tm,),
        in_specs=[
            pl.BlockSpec((tm, D_MODEL), row),
            pl.BlockSpec((1, D_MODEL), fixed),
            pl.BlockSpec((D_MODEL, IN_WIDTH), fixed),
            pl.BlockSpec((tm, LANES), row),
            pl.BlockSpec((tm, LANES), row),
            pl.BlockSpec((tm, LANES), row),
            pl.BlockSpec((1, QK_WIDTH), fixed),
            pl.BlockSpec((QK_WIDTH, QK_WIDTH), fixed),
        ],
        out_specs=[
            pl.BlockSpec((tm, QKV_WIDTH), row),
            pl.BlockSpec((tm, 3 * REC_WIDTH), row),
            pl.BlockSpec((tm, REC_WIDTH), row),
        ],
        out_shape=[
            jax.ShapeDtypeStruct((T, QKV_WIDTH), BF16),
            jax.ShapeDtypeStruct((T, 3 * REC_WIDTH), BF16),
            jax.ShapeDtypeStruct((T, REC_WIDTH), F32),
        ],
        compiler_params=pltpu.CompilerParams(dimension_semantics=("parallel",)),
        name="mix_in",
    )(h, gain, w, rc, rs1, rs2, qkg, seg)


def _attn_kernel(sink_ref, q_ref, kvc_ref, kvp_ref, o_ref):
    n = pl.program_id(1)
    rows = lax.broadcasted_iota(I32, (ATTN_GROUP * WINDOW, 2 * WINDOW), 0)
    cols = lax.broadcasted_iota(I32, (ATTN_GROUP * WINDOW, 2 * WINDOW), 1)
    qi = rows & (WINDOW - 1)
    in_window = (cols > qi) & (cols <= qi + WINDOW)
    grp = lax.broadcasted_iota(I32, (ATTN_GROUP * WINDOW, 1), 0) // WINDOW
    low_lanes = lax.broadcasted_iota(I32, (1, LANES), 1) < ATTN_HEAD_DIM
    keep_low = jnp.where(low_lanes, 1.0, 0.0).astype(BF16)
    keep_high = jnp.where(low_lanes, 0.0, 1.0).astype(BF16)
    out_low = lax.broadcasted_iota(I32, (WINDOW, LANES), 1) < ATTN_HEAD_DIM
    ones = jnp.ones((2 * WINDOW, LANES), BF16)
    for qb in range(q_ref.shape[0] // WINDOW):
        blk = slice(qb * WINDOW, (qb + 1) * WINDOW)
        kvc = kvc_ref[blk, :]
        kvp = kvp_ref[...] if qb == 0 else kvc_ref[(qb - 1) * WINDOW:qb * WINDOW, :]
        valid = in_window & ((cols >= WINDOW) | (n > 0)) if qb == 0 else in_window
        for j in range(ATTN_KV_HEADS):
            kcols = slice(j * LANES, (j + 1) * LANES)
            vcols = slice(KV_COLS + j * LANES, KV_COLS + (j + 1) * LANES)
            kk = jnp.concatenate([kvp[:, kcols], kvc[:, kcols]], axis=0)
            vv = jnp.concatenate([kvp[:, vcols], kvc[:, vcols]], axis=0)
            pairs = jnp.concatenate([q_ref[blk, (2 * j) * LANES:(2 * j + 1) * LANES],
                                     q_ref[blk, (2 * j + 1) * LANES:(2 * j + 2) * LANES]], axis=0)
            heads = [4 * j, 4 * j + 2, 4 * j + 1, 4 * j + 3]
            s = jnp.concatenate([_dot_nt(pairs, kk * keep_low), _dot_nt(pairs, kk * keep_high)], axis=0)
            s = jnp.where(valid, s, MASK_VALUE)
            sink = jnp.zeros((ATTN_GROUP * WINDOW, 1), F32)
            for g, hh in enumerate(heads):
                sink = jnp.where(grp == g, sink_ref[hh], sink)
            m = jnp.maximum(jnp.max(s, axis=-1, keepdims=True), sink)
            e = jnp.exp(s - m).astype(BF16)
            o2 = _dot(e, jnp.concatenate([vv, ones], axis=1))
            on = o2[:, :LANES] * (1.0 / (o2[:, LANES:] + jnp.exp(sink - m)))
            for c in range(2):
                pair = jnp.where(out_low, on[c * WINDOW:(c + 1) * WINDOW], on[(2 + c) * WINDOW:(3 + c) * WINDOW])
                o_ref[blk, (2 * j + c) * LANES:(2 * j + c + 1) * LANES] = pair.astype(BF16)


def _attention(za, sinks, B, S):
    nq = ATTN_STEP_BLOCKS if (S // WINDOW) % ATTN_STEP_BLOCKS == 0 else 1
    ns = S // (nq * WINDOW)
    kvblk = ATTN_WIDTH // (2 * KV_COLS)
    return pl.pallas_call(
        _attn_kernel,
        grid=(B, ns),
        in_specs=[
            pl.BlockSpec(memory_space=pltpu.SMEM),
            pl.BlockSpec((nq * WINDOW, ATTN_WIDTH), lambda b, n: (b * ns + n, 0)),
            pl.BlockSpec((nq * WINDOW, 2 * KV_COLS), lambda b, n: (b * ns + n, kvblk)),
            pl.BlockSpec((WINDOW, 2 * KV_COLS), lambda b, n: ((b * ns + n) * nq - jnp.minimum(n, 1), kvblk)),
        ],
        out_specs=pl.BlockSpec((nq * WINDOW, ATTN_WIDTH), lambda b, n: (b * ns + n, 0)),
        out_shape=jax.ShapeDtypeStruct((B * S, ATTN_WIDTH), BF16),
        compiler_params=pltpu.CompilerParams(dimension_semantics=("parallel", "arbitrary")),
        name="swa_attention",
    )(sinks, za, za, za)


def _hgrn_levels(C):
    out, s = [], C // 2
    while s >= 1:
        out.append(s)
        s //= 2
    return out


def _hgrn_constants(C):
    r = np.arange(C)[:, None]
    u = np.arange(C)[None, :]
    mats = [u <= r]
    masks = []
    for s in _hgrn_levels(C):
        mid = (r // (2 * s)) * 2 * s + s
        if s < HGRN_BROADCAST_MIN:
            mats.append(np.where(r >= mid, (u >= mid) & (u <= r), (u > r) & (u < mid)))
        masks.append(((r // (2 * s)) == (u // (2 * s))) & ((r & s) != 0) & ((u & s) == 0))
    masks.append(r == u)
    return (np.concatenate(mats, axis=0).astype(np.float32), np.stack(masks).astype(np.float32))


def _hgrn_head(hd, zr_ref, zf_ref, lb_ref, gain_ref, msum_ref, pmask_ref, o_ref, st_ref):
    C = zf_ref.shape[0]
    levels = _hgrn_levels(C)
    cs = slice(hd * REC_DIM, (hd + 1) * REC_DIM)
    zq = zr_ref[:, cs].astype(F32)
    v = zr_ref[:, REC_WIDTH + hd * REC_DIM:REC_WIDTH + (hd + 1) * REC_DIM]
    og = zr_ref[:, 2 * REC_WIDTH + hd * REC_DIM:2 * REC_WIDTH + (hd + 1) * REC_DIM].astype(F32)
    z = zf_ref[:, cs]
    lb = lb_ref[:, cs]
    qp = zq * _sigmoid(zq)
    a = jnp.exp(-jnp.abs(z))
    r = 1.0 / (1.0 + a)
    pos = z >= 0
    g = jnp.log2(lb + (1.0 - lb) * jnp.where(pos, r, a * r))
    k = (1.0 - lb) * jnp.where(pos, a * r, r)
    g_hi = g.astype(BF16)
    g_lo = (g - g_hi.astype(F32)).astype(BF16)
    x2 = _dot(msum_ref[...], jnp.concatenate([g_hi, g_lo], axis=1))
    sums = x2[:, :REC_DIM] + x2[:, REC_DIM:]
    b = sums[0:C]
    b_last = b[C - 1:C, :]
    e_b = jnp.exp2(b)
    st = st_ref[...]
    o = _dot_nt((qp * e_b).astype(BF16), st.astype(BF16))
    amat = pmask_ref[len(levels)] * _dot_nt(qp.astype(BF16), k.astype(BF16))
    n_matrix_levels = 0
    for li, s in enumerate(levels):
        if s >= HGRN_BROADCAST_MIN:
            ref_rows = [jnp.broadcast_to(b[j + s - 1:j + s, :], (2 * s, REC_DIM)) for j in range(0, C, 2 * s)]
            expo = -jnp.abs(b - jnp.concatenate(ref_rows, axis=0))
        else:
            n_matrix_levels += 1
            expo = sums[n_matrix_levels * C:(n_matrix_levels + 1) * C]
        e_l = jnp.exp2(expo)
        amat = amat + pmask_ref[li] * _dot_nt((qp * e_l).astype(BF16), (k * e_l).astype(BF16))
    o = o + _dot(amat.astype(BF16), v)
    st_ref[...] = st * e_b[C - 1:C, :] + _dot_tn(v, (k * jnp.exp2(b_last - b)).astype(BF16))
    og_act = og * _sigmoid(og)
    o_ref[:, cs] = (_rms(o, gain_ref[...]) * og_act).astype(BF16)


def _hgrn_kernel(zr_ref, zf_ref, lb_ref, gain_ref, msum_ref, pmask_ref, o_ref, st_ref):
    @pl.when(pl.program_id(1) == 0)
    def _():
        st_ref[...] = jnp.zeros_like(st_ref)

    for bb in range(zf_ref.shape[0]):
        for hd in range(REC_HEADS):
            _hgrn_head(hd, zr_ref.at[bb], zf_ref.at[bb], lb_ref, gain_ref, msum_ref, pmask_ref, o_ref.at[bb],
                       st_ref.at[bb * REC_HEADS + hd])


def _hgrn2(zr, zf, lb, gain, msum, pmask, B, S):
    C = min(REC_CHUNK, S)
    nb = HGRN_BATCH_ROWS if B % HGRN_BATCH_ROWS == 0 else 1
    blk = lambda b, c: (b, c, 0)
    out = pl.pallas_call(
        _hgrn_kernel,
        grid=(B // nb, S // C),
        in_specs=[
            pl.BlockSpec((nb, C, 3 * REC_WIDTH), blk),
            pl.BlockSpec((nb, C, REC_WIDTH), blk),
            pl.BlockSpec((1, REC_WIDTH), lambda b, c: (0, 0)),
            pl.BlockSpec((1, REC_DIM), lambda b, c: (0, 0)),
            pl.BlockSpec(msum.shape, lambda b, c: (0, 0)),
            pl.BlockSpec(pmask.shape, lambda b, c: (0, 0, 0)),
        ],
        out_specs=pl.BlockSpec((nb, C, REC_WIDTH), blk),
        out_shape=jax.ShapeDtypeStruct((B, S, REC_WIDTH), BF16),
        scratch_shapes=[pltpu.VMEM((nb * REC_HEADS, REC_DIM, REC_DIM), F32)],
        compiler_params=pltpu.CompilerParams(dimension_semantics=("parallel", "arbitrary")),
        name="hgrn2",
    )(zr.reshape(B, S, 3 * REC_WIDTH), zf.reshape(B, S, REC_WIDTH), lb, gain, msum, pmask)
    return out.reshape(B * S, REC_WIDTH)


def _mix_out_kernel(attn_ref, rec_ref, h_ref, wo_ref, gain_ref, wrh_ref, wrl_ref, tril_ref,
                    h1_ref, xn_ref, ri_ref, rw_ref, cnt_ref, carry_ref):
    @pl.when(pl.program_id(0) == 0)
    def _():
        carry_ref[...] = jnp.zeros_like(carry_ref)

    sub = tril_ref.shape[0]
    carry = carry_ref[...]
    for s in range(h_ref.shape[0] // sub):
        carry = _mix_out_rows(slice(s * sub, (s + 1) * sub), s * sub, carry, attn_ref, rec_ref, h_ref, wo_ref,
                              gain_ref, wrh_ref, wrl_ref, tril_ref, h1_ref, xn_ref, ri_ref, rw_ref)
    carry_ref[...] = carry
    cnt_ref[...] = jnp.broadcast_to(carry, cnt_ref.shape)


def _mix_out_rows(rows, first, carry, attn_ref, rec_ref, h_ref, wo_ref, gain_ref, wrh_ref, wrl_ref, tril_ref,
                  h1_ref, xn_ref, ri_ref, rw_ref):
    mixed = jnp.concatenate([attn_ref[rows, :], rec_ref[rows, :]], axis=1)
    h1 = h_ref[rows, :] + _dot(mixed, wo_ref[...])
    h1_ref[rows, :] = h1
    xn = _rms(h1, gain_ref[...])
    _store_row_tiles(xn_ref, xn, first=first)
    xh = xn.astype(BF16)
    xl = (xn - xh.astype(F32)).astype(BF16)
    logits = _dot(xh, wrh_ref[...]) + _dot(xl, wrh_ref[...]) + _dot(xh, wrl_ref[...])
    lane = lax.broadcasted_iota(I32, logits.shape, 1)
    lanef = lane.astype(F32)
    neg = jnp.float32(-jnp.inf)
    big = jnp.float32(1e9)
    gl = jnp.where(lane < N_GROUPS, logits, neg)
    gmax = jnp.max(gl, axis=-1, keepdims=True)
    gidx = jnp.min(jnp.where(gl == gmax, lanef, big), axis=-1, keepdims=True)
    p_group = 1.0 / jnp.sum(jnp.where(lane < N_GROUPS, jnp.exp(logits - gmax), 0.0), axis=-1, keepdims=True)
    lo = N_GROUPS + gidx * EXPERTS_PER_GROUP
    el = jnp.where((lanef >= lo) & (lanef < lo + EXPERTS_PER_GROUP), logits, neg)
    t1 = jnp.max(el, axis=-1, keepdims=True)
    i1 = jnp.min(jnp.where(el == t1, lanef, big), axis=-1, keepdims=True)
    el2 = jnp.where(lanef == i1, neg, el)
    t2 = jnp.max(el2, axis=-1, keepdims=True)
    i2 = jnp.min(jnp.where(el2 == t2, lanef, big), axis=-1, keepdims=True)
    r21 = jnp.exp(t2 - t1)
    w1 = p_group / (1.0 + r21)
    w2 = w1 * r21
    e1 = i1 - N_GROUPS
    e2 = i2 - N_GROUPS
    oh1 = (lanef == e1).astype(F32)
    oh2 = (lanef == e2).astype(F32)
    oh = oh1 + oh2
    prefix = _dot(tril_ref[...], oh.astype(BF16)) + carry
    rank1 = jnp.sum(prefix * oh1, axis=-1, keepdims=True)
    rank2 = jnp.sum(prefix * oh2, axis=-1, keepdims=True)
    ri = jnp.where(lane == 0, e1, jnp.where(lane == 1, e2, jnp.where(lane == 2, rank1, jnp.where(lane == 3, rank2, 0.0))))
    ri_ref[rows, :] = ri.astype(I32)
    rw_ref[rows, :] = jnp.where(lane == 0, w1, jnp.where(lane == 1, w2, 0.0))
    return carry + jnp.sum(oh, axis=0, keepdims=True)


def _mix_out(attn, rec, h, wo, gain, wrh, wrl, tril):
    T = h.shape[0]
    tm = min(TOKEN_TILE, T)
    row = lambda i: (i, 0)
    fixed = lambda i: (0, 0)
    return pl.pallas_call(
        _mix_out_kernel,
        grid=(T // tm,),
        in_specs=[
            pl.BlockSpec((tm, ATTN_WIDTH), row),
            pl.BlockSpec((tm, REC_WIDTH), row),
            pl.BlockSpec((tm, D_MODEL), row),
            pl.BlockSpec((ATTN_WIDTH + REC_WIDTH, D_MODEL), fixed),
            pl.BlockSpec((1, D_MODEL), fixed),
            pl.BlockSpec((D_MODEL, ROUTE_LANES), fixed),
            pl.BlockSpec((D_MODEL, ROUTE_LANES), fixed),
            pl.BlockSpec(tril.shape, fixed),
        ],
        out_specs=[
            pl.BlockSpec((tm, D_MODEL), row),
            pl.BlockSpec((tm * ROW_SUBLANES, LANES), row),
            pl.BlockSpec((tm, ROUTE_LANES), row),
            pl.BlockSpec((tm, ROUTE_LANES), row),
            pl.BlockSpec((8, ROUTE_LANES), fixed),
        ],
        out_shape=[
            jax.ShapeDtypeStruct((T, D_MODEL), F32),
            jax.ShapeDtypeStruct((T * ROW_SUBLANES, LANES), U32),
            jax.ShapeDtypeStruct((T, ROUTE_LANES), I32),
            jax.ShapeDtypeStruct((T, ROUTE_LANES), F32),
            jax.ShapeDtypeStruct((8, ROUTE_LANES), F32),
        ],
        scratch_shapes=[pltpu.VMEM((1, ROUTE_LANES), F32)],
        compiler_params=pltpu.CompilerParams(dimension_semantics=("arbitrary",)),
        name="mix_out_router",
    )(attn, rec, h, wo, gain, wrh, wrl, tril)


def _row_move_loops(copy, tokens):
    def start(t, carry):
        copy(t, 0).start()
        copy(t, 1).start()
        return carry

    def wait(t, carry):
        copy(t, 0).wait()
        copy(t, 1).wait()
        return carry

    lax.fori_loop(0, tokens, start, 0, unroll=COPY_UNROLL)
    lax.fori_loop(0, tokens, wait, 0, unroll=COPY_UNROLL)


def _dispatch_kernel(dest_ref, bound_ref, x_ref, out_ref, zero_ref, sem, *, tokens):
    @pl.when(pl.program_id(0) == 0)
    def _():
        zero_ref[...] = jnp.zeros_like(zero_ref)

        def fill(e):
            return pltpu.make_async_copy(zero_ref, out_ref.at[pl.ds(bound_ref[e + 1] - EXPERT_BLOCK, EXPERT_BLOCK)], sem)

        for e in range(N_EXPERTS):
            pl.when(bound_ref[e + 1] > bound_ref[e])(lambda e=e: fill(e).start())
        for e in range(N_EXPERTS):
            pl.when(bound_ref[e + 1] > bound_ref[e])(lambda e=e: fill(e).wait())

        def tail(i):
            return pltpu.make_async_copy(zero_ref, out_ref.at[pl.ds(i * EXPERT_BLOCK, EXPERT_BLOCK)], sem)

        first_unused = bound_ref[N_EXPERTS] // EXPERT_BLOCK
        n_blocks = out_ref.shape[0] // EXPERT_BLOCK
        lax.fori_loop(first_unused, n_blocks, lambda i, c: (tail(i).start(), c)[1], 0)
        lax.fori_loop(first_unused, n_blocks, lambda i, c: (tail(i).wait(), c)[1], 0)

    base = 2 * pl.program_id(0) * tokens

    def copy(t, j):
        return pltpu.make_async_copy(x_ref.at[t], out_ref.at[dest_ref[base + 2 * t + j]], sem)

    _row_move_loops(copy, tokens)


def _dispatch(dest, bounds, x3, n_rows):
    tokens = min(COPY_TOKENS, x3.shape[0])
    tile = x3.shape[1:]
    return pl.pallas_call(
        functools.partial(_dispatch_kernel, tokens=tokens),
        grid_spec=pltpu.PrefetchScalarGridSpec(
            num_scalar_prefetch=2,
            grid=(x3.shape[0] // tokens,),
            in_specs=[pl.BlockSpec((tokens,) + tile, lambda i, d, b: (i, 0, 0))],
            out_specs=pl.BlockSpec(memory_space=pl.ANY),
            scratch_shapes=[pltpu.VMEM((EXPERT_BLOCK,) + tile, x3.dtype), pltpu.SemaphoreType.DMA],
        ),
        out_shape=jax.ShapeDtypeStruct((n_rows,) + tile, x3.dtype),
        compiler_params=pltpu.CompilerParams(dimension_semantics=("arbitrary",)),
        name="moe_dispatch_rows",
    )(dest, bounds, x3)


def _expert_kernel(be_ref, sched_ref, x_ref, wg_ref, wu_ref, wd_ref, y_ref,
                   wgf_ref, wuf_ref, wdf_ref, wgb_ref, wub_ref, wdb_ref, sem, *, layer):
    i = pl.program_id(0)
    nblk = pl.num_programs(0)
    used = i < sched_ref[0]
    buf = sched_ref[1 + i]
    next_expert = sched_ref[1 + nblk + i]
    first_of_run = (i == 0) | (be_ref[i] != be_ref[jnp.maximum(i - 1, 0)])

    def fetch(expert, slot):
        e = layer * N_EXPERTS + expert
        return [pltpu.make_async_copy(src.at[e], dst.at[slot], sem.at[slot, n])
                for n, (src, dst) in enumerate(((wg_ref, wgf_ref), (wu_ref, wuf_ref), (wd_ref, wdf_ref)))]

    @pl.when(i == 0)
    def _():
        for c in fetch(be_ref[0], buf):
            c.start()

    @pl.when(used & first_of_run & (next_expert >= 0))
    def _():
        for c in fetch(next_expert, 1 - buf):
            c.start()

    @pl.when(used & first_of_run)
    def _():
        for c in fetch(be_ref[i], buf):
            c.wait()
        wgb_ref[...] = wgf_ref[buf].astype(BF16)
        wub_ref[...] = wuf_ref[buf].astype(BF16)
        wdb_ref[...] = wdf_ref[buf].astype(BF16)

    @pl.when(used)
    def _():
        for s in range(EXPERT_BLOCK // EXPERT_SUB):
            x = _load_row_tiles(x_ref, EXPERT_SUB, first=s * EXPERT_SUB).astype(BF16)
            gate = _dot(x, wgb_ref[...])
            up = _dot(x, wub_ref[...])
            hidden = (gate * _sigmoid(gate) * up).astype(BF16)
            _store_row_tiles(y_ref, _dot(hidden, wdb_ref[...]), first=s * EXPERT_SUB)

    @pl.when(jnp.logical_not(used))
    def _():
        y_ref[...] = jnp.zeros_like(y_ref)


def _experts(block_expert, sched, xs, wg, wu, wd, layer):
    blk = EXPERT_BLOCK * ROW_SUBLANES
    nblk = xs.shape[0] // blk
    hbm = pl.BlockSpec(memory_space=pl.ANY)
    return pl.pallas_call(
        functools.partial(_expert_kernel, layer=layer),
        grid_spec=pltpu.PrefetchScalarGridSpec(
            num_scalar_prefetch=2,
            grid=(nblk,),
            in_specs=[pl.BlockSpec((blk, LANES), lambda i, be, sc: (jnp.minimum(i, sc[0] - 1), 0)), hbm, hbm, hbm],
            out_specs=pl.BlockSpec((blk, LANES), lambda i, be, sc: (i, 0)),
            scratch_shapes=[
                pltpu.VMEM((2, D_MODEL, EXPERT_FF), F32),
                pltpu.VMEM((2, D_MODEL, EXPERT_FF), F32),
                pltpu.VMEM((2, EXPERT_FF, D_MODEL), F32),
                pltpu.VMEM((D_MODEL, EXPERT_FF), BF16),
                pltpu.VMEM((D_MODEL, EXPERT_FF), BF16),
                pltpu.VMEM((EXPERT_FF, D_MODEL), BF16),
                pltpu.SemaphoreType.DMA((2, 3)),
            ],
        ),
        out_shape=jax.ShapeDtypeStruct(xs.shape, xs.dtype),
        compiler_params=pltpu.CompilerParams(dimension_semantics=("arbitrary",)),
        name="moe_experts",
    )(block_expert, sched, xs, wg, wu, wd)


def _ple_kernel(dest_ref, h1_ref, ys_ref, rw_ref, p_ref, wple_ref, pgain_ref, ggain_ref, wpg_ref, o_ref,
                ybuf_ref, sem):
    i = pl.program_id(0)
    tm = h1_ref.shape[0]
    slot = i % 2

    def row_copy(tile, buf, t, j):
        a = 2 * t + j
        dst = ybuf_ref.at[buf, pl.ds(pl.multiple_of(a * ROW_SUBLANES, ROW_SUBLANES), ROW_SUBLANES), :]
        return pltpu.make_async_copy(ys_ref.at[dest_ref[2 * tile * tm + a]], dst, sem.at[buf])

    def start_tile(tile, buf):
        def body(t, carry):
            row_copy(tile, buf, t, 0).start()
            row_copy(tile, buf, t, 1).start()
            return carry
        lax.fori_loop(0, tm, body, 0, unroll=COPY_UNROLL)

    def wait_tile(tile, buf):
        def body(t, carry):
            row_copy(tile, buf, t, 0).wait()
            row_copy(tile, buf, t, 1).wait()
            return carry
        lax.fori_loop(0, tm, body, 0, unroll=COPY_UNROLL)

    pl.when(i == 0)(lambda: start_tile(0, 0))
    pl.when(i + 1 < pl.num_programs(0))(lambda: start_tile(i + 1, 1 - slot))
    wait_tile(i, slot)

    yg_ref = ybuf_ref.at[slot]
    sub = min(SUB_TILE, tm)
    for s in range(tm // sub):
        rows = slice(s * sub, (s + 1) * sub)
        rw = rw_ref[rows, :]
        y1 = _load_row_tiles(yg_ref, sub, first=2 * s * sub, every=2)
        y2 = _load_row_tiles(yg_ref, sub, first=2 * s * sub + 1, every=2)
        h2 = h1_ref[rows, :] + rw[:, 0:1] * y1 + rw[:, 1:2] * y2
        ple = _rms(_dot(p_ref[rows, :].astype(BF16), wple_ref[...]), pgain_ref[...])
        gate = _sigmoid(_dot(_rms(h2, ggain_ref[...]).astype(BF16), wpg_ref[...]))
        o_ref[rows, :] = h2 + ple * gate


def _ple(dest, h1, ys3, rw, p, wple, pgain, ggain, wpg, layer):
    T = h1.shape[0]
    tm = min(TOKEN_TILE, T)
    nt = T // tm
    row = lambda i, d: (i, 0)
    fixed = lambda i, d: (0, 0)
    return pl.pallas_call(
        _ple_kernel,
        grid_spec=pltpu.PrefetchScalarGridSpec(
            num_scalar_prefetch=1,
            grid=(nt,),
            in_specs=[
                pl.BlockSpec((tm, D_MODEL), row),
                pl.BlockSpec(memory_space=pl.ANY),
                pl.BlockSpec((tm, ROUTE_LANES), row),
                pl.BlockSpec((tm, PLE_DIM), lambda i, d: (layer * nt + i, 0)),
                pl.BlockSpec((PLE_DIM, D_MODEL), fixed),
                pl.BlockSpec((1, D_MODEL), fixed),
                pl.BlockSpec((1, D_MODEL), fixed),
                pl.BlockSpec((D_MODEL, D_MODEL), fixed),
            ],
            out_specs=pl.BlockSpec((tm, D_MODEL), row),
            scratch_shapes=[pltpu.VMEM((2, 2 * tm * ROW_SUBLANES, LANES), ys3.dtype), pltpu.SemaphoreType.DMA((2,))],
        ),
        out_shape=jax.ShapeDtypeStruct((T, D_MODEL), F32),
        compiler_params=pltpu.CompilerParams(dimension_semantics=("arbitrary",)),
        name="combine_ple",
    )(dest, h1, ys3, rw, p, wple, pgain, ggain, wpg)


def _rope_tables(positions):
    inv_freq = ROPE_THETA ** (-jnp.arange(0, ROPE_DIM, 2, dtype=F32) / ROPE_DIM)
    ang = positions.astype(F32).reshape(-1, 1) * inv_freq
    cos, sin = jnp.cos(ang), jnp.sin(ang)
    T = ang.shape[0]
    rest = jnp.zeros((T, ATTN_HEAD_DIM - ROPE_DIM), F32)
    zero = jnp.zeros((T, ROPE_HALF), F32)
    rc = jnp.concatenate([cos, cos, rest + 1.0], axis=1)
    rs1 = jnp.concatenate([-sin, zero, rest], axis=1)
    rs2 = jnp.concatenate([zero, sin, rest], axis=1)
    reps = LANES // ATTN_HEAD_DIM
    return tuple(jnp.tile(t, (1, reps)) for t in (rc, rs1, rs2))


def kernel(x, p, positions, mix_norm, w_in, q_norm, k_norm, sinks, lb_logits, rec_norm, w_out, ffn_norm,
           w_router_group, w_router_expert, w_gate, w_up, w_down, w_ple, ple_norm, ple_gate_norm, w_ple_gate):
    B, S, D = x.shape
    depth = w_in.shape[0]
    T = B * S
    n_assign = 2 * T
    assert D == D_MODEL and S % WINDOW == 0 and T % min(TOKEN_TILE, T) == 0

    rc, rs1, rs2 = _rope_tables(positions)
    lb_sm = jax.nn.softmax(lb_logits.astype(F32), axis=0)
    lower_bounds = jnp.cumsum(lb_sm, axis=0) - lb_sm[0:1]

    def twice(cols):
        heads = cols.reshape(depth, D_MODEL, ATTN_KV_HEADS, 1, ATTN_HEAD_DIM)
        return jnp.broadcast_to(heads, (depth, D_MODEL, ATTN_KV_HEADS, KV_DUP, ATTN_HEAD_DIM)).reshape(
            depth, D_MODEL, KV_COLS)

    k0, v0, r0 = ATTN_WIDTH, ATTN_WIDTH + KV_WIDTH, ATTN_WIDTH + 2 * KV_WIDTH
    w_in_p = jnp.concatenate([w_in[:, :, :k0], twice(w_in[:, :, k0:v0]), twice(w_in[:, :, v0:r0]),
                              w_in[:, :, r0:r0 + REC_WIDTH], w_in[:, :, r0 + 2 * REC_WIDTH:],
                              w_in[:, :, r0 + REC_WIDTH:r0 + 2 * REC_WIDTH]], axis=2).astype(BF16)
    w_out_b = w_out.astype(BF16)
    wg_all = w_gate.reshape(depth * N_EXPERTS, D_MODEL, EXPERT_FF)
    wu_all = w_up.reshape(depth * N_EXPERTS, D_MODEL, EXPERT_FF)
    wd_all = w_down.reshape(depth * N_EXPERTS, EXPERT_FF, D_MODEL)
    w_ple_b = w_ple.astype(BF16)
    w_pg_b = w_ple_gate.astype(BF16)
    w_r = jnp.concatenate([w_router_group, w_router_expert,
                           jnp.zeros((depth, D_MODEL, ROUTE_LANES - N_GROUPS - N_EXPERTS), F32)], axis=2)
    w_r_hi = w_r.astype(BF16)
    w_r_lo = (w_r - w_r_hi.astype(F32)).astype(BF16)
    qk_gain = jnp.concatenate([jnp.tile(q_norm, (1, ATTN_HEADS)) * (ATTN_HEAD_DIM ** -0.5),
                               jnp.tile(k_norm, (1, ATTN_KV_HEADS * KV_DUP))], axis=1)
    seg_id = np.arange(QK_WIDTH) // ATTN_HEAD_DIM
    seg = jnp.asarray(seg_id[:, None] == seg_id[None, :], BF16)
    msum_np, pmask_np = _hgrn_constants(min(REC_CHUNK, S))
    msum = jnp.asarray(msum_np, BF16)
    pmask = jnp.asarray(pmask_np, F32)
    sub = min(SUB_TILE, T)
    tril = jnp.asarray(np.tril(np.ones((sub, sub), np.float32), -1), BF16)

    n_rows = n_assign + N_EXPERTS * EXPERT_BLOCK
    nblk = n_rows // EXPERT_BLOCK
    eids = jnp.arange(N_EXPERTS, dtype=I32)
    p2 = p.reshape(depth * T, PLE_DIM)

    h = x.reshape(T, D)
    for l in range(depth):
        za, zr, zf = _mix_in(h, mix_norm[l][None], w_in_p[l], rc, rs1, rs2, qk_gain[l][None], seg)
        attn = _attention(za, sinks[l], B, S)
        rec = _hgrn2(zr, zf, lower_bounds[l][None], rec_norm[l][None], msum, pmask, B, S)
        h1, xn, ri, rw, cnt = _mix_out(attn, rec, h, w_out_b[l], ffn_norm[l][None], w_r_hi[l], w_r_lo[l], tril)
        counts = cnt[0, :N_EXPERTS].astype(I32)
        padded = ((counts + EXPERT_BLOCK - 1) // EXPERT_BLOCK) * EXPERT_BLOCK
        pad_end = jnp.cumsum(padded)
        pad_start = pad_end - padded
        expert = ri[:, 0:2]
        dest = ri[:, 2:4] + jnp.sum(jnp.where(expert[:, :, None] == eids, pad_start, 0), axis=-1)
        dest = dest.reshape(n_assign).astype(I32)
        n_used = (pad_end[-1] // EXPERT_BLOCK).astype(I32).reshape(1)
        blk_start = jnp.arange(nblk, dtype=I32) * EXPERT_BLOCK
        block_expert = jnp.minimum(jnp.sum(pad_end[None, :] <= blk_start[:, None], axis=1), N_EXPERTS - 1).astype(I32)
        bounds = jnp.concatenate([jnp.zeros((1,), I32), pad_end.astype(I32)])
        xs = _dispatch(dest, bounds, xn.reshape(T, ROW_SUBLANES, LANES), n_rows)
        run_start = jnp.concatenate([jnp.ones((1,), I32), (block_expert[1:] != block_expert[:-1]).astype(I32)])
        run_buf = (jnp.cumsum(run_start) - 1) % 2
        later = jnp.where(padded > 0, eids, N_EXPERTS)
        next_used = jnp.concatenate([lax.cummin(later[::-1])[::-1][1:], jnp.full((1,), N_EXPERTS, I32)])
        next_used = jnp.where(next_used < N_EXPERTS, next_used, -1)
        sched = jnp.concatenate([n_used, run_buf, next_used[block_expert]]).astype(I32)
        ys = _experts(block_expert, sched, xs.reshape(n_rows * ROW_SUBLANES, LANES), wg_all, wu_all, wd_all, l)
        h = _ple(dest, h1, ys.reshape(n_rows, ROW_SUBLANES, LANES), rw, p2, w_ple_b[l], ple_norm[l][None],
                 ple_gate_norm[l][None], w_pg_b[l], l)
    return h.reshape(B, S, D)
```

```python
import functools

import jax
import jax.numpy as jnp
import numpy as np
from jax import lax
from jax.experimental import pallas as pl
from jax.experimental.pallas import tpu as pltpu

F32 = jnp.float32
BF16 = jnp.bfloat16
I32 = jnp.int32
U32 = jnp.uint32

D_MODEL = 1024
ATTN_HEADS = 8
ATTN_KV_HEADS = 2
ATTN_HEAD_DIM = 64
ATTN_GROUP = ATTN_HEADS // ATTN_KV_HEADS
ATTN_WIDTH = ATTN_HEADS * ATTN_HEAD_DIM
KV_WIDTH = ATTN_KV_HEADS * ATTN_HEAD_DIM
KV_DUP = 2
KV_COLS = KV_DUP * KV_WIDTH
QK_WIDTH = ATTN_WIDTH + KV_COLS
QKV_WIDTH = ATTN_WIDTH + 2 * KV_COLS
WINDOW = 128
ATTN_STEP_BLOCKS = 4
ROPE_THETA = 500000.0
ROPE_DIM = ATTN_HEAD_DIM // 4
ROPE_HALF = ROPE_DIM // 2
REC_HEADS = 4
REC_DIM = 128
REC_WIDTH = REC_HEADS * REC_DIM
REC_CHUNK = 128
HGRN_BROADCAST_MIN = 8
HGRN_BATCH_ROWS = 4
IN_WIDTH = QKV_WIDTH + 4 * REC_WIDTH
N_GROUPS = 4
EXPERTS_PER_GROUP = 8
N_EXPERTS = N_GROUPS * EXPERTS_PER_GROUP
EXPERT_FF = 512
PLE_DIM = 256
RMS_EPS = 1e-6
MASK_VALUE = -1e30
LANES = 128
ROUTE_LANES = LANES
ROW_WORDS = D_MODEL // 2
ROW_SUBLANES = ROW_WORDS // LANES
HIGH_HALF = np.uint32(0xFFFF0000)

TOKEN_TILE = 512
SUB_TILE = 256
EXPERT_BLOCK = 512
EXPERT_SUB = 512
COPY_TOKENS = 512
COPY_UNROLL = 8


def _dot(a, b):
    return jnp.dot(a, b, preferred_element_type=F32)


def _dot_nt(a, b):
    return lax.dot_general(a, b, (((1,), (1,)), ((), ())), preferred_element_type=F32)


def _dot_tn(a, b):
    return lax.dot_general(a, b, (((0,), (0,)), ((), ())), preferred_element_type=F32)


def _store_row_tiles(ref, x, first=0):
    m = x.shape[0]
    lo = lax.bitcast_convert_type(x[:, :ROW_WORDS].astype(BF16).astype(F32), U32) >> 16
    hi = lax.bitcast_convert_type(x[:, ROW_WORDS:].astype(BF16).astype(F32), U32) & HIGH_HALF
    words = lo | hi
    for c in range(ROW_SUBLANES):
        ref[pl.ds(first * ROW_SUBLANES + c, m, stride=ROW_SUBLANES), :] = words[:, c * LANES:(c + 1) * LANES]


def _load_row_tiles(ref, m, first=0, every=1):
    words = jnp.concatenate(
        [ref[pl.ds(first * ROW_SUBLANES + c, m, stride=every * ROW_SUBLANES), :] for c in range(ROW_SUBLANES)], axis=1)
    lo = lax.bitcast_convert_type(words << 16, F32)
    hi = lax.bitcast_convert_type(words & HIGH_HALF, F32)
    return jnp.concatenate([lo, hi], axis=1)


def _sigmoid(x):
    return 1.0 / (1.0 + jnp.exp(-x))


def _rms(x, gain):
    ms = jnp.mean(x * x, axis=-1, keepdims=True)
    return x * lax.rsqrt(ms + RMS_EPS) * gain


def _mix_in_kernel(h_ref, gain_ref, w_ref, rc_ref, rs1_ref, rs2_ref, qkg_ref, seg_ref,
                   za_ref, zr_ref, zf_ref):
    xn = _rms(h_ref[...], gain_ref[...]).astype(BF16)
    z_a = _dot(xn, w_ref[:, 0:QKV_WIDTH])
    qk = z_a[:, 0:QK_WIDTH]
    seg = _dot((qk * qk).astype(BF16), seg_ref[...]) * (1.0 / ATTN_HEAD_DIM)
    qkn = qk * lax.rsqrt(seg + RMS_EPS) * qkg_ref[...]
    rc, rs1, rs2 = rc_ref[...], rs1_ref[...], rs2_ref[...]
    for c in range(QK_WIDTH // LANES):
        col = qkn[:, c * LANES:(c + 1) * LANES]
        rot = col * rc + pltpu.roll(col, LANES - ROPE_HALF, 1) * rs1 + pltpu.roll(col, ROPE_HALF, 1) * rs2
        za_ref[:, c * LANES:(c + 1) * LANES] = rot.astype(BF16)
    za_ref[:, QK_WIDTH:QKV_WIDTH] = z_a[:, QK_WIDTH:QKV_WIDTH].astype(BF16)
    zr_ref[...] = _dot(xn, w_ref[:, QKV_WIDTH:QKV_WIDTH + 3 * REC_WIDTH]).astype(BF16)
    zf_ref[...] = _dot(xn, w_ref[:, QKV_WIDTH + 3 * REC_WIDTH:IN_WIDTH])


def _mix_in(h, gain, w, rc, rs1, rs2, qkg, seg):
    T = h.shape[0]
    tm = min(TOKEN_TILE, T)
    row = lambda i: (i, 0)
    fixed = lambda i: (0, 0)
    return pl.pallas_call(
        _mix_in_kernel,
        grid=(T // tm,),
        in_specs=[
            pl.BlockSpec((tm, D_MODEL), row),
            pl.BlockSpec((1, D_MODEL), fixed),
            pl.BlockSpec((D_MODEL, IN_WIDTH), fixed),
            pl.BlockSpec((tm, LANES), row),
            pl.BlockSpec((tm, LANES), row),
            pl.BlockSpec((tm, LANES), row),
            pl.BlockSpec((1, QK_WIDTH), fixed),
            pl.BlockSpec((QK_WIDTH, QK_WIDTH), fixed),
        ],
        out_specs=[
            pl.BlockSpec((tm, QKV_WIDTH), row),
            pl.BlockSpec((tm, 3 * REC_WIDTH), row),
            pl.BlockSpec((tm, REC_WIDTH), row),
        ],
        out_shape=[
            jax.ShapeDtypeStruct((T, QKV_WIDTH), BF16),
            jax.ShapeDtypeStruct((T, 3 * REC_WIDTH), BF16),
            jax.ShapeDtypeStruct((T, REC_WIDTH), F32),
        ],
        compiler_params=pltpu.CompilerParams(dimension_semantics=("parallel",)),
        name="mix_in",
    )(h, gain, w, rc, rs1, rs2, qkg, seg)


def _attn_kernel(sink_ref, q_ref, kvc_ref, kvp_ref, o_ref):
    n = pl.program_id(1)
    rows = lax.broadcasted_iota(I32, (ATTN_GROUP * WINDOW, 2 * WINDOW), 0)
    cols = lax.broadcasted_iota(I32, (ATTN_GROUP * WINDOW, 2 * WINDOW), 1)
    qi = rows & (WINDOW - 1)
    in_window = (cols > qi) & (cols <= qi + WINDOW)
    grp = lax.broadcasted_iota(I32, (ATTN_GROUP * WINDOW, 1), 0) // WINDOW
    low_lanes = lax.broadcasted_iota(I32, (1, LANES), 1) < ATTN_HEAD_DIM
    keep_low = jnp.where(low_lanes, 1.0, 0.0).astype(BF16)
    keep_high = jnp.where(low_lanes, 0.0, 1.0).astype(BF16)
    out_low = lax.broadcasted_iota(I32, (WINDOW, LANES), 1) < ATTN_HEAD_DIM
    ones = jnp.ones((2 * WINDOW, LANES), BF16)
    for qb in range(q_ref.shape[0] // WINDOW):
        blk = slice(qb * WINDOW, (qb + 1) * WINDOW)
        kvc = kvc_ref[blk, :]
        kvp = kvp_ref[...] if qb == 0 else kvc_ref[(qb - 1) * WINDOW:qb * WINDOW, :]
        valid = in_window & ((cols >= WINDOW) | (n > 0)) if qb == 0 else in_window
        for j in range(ATTN_KV_HEADS):
            kcols = slice(j * LANES, (j + 1) * LANES)
            vcols = slice(KV_COLS + j * LANES, KV_COLS + (j + 1) * LANES)
            kk = jnp.concatenate([kvp[:, kcols], kvc[:, kcols]], axis=0)
            vv = jnp.concatenate([kvp[:, vcols], kvc[:, vcols]], axis=0)
            pairs = jnp.concatenate([q_ref[blk, (2 * j) * LANES:(2 * j + 1) * LANES],
                                     q_ref[blk, (2 * j + 1) * LANES:(2 * j + 2) * LANES]], axis=0)
            heads = [4 * j, 4 * j + 2, 4 * j + 1, 4 * j + 3]
            s = jnp.concatenate([_dot_nt(pairs, kk * keep_low), _dot_nt(pairs, kk * keep_high)], axis=0)
            s = jnp.where(valid, s, MASK_VALUE)
            sink = jnp.zeros((ATTN_GROUP * WINDOW, 1), F32)
            for g, hh in enumerate(heads):
                sink = jnp.where(grp == g, sink_ref[hh], sink)
            m = jnp.maximum(jnp.max(s, axis=-1, keepdims=True), sink)
            e = jnp.exp(s - m).astype(BF16)
            o2 = _dot(e, jnp.concatenate([vv, ones], axis=1))
            on = o2[:, :LANES] * (1.0 / (o2[:, LANES:] + jnp.exp(sink - m)))
            for c in range(2):
                pair = jnp.where(out_low, on[c * WINDOW:(c + 1) * WINDOW], on[(2 + c) * WINDOW:(3 + c) * WINDOW])
                o_ref[blk, (2 * j + c) * LANES:(2 * j + c + 1) * LANES] = pair.astype(BF16)


def _attention(za, sinks, B, S):
    nq = ATTN_STEP_BLOCKS if (S // WINDOW) % ATTN_STEP_BLOCKS == 0 else 1
    ns = S // (nq * WINDOW)
    kvblk = ATTN_WIDTH // (2 * KV_COLS)
    return pl.pallas_call(
        _attn_kernel,
        grid=(B, ns),
        in_specs=[
            pl.BlockSpec(memory_space=pltpu.SMEM),
            pl.BlockSpec((nq * WINDOW, ATTN_WIDTH), lambda b, n: (b * ns + n, 0)),
            pl.BlockSpec((nq * WINDOW, 2 * KV_COLS), lambda b, n: (b * ns + n, kvblk)),
            pl.BlockSpec((WINDOW, 2 * KV_COLS), lambda b, n: ((b * ns + n) * nq - jnp.minimum(n, 1), kvblk)),
        ],
        out_specs=pl.BlockSpec((nq * WINDOW, ATTN_WIDTH), lambda b, n: (b * ns + n, 0)),
        out_shape=jax.ShapeDtypeStruct((B * S, ATTN_WIDTH), BF16),
        compiler_params=pltpu.CompilerParams(dimension_semantics=("parallel", "arbitrary")),
        name="swa_attention",
    )(sinks, za, za, za)


def _hgrn_levels(C):
    out, s = [], C // 2
    while s >= 1:
        out.append(s)
        s //= 2
    return out


def _hgrn_constants(C):
    r = np.arange(C)[:, None]
    u = np.arange(C)[None, :]
    mats = [u <= r]
    masks = []
    for s in _hgrn_levels(C):
        mid = (r // (2 * s)) * 2 * s + s
        if s < HGRN_BROADCAST_MIN:
            mats.append(np.where(r >= mid, (u >= mid) & (u <= r), (u > r) & (u < mid)))
        masks.append(((r // (2 * s)) == (u // (2 * s))) & ((r & s) != 0) & ((u & s) == 0))
    masks.append(r == u)
    return (np.concatenate(mats, axis=0).astype(np.float32), np.stack(masks).astype(np.float32))


def _hgrn_head(hd, zr_ref, zf_ref, lb_ref, gain_ref, msum_ref, pmask_ref, o_ref, st_ref):
    C = zf_ref.shape[0]
    levels = _hgrn_levels(C)
    cs = slice(hd * REC_DIM, (hd + 1) * REC_DIM)
    zq = zr_ref[:, cs].astype(F32)
    v = zr_ref[:, REC_WIDTH + hd * REC_DIM:REC_WIDTH + (hd + 1) * REC_DIM]
    og = zr_ref[:, 2 * REC_WIDTH + hd * REC_DIM:2 * REC_WIDTH + (hd + 1) * REC_DIM].astype(F32)
    z = zf_ref[:, cs]
    lb = lb_ref[:, cs]
    qp = zq * _sigmoid(zq)
    a = jnp.exp(-jnp.abs(z))
    r = 1.0 / (1.0 + a)
    pos = z >= 0
    g = jnp.log2(lb + (1.0 - lb) * jnp.where(pos, r, a * r))
    k = (1.0 - lb) * jnp.where(pos, a * r, r)
    g_hi = g.astype(BF16)
    g_lo = (g - g_hi.astype(F32)).astype(BF16)
    x2 = _dot(msum_ref[...], jnp.concatenate([g_hi, g_lo], axis=1))
    sums = x2[:, :REC_DIM] + x2[:, REC_DIM:]
    b = sums[0:C]
    b_last = b[C - 1:C, :]
    e_b = jnp.exp2(b)
    st = st_ref[...]
    o = _dot_nt((qp * e_b).astype(BF16), st.astype(BF16))
    amat = pmask_ref[len(levels)] * _dot_nt(qp.astype(BF16), k.astype(BF16))
    n_matrix_levels = 0
    for li, s in enumerate(levels):
        if s >= HGRN_BROADCAST_MIN:
            ref_rows = [jnp.broadcast_to(b[j + s - 1:j + s, :], (2 * s, REC_DIM)) for j in range(0, C, 2 * s)]
            expo = -jnp.abs(b - jnp.concatenate(ref_rows, axis=0))
        else:
            n_matrix_levels += 1
            expo = sums[n_matrix_levels * C:(n_matrix_levels + 1) * C]
        e_l = jnp.exp2(expo)
        amat = amat + pmask_ref[li] * _dot_nt((qp * e_l).astype(BF16), (k * e_l).astype(BF16))
    o = o + _dot(amat.astype(BF16), v)
    st_ref[...] = st * e_b[C - 1:C, :] + _dot_tn(v, (k * jnp.exp2(b_last - b)).astype(BF16))
    og_act = og * _sigmoid(og)
    o_ref[:, cs] = (_rms(o, gain_ref[...]) * og_act).astype(BF16)


def _hgrn_kernel(zr_ref, zf_ref, lb_ref, gain_ref, msum_ref, pmask_ref, o_ref, st_ref):
    @pl.when(pl.program_id(1) == 0)
    def _():
        st_ref[...] = jnp.zeros_like(st_ref)

    for bb in range(zf_ref.shape[0]):
        for hd in range(REC_HEADS):
            _hgrn_head(hd, zr_ref.at[bb], zf_ref.at[bb], lb_ref, gain_ref, msum_ref, pmask_ref, o_ref.at[bb],
                       st_ref.at[bb * REC_HEADS + hd])


def _hgrn2(zr, zf, lb, gain, msum, pmask, B, S):
    C = min(REC_CHUNK, S)
    nb = HGRN_BATCH_ROWS if B % HGRN_BATCH_ROWS == 0 else 1
    blk = lambda b, c: (b, c, 0)
    out = pl.pallas_call(
        _hgrn_kernel,
        grid=(B // nb, S // C),
        in_specs=[
            pl.BlockSpec((nb, C, 3 * REC_WIDTH), blk),
            pl.BlockSpec((nb, C, REC_WIDTH), blk),
            pl.BlockSpec((1, REC_WIDTH), lambda b, c: (0, 0)),
            pl.BlockSpec((1, REC_DIM), lambda b, c: (0, 0)),
            pl.BlockSpec(msum.shape, lambda b, c: (0, 0)),
            pl.BlockSpec(pmask.shape, lambda b, c: (0, 0, 0)),
        ],
        out_specs=pl.BlockSpec((nb, C, REC_WIDTH), blk),
        out_shape=jax.ShapeDtypeStruct((B, S, REC_WIDTH), BF16),
        scratch_shapes=[pltpu.VMEM((nb * REC_HEADS, REC_DIM, REC_DIM), F32)],
        compiler_params=pltpu.CompilerParams(dimension_semantics=("parallel", "arbitrary")),
        name="hgrn2",
    )(zr.reshape(B, S, 3 * REC_WIDTH), zf.reshape(B, S, REC_WIDTH), lb, gain, msum, pmask)
    return out.reshape(B * S, REC_WIDTH)


def _mix_out_kernel(attn_ref, rec_ref, h_ref, wo_ref, gain_ref, wrh_ref, wrl_ref, tril_ref,
                    h1_ref, xn_ref, ri_ref, rw_ref, cnt_ref, carry_ref):
    @pl.when(pl.program_id(0) == 0)
    def _():
        carry_ref[...] = jnp.zeros_like(carry_ref)

    sub = tril_ref.shape[0]
    carry = carry_ref[...]
    for s in range(h_ref.shape[0] // sub):
        carry = _mix_out_rows(slice(s * sub, (s + 1) * sub), s * sub, carry, attn_ref, rec_ref, h_ref, wo_ref,
                              gain_ref, wrh_ref, wrl_ref, tril_ref, h1_ref, xn_ref, ri_ref, rw_ref)
    carry_ref[...] = carry
    cnt_ref[...] = jnp.broadcast_to(carry, cnt_ref.shape)


def _mix_out_rows(rows, first, carry, attn_ref, rec_ref, h_ref, wo_ref, gain_ref, wrh_ref, wrl_ref, tril_ref,
                  h1_ref, xn_ref, ri_ref, rw_ref):
    mixed = jnp.concatenate([attn_ref[rows, :], rec_ref[rows, :]], axis=1)
    h1 = h_ref[rows, :] + _dot(mixed, wo_ref[...])
    h1_ref[rows, :] = h1
    xn = _rms(h1, gain_ref[...])
    _store_row_tiles(xn_ref, xn, first=first)
    xh = xn.astype(BF16)
    xl = (xn - xh.astype(F32)).astype(BF16)
    logits = _dot(xh, wrh_ref[...]) + _dot(xl, wrh_ref[...]) + _dot(xh, wrl_ref[...])
    lane = lax.broadcasted_iota(I32, logits.shape, 1)
    lanef = lane.astype(F32)
    neg = jnp.float32(-jnp.inf)
    big = jnp.float32(1e9)
    gl = jnp.where(lane < N_GROUPS, logits, neg)
    gmax = jnp.max(gl, axis=-1, keepdims=True)
    gidx = jnp.min(jnp.where(gl == gmax, lanef, big), axis=-1, keepdims=True)
    p_group = 1.0 / jnp.sum(jnp.where(lane < N_GROUPS, jnp.exp(logits - gmax), 0.0), axis=-1, keepdims=True)
    lo = N_GROUPS + gidx * EXPERTS_PER_GROUP
    el = jnp.where((lanef >= lo) & (lanef < lo + EXPERTS_PER_GROUP), logits, neg)
    t1 = jnp.max(el, axis=-1, keepdims=True)
    i1 = jnp.min(jnp.where(el == t1, lanef, big), axis=-1, keepdims=True)
    el2 = jnp.where(lanef == i1, neg, el)
    t2 = jnp.max(el2, axis=-1, keepdims=True)
    i2 = jnp.min(jnp.where(el2 == t2, lanef, big), axis=-1, keepdims=True)
    r21 = jnp.exp(t2 - t1)
    w1 = p_group / (1.0 + r21)
    w2 = w1 * r21
    e1 = i1 - N_GROUPS
    e2 = i2 - N_GROUPS
    oh1 = (lanef == e1).astype(F32)
    oh2 = (lanef == e2).astype(F32)
    oh = oh1 + oh2
    prefix = _dot(tril_ref[...], oh.astype(BF16)) + carry
    rank1 = jnp.sum(prefix * oh1, axis=-1, keepdims=True)
    rank2 = jnp.sum(prefix * oh2, axis=-1, keepdims=True)
    ri = jnp.where(lane == 0, e1, jnp.where(lane == 1, e2, jnp.where(lane == 2, rank1, jnp.where(lane == 3, rank2, 0.0))))
    ri_ref[rows, :] = ri.astype(I32)
    rw_ref[rows, :] = jnp.where(lane == 0, w1, jnp.where(lane == 1, w2, 0.0))
    return carry + jnp.sum(oh, axis=0, keepdims=True)


def _mix_out(attn, rec, h, wo, gain, wrh, wrl, tril):
    T = h.shape[0]
    tm = min(TOKEN_TILE, T)
    row = lambda i: (i, 0)
    fixed = lambda i: (0, 0)
    return pl.pallas_call(
        _mix_out_kernel,
        grid=(T // tm,),
        in_specs=[
            pl.BlockSpec((tm, ATTN_WIDTH), row),
            pl.BlockSpec((tm, REC_WIDTH), row),
            pl.BlockSpec((tm, D_MODEL), row),
            pl.BlockSpec((ATTN_WIDTH + REC_WIDTH, D_MODEL), fixed),
            pl.BlockSpec((1, D_MODEL), fixed),
            pl.BlockSpec((D_MODEL, ROUTE_LANES), fixed),
            pl.BlockSpec((D_MODEL, ROUTE_LANES), fixed),
            pl.BlockSpec(tril.shape, fixed),
        ],
        out_specs=[
            pl.BlockSpec((tm, D_MODEL), row),
            pl.BlockSpec((tm * ROW_SUBLANES, LANES), row),
            pl.BlockSpec((tm, ROUTE_LANES), row),
            pl.BlockSpec((tm, ROUTE_LANES), row),
            pl.BlockSpec((8, ROUTE_LANES), fixed),
        ],
        out_shape=[
            jax.ShapeDtypeStruct((T, D_MODEL), F32),
            jax.ShapeDtypeStruct((T * ROW_SUBLANES, LANES), U32),
            jax.ShapeDtypeStruct((T, ROUTE_LANES), I32),
            jax.ShapeDtypeStruct((T, ROUTE_LANES), F32),
            jax.ShapeDtypeStruct((8, ROUTE_LANES), F32),
        ],
        scratch_shapes=[pltpu.VMEM((1, ROUTE_LANES), F32)],
        compiler_params=pltpu.CompilerParams(dimension_semantics=("arbitrary",)),
        name="mix_out_router",
    )(attn, rec, h, wo, gain, wrh, wrl, tril)


def _row_move_loops(copy, tokens):
    def start(t, carry):
        copy(t, 0).start()
        copy(t, 1).start()
        return carry

    def wait(t, carry):
        copy(t, 0).wait()
        copy(t, 1).wait()
        return carry

    lax.fori_loop(0, tokens, start, 0, unroll=COPY_UNROLL)
    lax.fori_loop(0, tokens, wait, 0, unroll=COPY_UNROLL)


def _dispatch_kernel(dest_ref, bound_ref, x_ref, out_ref, zero_ref, sem, *, tokens):
    @pl.when(pl.program_id(0) == 0)
    def _():
        zero_ref[...] = jnp.zeros_like(zero_ref)

        def fill(e):
            return pltpu.make_async_copy(zero_ref, out_ref.at[pl.ds(bound_ref[e + 1] - EXPERT_BLOCK, EXPERT_BLOCK)], sem)

        for e in range(N_EXPERTS):
            pl.when(bound_ref[e + 1] > bound_ref[e])(lambda e=e: fill(e).start())
        for e in range(N_EXPERTS):
            pl.when(bound_ref[e + 1] > bound_ref[e])(lambda e=e: fill(e).wait())

        def tail(i):
            return pltpu.make_async_copy(zero_ref, out_ref.at[pl.ds(i * EXPERT_BLOCK, EXPERT_BLOCK)], sem)

        first_unused = bound_ref[N_EXPERTS] // EXPERT_BLOCK
        n_blocks = out_ref.shape[0] // EXPERT_BLOCK
        lax.fori_loop(first_unused, n_blocks, lambda i, c: (tail(i).start(), c)[1], 0)
        lax.fori_loop(first_unused, n_blocks, lambda i, c: (tail(i).wait(), c)[1], 0)

    base = 2 * pl.program_id(0) * tokens

    def copy(t, j):
        return pltpu.make_async_copy(x_ref.at[t], out_ref.at[dest_ref[base + 2 * t + j]], sem)

    _row_move_loops(copy, tokens)


def _dispatch(dest, bounds, x3, n_rows):
    tokens = min(COPY_TOKENS, x3.shape[0])
    tile = x3.shape[1:]
    return pl.pallas_call(
        functools.partial(_dispatch_kernel, tokens=tokens),
        grid_spec=pltpu.PrefetchScalarGridSpec(
            num_scalar_prefetch=2,
            grid=(x3.shape[0] // tokens,),
            in_specs=[pl.BlockSpec((tokens,) + tile, lambda i, d, b: (i, 0, 0))],
            out_specs=pl.BlockSpec(memory_space=pl.ANY),
            scratch_shapes=[pltpu.VMEM((EXPERT_BLOCK,) + tile, x3.dtype), pltpu.SemaphoreType.DMA],
        ),
        out_shape=jax.ShapeDtypeStruct((n_rows,) + tile, x3.dtype),
        compiler_params=pltpu.CompilerParams(dimension_semantics=("arbitrary",)),
        name="moe_dispatch_rows",
    )(dest, bounds, x3)


def _expert_kernel(be_ref, sched_ref, x_ref, wg_ref, wu_ref, wd_ref, y_ref,
                   wgf_ref, wuf_ref, wdf_ref, wgb_ref, wub_ref, wdb_ref, sem, *, layer):
    i = pl.program_id(0)
    nblk = pl.num_programs(0)
    used = i < sched_ref[0]
    buf = sched_ref[1 + i]
    next_expert = sched_ref[1 + nblk + i]
    first_of_run = (i == 0) | (be_ref[i] != be_ref[jnp.maximum(i - 1, 0)])

    def fetch(expert, slot):
        e = layer * N_EXPERTS + expert
        return [pltpu.make_async_copy(src.at[e], dst.at[slot], sem.at[slot, n])
                for n, (src, dst) in enumerate(((wg_ref, wgf_ref), (wu_ref, wuf_ref), (wd_ref, wdf_ref)))]

    @pl.when(i == 0)
    def _():
        for c in fetch(be_ref[0], buf):
            c.start()

    @pl.when(used & first_of_run & (next_expert >= 0))
    def _():
        for c in fetch(next_expert, 1 - buf):
            c.start()

    @pl.when(used & first_of_run)
    def _():
        for c in fetch(be_ref[i], buf):
            c.wait()
        wgb_ref[...] = wgf_ref[buf].astype(BF16)
        wub_ref[...] = wuf_ref[buf].astype(BF16)
        wdb_ref[...] = wdf_ref[buf].astype(BF16)

    @pl.when(used)
    def _():
        for s in range(EXPERT_BLOCK // EXPERT_SUB):
            x = _load_row_tiles(x_ref, EXPERT_SUB, first=s * EXPERT_SUB).astype(BF16)
            gate = _dot(x, wgb_ref[...])
            up = _dot(x, wub_ref[...])
            hidden = (gate * _sigmoid(gate) * up).astype(BF16)
            _store_row_tiles(y_ref, _dot(hidden, wdb_ref[...]), first=s * EXPERT_SUB)

    @pl.when(jnp.logical_not(used))
    def _():
        y_ref[...] = jnp.zeros_like(y_ref)


def _experts(block_expert, sched, xs, wg, wu, wd, layer):
    blk = EXPERT_BLOCK * ROW_SUBLANES
    nblk = xs.shape[0] // blk
    hbm = pl.BlockSpec(memory_space=pl.ANY)
    return pl.pallas_call(
        functools.partial(_expert_kernel, layer=layer),
        grid_spec=pltpu.PrefetchScalarGridSpec(
            num_scalar_prefetch=2,
            grid=(nblk,),
            in_specs=[pl.BlockSpec((blk, LANES), lambda i, be, sc: (jnp.minimum(i, sc[0] - 1), 0)), hbm, hbm, hbm],
            out_specs=pl.BlockSpec((blk, LANES), lambda i, be, sc: (i, 0)),
            scratch_shapes=[
                pltpu.VMEM((2, D_MODEL, EXPERT_FF), F32),
                pltpu.VMEM((2, D_MODEL, EXPERT_FF), F32),
                pltpu.VMEM((2, EXPERT_FF, D_MODEL), F32),
                pltpu.VMEM((D_MODEL, EXPERT_FF), BF16),
                pltpu.VMEM((D_MODEL, EXPERT_FF), BF16),
                pltpu.VMEM((EXPERT_FF, D_MODEL), BF16),
                pltpu.SemaphoreType.DMA((2, 3)),
            ],
        ),
        out_shape=jax.ShapeDtypeStruct(xs.shape, xs.dtype),
        compiler_params=pltpu.CompilerParams(dimension_semantics=("arbitrary",)),
        name="moe_experts",
    )(block_expert, sched, xs, wg, wu, wd)


def _ple_kernel(dest_ref, h1_ref, ys_ref, rw_ref, p_ref, wple_ref, pgain_ref, ggain_ref, wpg_ref, o_ref,
                ybuf_ref, sem):
    i = pl.program_id(0)
    tm = h1_ref.shape[0]
    slot = i % 2

    def row_copy(tile, buf, t, j):
        a = 2 * t + j
        dst = ybuf_ref.at[buf, pl.ds(pl.multiple_of(a * ROW_SUBLANES, ROW_SUBLANES), ROW_SUBLANES), :]
        return pltpu.make_async_copy(ys_ref.at[dest_ref[2 * tile * tm + a]], dst, sem.at[buf])

    def start_tile(tile, buf):
        def body(t, carry):
            row_copy(tile, buf, t, 0).start()
            row_copy(tile, buf, t, 1).start()
            return carry
        lax.fori_loop(0, tm, body, 0, unroll=COPY_UNROLL)

    def wait_tile(tile, buf):
        def body(t, carry):
            row_copy(tile, buf, t, 0).wait()
            row_copy(tile, buf, t, 1).wait()
            return carry
        lax.fori_loop(0, tm, body, 0, unroll=COPY_UNROLL)

    n_tiles = pl.num_programs(0)
    pl.when(i == 0)(lambda: start_tile(0, 0))
    wait_tile(i, slot)
    nxt = jnp.minimum(i + 1, n_tiles - 1)

    yg_ref = ybuf_ref.at[slot]
    sub = min(SUB_TILE, tm)
    for s in range(tm // sub):
        rows = slice(s * sub, (s + 1) * sub)
        rw = rw_ref[rows, :]
        y1 = _load_row_tiles(yg_ref, sub, first=2 * s * sub, every=2)
        y2 = _load_row_tiles(yg_ref, sub, first=2 * s * sub + 1, every=2)
        h2 = h1_ref[rows, :] + rw[:, 0:1] * y1 + rw[:, 1:2] * y2
        ple = _rms(_dot(p_ref[rows, :].astype(BF16), wple_ref[...]), pgain_ref[...])
        gate = _sigmoid(_dot(_rms(h2, ggain_ref[...]).astype(BF16), wpg_ref[...]))
        o_ref[rows, :] = h2 + ple * gate
        for t in range(s * sub, (s + 1) * sub):
            row_copy(nxt, 1 - slot, t, 0).start()
            row_copy(nxt, 1 - slot, t, 1).start()

    pl.when(i == n_tiles - 1)(lambda: wait_tile(i, 1 - slot))


def _ple(dest, h1, ys3, rw, p, wple, pgain, ggain, wpg, layer):
    T = h1.shape[0]
    tm = min(TOKEN_TILE, T)
    nt = T // tm
    row = lambda i, d: (i, 0)
    fixed = lambda i, d: (0, 0)
    return pl.pallas_call(
        _ple_kernel,
        grid_spec=pltpu.PrefetchScalarGridSpec(
            num_scalar_prefetch=1,
            grid=(nt,),
            in_specs=[
                pl.BlockSpec((tm, D_MODEL), row),
                pl.BlockSpec(memory_space=pl.ANY),
                pl.BlockSpec((tm, ROUTE_LANES), row),
                pl.BlockSpec((tm, PLE_DIM), lambda i, d: (layer * nt + i, 0)),
                pl.BlockSpec((PLE_DIM, D_MODEL), fixed),
                pl.BlockSpec((1, D_MODEL), fixed),
                pl.BlockSpec((1, D_MODEL), fixed),
                pl.BlockSpec((D_MODEL, D_MODEL), fixed),
            ],
            out_specs=pl.BlockSpec((tm, D_MODEL), row),
            scratch_shapes=[pltpu.VMEM((2, 2 * tm * ROW_SUBLANES, LANES), ys3.dtype), pltpu.SemaphoreType.DMA((2,))],
        ),
        out_shape=jax.ShapeDtypeStruct((T, D_MODEL), F32),
        compiler_params=pltpu.CompilerParams(dimension_semantics=("arbitrary",)),
        name="combine_ple",
    )(dest, h1, ys3, rw, p, wple, pgain, ggain, wpg)


def _rope_tables(positions):
    inv_freq = ROPE_THETA ** (-jnp.arange(0, ROPE_DIM, 2, dtype=F32) / ROPE_DIM)
    ang = positions.astype(F32).reshape(-1, 1) * inv_freq
    cos, sin = jnp.cos(ang), jnp.sin(ang)
    T = ang.shape[0]
    rest = jnp.zeros((T, ATTN_HEAD_DIM - ROPE_DIM), F32)
    zero = jnp.zeros((T, ROPE_HALF), F32)
    rc = jnp.concatenate([cos, cos, rest + 1.0], axis=1)
    rs1 = jnp.concatenate([-sin, zero, rest], axis=1)
    rs2 = jnp.concatenate([zero, sin, rest], axis=1)
    reps = LANES // ATTN_HEAD_DIM
    return tuple(jnp.tile(t, (1, reps)) for t in (rc, rs1, rs2))


def kernel(x, p, positions, mix_norm, w_in, q_norm, k_norm, sinks, lb_logits, rec_norm, w_out, ffn_norm,
           w_router_group, w_router_expert, w_gate, w_up, w_down, w_ple, ple_norm, ple_gate_norm, w_ple_gate):
    B, S, D = x.shape
    depth = w_in.shape[0]
    T = B * S
    n_assign = 2 * T
    assert D == D_MODEL and S % WINDOW == 0 and T % min(TOKEN_TILE, T) == 0

    rc, rs1, rs2 = _rope_tables(positions)
    lb_sm = jax.nn.softmax(lb_logits.astype(F32), axis=0)
    lower_bounds = jnp.cumsum(lb_sm, axis=0) - lb_sm[0:1]

    def twice(cols):
        heads = cols.reshape(depth, D_MODEL, ATTN_KV_HEADS, 1, ATTN_HEAD_DIM)
        return jnp.broadcast_to(heads, (depth, D_MODEL, ATTN_KV_HEADS, KV_DUP, ATTN_HEAD_DIM)).reshape(
            depth, D_MODEL, KV_COLS)

    k0, v0, r0 = ATTN_WIDTH, ATTN_WIDTH + KV_WIDTH, ATTN_WIDTH + 2 * KV_WIDTH
    w_in_p = jnp.concatenate([w_in[:, :, :k0], twice(w_in[:, :, k0:v0]), twice(w_in[:, :, v0:r0]),
                              w_in[:, :, r0:r0 + REC_WIDTH], w_in[:, :, r0 + 2 * REC_WIDTH:],
                              w_in[:, :, r0 + REC_WIDTH:r0 + 2 * REC_WIDTH]], axis=2).astype(BF16)
    w_out_b = w_out.astype(BF16)
    wg_all = w_gate.reshape(depth * N_EXPERTS, D_MODEL, EXPERT_FF)
    wu_all = w_up.reshape(depth * N_EXPERTS, D_MODEL, EXPERT_FF)
    wd_all = w_down.reshape(depth * N_EXPERTS, EXPERT_FF, D_MODEL)
    w_ple_b = w_ple.astype(BF16)
    w_pg_b = w_ple_gate.astype(BF16)
    w_r = jnp.concatenate([w_router_group, w_router_expert,
                           jnp.zeros((depth, D_MODEL, ROUTE_LANES - N_GROUPS - N_EXPERTS), F32)], axis=2)
    w_r_hi = w_r.astype(BF16)
    w_r_lo = (w_r - w_r_hi.astype(F32)).astype(BF16)
    qk_gain = jnp.concatenate([jnp.tile(q_norm, (1, ATTN_HEADS)) * (ATTN_HEAD_DIM ** -0.5),
                               jnp.tile(k_norm, (1, ATTN_KV_HEADS * KV_DUP))], axis=1)
    seg_id = np.arange(QK_WIDTH) // ATTN_HEAD_DIM
    seg = jnp.asarray(seg_id[:, None] == seg_id[None, :], BF16)
    msum_np, pmask_np = _hgrn_constants(min(REC_CHUNK, S))
    msum = jnp.asarray(msum_np, BF16)
    pmask = jnp.asarray(pmask_np, F32)
    sub = min(SUB_TILE, T)
    tril = jnp.asarray(np.tril(np.ones((sub, sub), np.float32), -1), BF16)

    n_rows = n_assign + N_EXPERTS * EXPERT_BLOCK
    nblk = n_rows // EXPERT_BLOCK
    eids = jnp.arange(N_EXPERTS, dtype=I32)
    p2 = p.reshape(depth * T, PLE_DIM)

    h = x.reshape(T, D)
    for l in range(depth):
        za, zr, zf = _mix_in(h, mix_norm[l][None], w_in_p[l], rc, rs1, rs2, qk_gain[l][None], seg)
        attn = _attention(za, sinks[l], B, S)
        rec = _hgrn2(zr, zf, lower_bounds[l][None], rec_norm[l][None], msum, pmask, B, S)
        h1, xn, ri, rw, cnt = _mix_out(attn, rec, h, w_out_b[l], ffn_norm[l][None], w_r_hi[l], w_r_lo[l], tril)
        counts = cnt[0, :N_EXPERTS].astype(I32)
        padded = ((counts + EXPERT_BLOCK - 1) // EXPERT_BLOCK) * EXPERT_BLOCK
        pad_end = jnp.cumsum(padded)
        pad_start = pad_end - padded
        expert = ri[:, 0:2]
        dest = ri[:, 2:4] + jnp.sum(jnp.where(expert[:, :, None] == eids, pad_start, 0), axis=-1)
        dest = dest.reshape(n_assign).astype(I32)
        n_used = (pad_end[-1] // EXPERT_BLOCK).astype(I32).reshape(1)
        blk_start = jnp.arange(nblk, dtype=I32) * EXPERT_BLOCK
        block_expert = jnp.minimum(jnp.sum(pad_end[None, :] <= blk_start[:, None], axis=1), N_EXPERTS - 1).astype(I32)
        bounds = jnp.concatenate([jnp.zeros((1,), I32), pad_end.astype(I32)])
        xs = _dispatch(dest, bounds, xn.reshape(T, ROW_SUBLANES, LANES), n_rows)
        run_start = jnp.concatenate([jnp.ones((1,), I32), (block_expert[1:] != block_expert[:-1]).astype(I32)])
        run_buf = (jnp.cumsum(run_start) - 1) % 2
        later = jnp.where(padded > 0, eids, N_EXPERTS)
        next_used = jnp.concatenate([lax.cummin(later[::-1])[::-1][1:], jnp.full((1,), N_EXPERTS, I32)])
        next_used = jnp.where(next_used < N_EXPERTS, next_used, -1)
        sched = jnp.concatenate([n_used, run_buf, next_used[block_expert]]).astype(I32)
        ys = _experts(block_expert, sched, xs.reshape(n_rows * ROW_SUBLANES, LANES), wg_all, wu_all, wd_all, l)
        h = _ple(dest, h1, ys.reshape(n_rows, ROW_SUBLANES, LANES), rw, p2, w_ple_b[l], ple_norm[l][None],
                 ple_gate_norm[l][None], w_pg_b[l], l)
    return h.reshape(B, S, D)
```

```python
import functools

import jax
import jax.numpy as jnp
import numpy as np
from jax import lax
from jax.experimental import pallas as pl
from jax.experimental.pallas import tpu as pltpu

F32 = jnp.float32
BF16 = jnp.bfloat16
I32 = jnp.int32
U32 = jnp.uint32

D_MODEL = 1024
ATTN_HEADS = 8
ATTN_KV_HEADS = 2
ATTN_HEAD_DIM = 64
ATTN_GROUP = ATTN_HEADS // ATTN_KV_HEADS
ATTN_WIDTH = ATTN_HEADS * ATTN_HEAD_DIM
KV_WIDTH = ATTN_KV_HEADS * ATTN_HEAD_DIM
KV_DUP = 2
KV_COLS = KV_DUP * KV_WIDTH
QK_WIDTH = ATTN_WIDTH + KV_COLS
QKV_WIDTH = ATTN_WIDTH + 2 * KV_COLS
WINDOW = 128
ATTN_STEP_BLOCKS = 4
ROPE_THETA = 500000.0
ROPE_DIM = ATTN_HEAD_DIM // 4
ROPE_HALF = ROPE_DIM // 2
REC_HEADS = 4
REC_DIM = 128
REC_WIDTH = REC_HEADS * REC_DIM
REC_CHUNK = 128
HGRN_BROADCAST_MIN = 8
HGRN_BATCH_ROWS = 4
IN_WIDTH = QKV_WIDTH + 4 * REC_WIDTH
N_GROUPS = 4
EXPERTS_PER_GROUP = 8
N_EXPERTS = N_GROUPS * EXPERTS_PER_GROUP
EXPERT_FF = 512
PLE_DIM = 256
RMS_EPS = 1e-6
MASK_VALUE = -1e30
LANES = 128
ROUTE_LANES = LANES
ROW_WORDS = D_MODEL // 2
ROW_SUBLANES = ROW_WORDS // LANES
HIGH_HALF = np.uint32(0xFFFF0000)

TOKEN_TILE = 512
SUB_TILE = 256
EXPERT_BLOCK = 512
EXPERT_SUB = 512
COPY_TOKENS = 512
COPY_UNROLL = 8


def _dot(a, b):
    return jnp.dot(a, b, preferred_element_type=F32)


def _dot_nt(a, b):
    return lax.dot_general(a, b, (((1,), (1,)), ((), ())), preferred_element_type=F32)


def _dot_tn(a, b):
    return lax.dot_general(a, b, (((0,), (0,)), ((), ())), preferred_element_type=F32)


def _store_row_tiles(ref, x, first=0):
    m = x.shape[0]
    lo = lax.bitcast_convert_type(x[:, :ROW_WORDS].astype(BF16).astype(F32), U32) >> 16
    hi = lax.bitcast_convert_type(x[:, ROW_WORDS:].astype(BF16).astype(F32), U32) & HIGH_HALF
    words = lo | hi
    for c in range(ROW_SUBLANES):
        ref[pl.ds(first * ROW_SUBLANES + c, m, stride=ROW_SUBLANES), :] = words[:, c * LANES:(c + 1) * LANES]


def _load_row_tiles(ref, m, first=0, every=1):
    words = jnp.concatenate(
        [ref[pl.ds(first * ROW_SUBLANES + c, m, stride=every * ROW_SUBLANES), :] for c in range(ROW_SUBLANES)], axis=1)
    lo = lax.bitcast_convert_type(words << 16, F32)
    hi = lax.bitcast_convert_type(words & HIGH_HALF, F32)
    return jnp.concatenate([lo, hi], axis=1)


def _sigmoid(x):
    return 1.0 / (1.0 + jnp.exp(-x))


def _rms(x, gain):
    ms = jnp.mean(x * x, axis=-1, keepdims=True)
    return x * lax.rsqrt(ms + RMS_EPS) * gain


def _mix_in_kernel(h_ref, gain_ref, w_ref, rc_ref, rs1_ref, rs2_ref, qkg_ref, seg_ref,
                   za_ref, zr_ref, zf_ref):
    _mix_in_rows(h_ref[...], gain_ref, w_ref, rc_ref, rs1_ref, rs2_ref, qkg_ref, seg_ref, za_ref, zr_ref, zf_ref)


def _mix_in_rows(h, gain_ref, w_ref, rc_ref, rs1_ref, rs2_ref, qkg_ref, seg_ref, za_ref, zr_ref, zf_ref):
    xn = _rms(h, gain_ref[...]).astype(BF16)
    z_a = _dot(xn, w_ref[:, 0:QKV_WIDTH])
    qk = z_a[:, 0:QK_WIDTH]
    seg = _dot((qk * qk).astype(BF16), seg_ref[...]) * (1.0 / ATTN_HEAD_DIM)
    qkn = qk * lax.rsqrt(seg + RMS_EPS) * qkg_ref[...]
    rc, rs1, rs2 = rc_ref[...], rs1_ref[...], rs2_ref[...]
    for c in range(QK_WIDTH // LANES):
        col = qkn[:, c * LANES:(c + 1) * LANES]
        rot = col * rc + pltpu.roll(col, LANES - ROPE_HALF, 1) * rs1 + pltpu.roll(col, ROPE_HALF, 1) * rs2
        za_ref[:, c * LANES:(c + 1) * LANES] = rot.astype(BF16)
    za_ref[:, QK_WIDTH:QKV_WIDTH] = z_a[:, QK_WIDTH:QKV_WIDTH].astype(BF16)
    zr_ref[...] = _dot(xn, w_ref[:, QKV_WIDTH:QKV_WIDTH + 3 * REC_WIDTH]).astype(BF16)
    zf_ref[...] = _dot(xn, w_ref[:, QKV_WIDTH + 3 * REC_WIDTH:IN_WIDTH])


def _mix_in_specs(T, tm, layer, row, fixed, stacked):
    in_specs = [
        pl.BlockSpec((None, 1, D_MODEL), stacked),
        pl.BlockSpec((None, D_MODEL, IN_WIDTH), stacked),
        pl.BlockSpec((tm, LANES), row),
        pl.BlockSpec((tm, LANES), row),
        pl.BlockSpec((tm, LANES), row),
        pl.BlockSpec((None, 1, QK_WIDTH), stacked),
        pl.BlockSpec((QK_WIDTH, QK_WIDTH), fixed),
    ]
    out_specs = [
        pl.BlockSpec((tm, QKV_WIDTH), row),
        pl.BlockSpec((tm, 3 * REC_WIDTH), row),
        pl.BlockSpec((tm, REC_WIDTH), row),
    ]
    out_shape = [
        jax.ShapeDtypeStruct((T, QKV_WIDTH), BF16),
        jax.ShapeDtypeStruct((T, 3 * REC_WIDTH), BF16),
        jax.ShapeDtypeStruct((T, REC_WIDTH), F32),
    ]
    return in_specs, out_specs, out_shape


def _mix_in(h, mix_in_params, layer):
    T = h.shape[0]
    tm = min(TOKEN_TILE, T)
    row = lambda i: (i, 0)
    in_specs, out_specs, out_shape = _mix_in_specs(T, tm, layer, row, lambda i: (0, 0), lambda i: (layer, 0, 0))
    return pl.pallas_call(
        _mix_in_kernel,
        grid=(T // tm,),
        in_specs=[pl.BlockSpec((tm, D_MODEL), row)] + in_specs,
        out_specs=out_specs,
        out_shape=out_shape,
        compiler_params=pltpu.CompilerParams(dimension_semantics=("parallel",)),
        name="mix_in",
    )(h, *mix_in_params)


def _attn_kernel(sink_ref, q_ref, kvc_ref, kvp_ref, o_ref):
    n = pl.program_id(1)
    rows = lax.broadcasted_iota(I32, (ATTN_GROUP * WINDOW, 2 * WINDOW), 0)
    cols = lax.broadcasted_iota(I32, (ATTN_GROUP * WINDOW, 2 * WINDOW), 1)
    qi = rows & (WINDOW - 1)
    in_window = (cols > qi) & (cols <= qi + WINDOW)
    grp = lax.broadcasted_iota(I32, (ATTN_GROUP * WINDOW, 1), 0) // WINDOW
    low_lanes = lax.broadcasted_iota(I32, (1, LANES), 1) < ATTN_HEAD_DIM
    keep_low = jnp.where(low_lanes, 1.0, 0.0).astype(BF16)
    keep_high = jnp.where(low_lanes, 0.0, 1.0).astype(BF16)
    out_low = lax.broadcasted_iota(I32, (WINDOW, LANES), 1) < ATTN_HEAD_DIM
    ones = jnp.ones((2 * WINDOW, LANES), BF16)
    for qb in range(q_ref.shape[0] // WINDOW):
        blk = slice(qb * WINDOW, (qb + 1) * WINDOW)
        kvc = kvc_ref[blk, :]
        kvp = kvp_ref[...] if qb == 0 else kvc_ref[(qb - 1) * WINDOW:qb * WINDOW, :]
        valid = in_window & ((cols >= WINDOW) | (n > 0)) if qb == 0 else in_window
        for j in range(ATTN_KV_HEADS):
            kcols = slice(j * LANES, (j + 1) * LANES)
            vcols = slice(KV_COLS + j * LANES, KV_COLS + (j + 1) * LANES)
            kk = jnp.concatenate([kvp[:, kcols], kvc[:, kcols]], axis=0)
            vv = jnp.concatenate([kvp[:, vcols], kvc[:, vcols]], axis=0)
            pairs = jnp.concatenate([q_ref[blk, (2 * j) * LANES:(2 * j + 1) * LANES],
                                     q_ref[blk, (2 * j + 1) * LANES:(2 * j + 2) * LANES]], axis=0)
            heads = [4 * j, 4 * j + 2, 4 * j + 1, 4 * j + 3]
            s = jnp.concatenate([_dot_nt(pairs, kk * keep_low), _dot_nt(pairs, kk * keep_high)], axis=0)
            s = jnp.where(valid, s, MASK_VALUE)
            sink = jnp.zeros((ATTN_GROUP * WINDOW, 1), F32)
            for g, hh in enumerate(heads):
                sink = jnp.where(grp == g, sink_ref[hh], sink)
            m = jnp.maximum(jnp.max(s, axis=-1, keepdims=True), sink)
            e = jnp.exp(s - m).astype(BF16)
            o2 = _dot(e, jnp.concatenate([vv, ones], axis=1))
            on = o2[:, :LANES] * (1.0 / (o2[:, LANES:] + jnp.exp(sink - m)))
            for c in range(2):
                pair = jnp.where(out_low, on[c * WINDOW:(c + 1) * WINDOW], on[(2 + c) * WINDOW:(3 + c) * WINDOW])
                o_ref[blk, (2 * j + c) * LANES:(2 * j + c + 1) * LANES] = pair.astype(BF16)


def _attention(za, sinks, B, S):
    nq = ATTN_STEP_BLOCKS if (S // WINDOW) % ATTN_STEP_BLOCKS == 0 else 1
    ns = S // (nq * WINDOW)
    kvblk = ATTN_WIDTH // (2 * KV_COLS)
    return pl.pallas_call(
        _attn_kernel,
        grid=(B, ns),
        in_specs=[
            pl.BlockSpec(memory_space=pltpu.SMEM),
            pl.BlockSpec((nq * WINDOW, ATTN_WIDTH), lambda b, n: (b * ns + n, 0)),
            pl.BlockSpec((nq * WINDOW, 2 * KV_COLS), lambda b, n: (b * ns + n, kvblk)),
            pl.BlockSpec((WINDOW, 2 * KV_COLS), lambda b, n: ((b * ns + n) * nq - jnp.minimum(n, 1), kvblk)),
        ],
        out_specs=pl.BlockSpec((nq * WINDOW, ATTN_WIDTH), lambda b, n: (b * ns + n, 0)),
        out_shape=jax.ShapeDtypeStruct((B * S, ATTN_WIDTH), BF16),
        compiler_params=pltpu.CompilerParams(dimension_semantics=("parallel", "arbitrary")),
        name="swa_attention",
    )(sinks, za, za, za)


def _hgrn_levels(C):
    out, s = [], C // 2
    while s >= 1:
        out.append(s)
        s //= 2
    return out


def _hgrn_constants(C):
    r = np.arange(C)[:, None]
    u = np.arange(C)[None, :]
    mats = [u <= r]
    masks = []
    for s in _hgrn_levels(C):
        mid = (r // (2 * s)) * 2 * s + s
        if s < HGRN_BROADCAST_MIN:
            mats.append(np.where(r >= mid, (u >= mid) & (u <= r), (u > r) & (u < mid)))
        masks.append(((r // (2 * s)) == (u // (2 * s))) & ((r & s) != 0) & ((u & s) == 0))
    masks.append(r == u)
    return (np.concatenate(mats, axis=0).astype(np.float32), np.stack(masks).astype(np.float32))


def _hgrn_head(hd, zr_ref, zf_ref, lb_ref, gain_ref, msum_ref, pmask_ref, o_ref, st_ref):
    C = zf_ref.shape[0]
    levels = _hgrn_levels(C)
    cs = slice(hd * REC_DIM, (hd + 1) * REC_DIM)
    zq = zr_ref[:, cs].astype(F32)
    v = zr_ref[:, REC_WIDTH + hd * REC_DIM:REC_WIDTH + (hd + 1) * REC_DIM]
    og = zr_ref[:, 2 * REC_WIDTH + hd * REC_DIM:2 * REC_WIDTH + (hd + 1) * REC_DIM].astype(F32)
    z = zf_ref[:, cs]
    lb = lb_ref[:, cs]
    qp = zq * _sigmoid(zq)
    a = jnp.exp(-jnp.abs(z))
    r = 1.0 / (1.0 + a)
    pos = z >= 0
    g = jnp.log2(lb + (1.0 - lb) * jnp.where(pos, r, a * r))
    k = (1.0 - lb) * jnp.where(pos, a * r, r)
    g_hi = g.astype(BF16)
    g_lo = (g - g_hi.astype(F32)).astype(BF16)
    x2 = _dot(msum_ref[...], jnp.concatenate([g_hi, g_lo], axis=1))
    sums = x2[:, :REC_DIM] + x2[:, REC_DIM:]
    b = sums[0:C]
    b_last = b[C - 1:C, :]
    e_b = jnp.exp2(b)
    st = st_ref[...]
    o = _dot_nt((qp * e_b).astype(BF16), st.astype(BF16))
    amat = pmask_ref[len(levels)] * _dot_nt(qp.astype(BF16), k.astype(BF16))
    n_matrix_levels = 0
    for li, s in enumerate(levels):
        if s >= HGRN_BROADCAST_MIN:
            ref_rows = [jnp.broadcast_to(b[j + s - 1:j + s, :], (2 * s, REC_DIM)) for j in range(0, C, 2 * s)]
            expo = -jnp.abs(b - jnp.concatenate(ref_rows, axis=0))
        else:
            n_matrix_levels += 1
            expo = sums[n_matrix_levels * C:(n_matrix_levels + 1) * C]
        e_l = jnp.exp2(expo)
        amat = amat + pmask_ref[li] * _dot_nt((qp * e_l).astype(BF16), (k * e_l).astype(BF16))
    o = o + _dot(amat.astype(BF16), v)
    st_ref[...] = st * e_b[C - 1:C, :] + _dot_tn(v, (k * jnp.exp2(b_last - b)).astype(BF16))
    og_act = og * _sigmoid(og)
    o_ref[:, cs] = (_rms(o, gain_ref[...]) * og_act).astype(BF16)


def _hgrn_kernel(zr_ref, zf_ref, lb_ref, gain_ref, msum_ref, pmask_ref, o_ref, st_ref):
    @pl.when(pl.program_id(1) == 0)
    def _():
        st_ref[...] = jnp.zeros_like(st_ref)

    for bb in range(zf_ref.shape[0]):
        for hd in range(REC_HEADS):
            _hgrn_head(hd, zr_ref.at[bb], zf_ref.at[bb], lb_ref, gain_ref, msum_ref, pmask_ref, o_ref.at[bb],
                       st_ref.at[bb * REC_HEADS + hd])


def _hgrn2(zr, zf, lb, gain, msum, pmask, B, S):
    C = min(REC_CHUNK, S)
    nb = HGRN_BATCH_ROWS if B % HGRN_BATCH_ROWS == 0 else 1
    blk = lambda b, c: (b, c, 0)
    out = pl.pallas_call(
        _hgrn_kernel,
        grid=(B // nb, S // C),
        in_specs=[
            pl.BlockSpec((nb, C, 3 * REC_WIDTH), blk),
            pl.BlockSpec((nb, C, REC_WIDTH), blk),
            pl.BlockSpec((1, REC_WIDTH), lambda b, c: (0, 0)),
            pl.BlockSpec((1, REC_DIM), lambda b, c: (0, 0)),
            pl.BlockSpec(msum.shape, lambda b, c: (0, 0)),
            pl.BlockSpec(pmask.shape, lambda b, c: (0, 0, 0)),
        ],
        out_specs=pl.BlockSpec((nb, C, REC_WIDTH), blk),
        out_shape=jax.ShapeDtypeStruct((B, S, REC_WIDTH), BF16),
        scratch_shapes=[pltpu.VMEM((nb * REC_HEADS, REC_DIM, REC_DIM), F32)],
        compiler_params=pltpu.CompilerParams(dimension_semantics=("parallel", "arbitrary")),
        name="hgrn2",
    )(zr.reshape(B, S, 3 * REC_WIDTH), zf.reshape(B, S, REC_WIDTH), lb, gain, msum, pmask)
    return out.reshape(B * S, REC_WIDTH)


def _mix_out_kernel(attn_ref, rec_ref, h_ref, wo_ref, gain_ref, wr_ref, tril_ref,
                    h1_ref, xn_ref, ri_ref, rw_ref, cnt_ref, carry_ref):
    @pl.when(pl.program_id(0) == 0)
    def _():
        carry_ref[...] = jnp.zeros_like(carry_ref)

    sub = tril_ref.shape[0]
    carry = carry_ref[...]
    for s in range(h_ref.shape[0] // sub):
        carry = _mix_out_rows(slice(s * sub, (s + 1) * sub), s * sub, carry, attn_ref, rec_ref, h_ref, wo_ref,
                              gain_ref, wr_ref, tril_ref, h1_ref, xn_ref, ri_ref, rw_ref)
    carry_ref[...] = carry
    cnt_ref[...] = jnp.broadcast_to(carry, cnt_ref.shape)


def _mix_out_rows(rows, first, carry, attn_ref, rec_ref, h_ref, wo_ref, gain_ref, wr_ref, tril_ref,
                  h1_ref, xn_ref, ri_ref, rw_ref):
    mixed = jnp.concatenate([attn_ref[rows, :], rec_ref[rows, :]], axis=1)
    h1 = h_ref[rows, :] + _dot(mixed, wo_ref[...])
    h1_ref[rows, :] = h1
    xn = _rms(h1, gain_ref[...])
    _store_row_tiles(xn_ref, xn, first=first)
    xh = xn.astype(BF16)
    xl = (xn - xh.astype(F32)).astype(BF16)
    both = _dot(xh, wr_ref[...])
    logits = both[:, :ROUTE_LANES] + both[:, ROUTE_LANES:] + _dot(xl, wr_ref[:, :ROUTE_LANES])
    lane = lax.broadcasted_iota(I32, logits.shape, 1)
    lanef = lane.astype(F32)
    neg = jnp.float32(-jnp.inf)
    big = jnp.float32(1e9)
    gl = jnp.where(lane < N_GROUPS, logits, neg)
    gmax = jnp.max(gl, axis=-1, keepdims=True)
    gidx = jnp.min(jnp.where(gl == gmax, lanef, big), axis=-1, keepdims=True)
    p_group = 1.0 / jnp.sum(jnp.where(lane < N_GROUPS, jnp.exp(logits - gmax), 0.0), axis=-1, keepdims=True)
    lo = N_GROUPS + gidx * EXPERTS_PER_GROUP
    el = jnp.where((lanef >= lo) & (lanef < lo + EXPERTS_PER_GROUP), logits, neg)
    t1 = jnp.max(el, axis=-1, keepdims=True)
    i1 = jnp.min(jnp.where(el == t1, lanef, big), axis=-1, keepdims=True)
    el2 = jnp.where(lanef == i1, neg, el)
    t2 = jnp.max(el2, axis=-1, keepdims=True)
    i2 = jnp.min(jnp.where(el2 == t2, lanef, big), axis=-1, keepdims=True)
    r21 = jnp.exp(t2 - t1)
    w1 = p_group / (1.0 + r21)
    w2 = w1 * r21
    e1 = i1 - N_GROUPS
    e2 = i2 - N_GROUPS
    oh1 = (lanef == e1).astype(F32)
    oh2 = (lanef == e2).astype(F32)
    oh = oh1 + oh2
    prefix = _dot(tril_ref[...], oh.astype(BF16)) + carry
    rank1 = jnp.sum(prefix * oh1, axis=-1, keepdims=True)
    rank2 = jnp.sum(prefix * oh2, axis=-1, keepdims=True)
    ri = jnp.where(lane == 0, e1, jnp.where(lane == 1, e2, jnp.where(lane == 2, rank1, jnp.where(lane == 3, rank2, 0.0))))
    ri_ref[rows, :] = ri.astype(I32)
    rw_ref[rows, :] = jnp.where(lane == 0, w1, jnp.where(lane == 1, w2, 0.0))
    return carry + jnp.sum(oh, axis=0, keepdims=True)


def _mix_out(attn, rec, h, wo, gain, wr, tril, layer):
    T = h.shape[0]
    tm = min(TOKEN_TILE, T)
    row = lambda i: (i, 0)
    fixed = lambda i: (0, 0)
    stacked = lambda i: (layer, 0, 0)
    return pl.pallas_call(
        _mix_out_kernel,
        grid=(T // tm,),
        in_specs=[
            pl.BlockSpec((tm, ATTN_WIDTH), row),
            pl.BlockSpec((tm, REC_WIDTH), row),
            pl.BlockSpec((tm, D_MODEL), row),
            pl.BlockSpec((None, ATTN_WIDTH + REC_WIDTH, D_MODEL), stacked),
            pl.BlockSpec((None, 1, D_MODEL), stacked),
            pl.BlockSpec((None, D_MODEL, 2 * ROUTE_LANES), stacked),
            pl.BlockSpec(tril.shape, fixed),
        ],
        out_specs=[
            pl.BlockSpec((tm, D_MODEL), row),
            pl.BlockSpec((tm * ROW_SUBLANES, LANES), row),
            pl.BlockSpec((tm, ROUTE_LANES), row),
            pl.BlockSpec((tm, ROUTE_LANES), row),
            pl.BlockSpec((8, ROUTE_LANES), fixed),
        ],
        out_shape=[
            jax.ShapeDtypeStruct((T, D_MODEL), F32),
            jax.ShapeDtypeStruct((T * ROW_SUBLANES, LANES), U32),
            jax.ShapeDtypeStruct((T, ROUTE_LANES), I32),
            jax.ShapeDtypeStruct((T, ROUTE_LANES), F32),
            jax.ShapeDtypeStruct((8, ROUTE_LANES), F32),
        ],
        scratch_shapes=[pltpu.VMEM((1, ROUTE_LANES), F32)],
        compiler_params=pltpu.CompilerParams(dimension_semantics=("arbitrary",)),
        name="mix_out_router",
    )(attn, rec, h, wo, gain, wr, tril)


def _row_move_loops(copy, tokens):
    def start(t, carry):
        copy(t, 0).start()
        copy(t, 1).start()
        return carry

    def wait(t, carry):
        copy(t, 0).wait()
        copy(t, 1).wait()
        return carry

    lax.fori_loop(0, tokens, start, 0, unroll=COPY_UNROLL)
    lax.fori_loop(0, tokens, wait, 0, unroll=COPY_UNROLL)


def _dispatch_kernel(dest_ref, bound_ref, x_ref, out_ref, zero_ref, sem, *, tokens):
    @pl.when(pl.program_id(0) == 0)
    def _():
        zero_ref[...] = jnp.zeros_like(zero_ref)

        def fill(e):
            return pltpu.make_async_copy(zero_ref, out_ref.at[pl.ds(bound_ref[e + 1] - EXPERT_BLOCK, EXPERT_BLOCK)], sem)

        for e in range(N_EXPERTS):
            pl.when(bound_ref[e + 1] > bound_ref[e])(lambda e=e: fill(e).start())
        for e in range(N_EXPERTS):
            pl.when(bound_ref[e + 1] > bound_ref[e])(lambda e=e: fill(e).wait())

        def tail(i):
            return pltpu.make_async_copy(zero_ref, out_ref.at[pl.ds(i * EXPERT_BLOCK, EXPERT_BLOCK)], sem)

        first_unused = bound_ref[N_EXPERTS] // EXPERT_BLOCK
        n_blocks = out_ref.shape[0] // EXPERT_BLOCK
        lax.fori_loop(first_unused, n_blocks, lambda i, c: (tail(i).start(), c)[1], 0)
        lax.fori_loop(first_unused, n_blocks, lambda i, c: (tail(i).wait(), c)[1], 0)

    base = 2 * pl.program_id(0) * tokens

    def copy(t, j):
        return pltpu.make_async_copy(x_ref.at[t], out_ref.at[dest_ref[base + 2 * t + j]], sem)

    _row_move_loops(copy, tokens)


def _dispatch(dest, bounds, x3, n_rows):
    tokens = min(COPY_TOKENS, x3.shape[0])
    tile = x3.shape[1:]
    return pl.pallas_call(
        functools.partial(_dispatch_kernel, tokens=tokens),
        grid_spec=pltpu.PrefetchScalarGridSpec(
            num_scalar_prefetch=2,
            grid=(x3.shape[0] // tokens,),
            in_specs=[pl.BlockSpec((tokens,) + tile, lambda i, d, b: (i, 0, 0))],
            out_specs=pl.BlockSpec(memory_space=pl.ANY),
            scratch_shapes=[pltpu.VMEM((EXPERT_BLOCK,) + tile, x3.dtype), pltpu.SemaphoreType.DMA],
        ),
        out_shape=jax.ShapeDtypeStruct((n_rows,) + tile, x3.dtype),
        compiler_params=pltpu.CompilerParams(dimension_semantics=("arbitrary",)),
        name="moe_dispatch_rows",
    )(dest, bounds, x3)


def _expert_kernel(be_ref, sched_ref, x_ref, wg_ref, wu_ref, wd_ref, y_ref,
                   wgf_ref, wuf_ref, wdf_ref, wgb_ref, wub_ref, wdb_ref, sem, *, layer):
    i = pl.program_id(0)
    nblk = pl.num_programs(0)
    used = i < sched_ref[0]
    buf = sched_ref[1 + i]
    next_expert = sched_ref[1 + nblk + i]
    first_of_run = (i == 0) | (be_ref[i] != be_ref[jnp.maximum(i - 1, 0)])

    def fetch(expert, slot):
        e = layer * N_EXPERTS + expert
        return [pltpu.make_async_copy(src.at[e], dst.at[slot], sem.at[slot, n])
                for n, (src, dst) in enumerate(((wg_ref, wgf_ref), (wu_ref, wuf_ref), (wd_ref, wdf_ref)))]

    @pl.when(i == 0)
    def _():
        for c in fetch(be_ref[0], buf):
            c.start()

    @pl.when(used & first_of_run & (next_expert >= 0))
    def _():
        for c in fetch(next_expert, 1 - buf):
            c.start()

    @pl.when(used & first_of_run)
    def _():
        for c in fetch(be_ref[i], buf):
            c.wait()
        wgb_ref[...] = wgf_ref[buf].astype(BF16)
        wub_ref[...] = wuf_ref[buf].astype(BF16)
        wdb_ref[...] = wdf_ref[buf].astype(BF16)

    @pl.when(used)
    def _():
        for s in range(EXPERT_BLOCK // EXPERT_SUB):
            x = _load_row_tiles(x_ref, EXPERT_SUB, first=s * EXPERT_SUB).astype(BF16)
            gate = _dot(x, wgb_ref[...])
            up = _dot(x, wub_ref[...])
            hidden = (gate * _sigmoid(gate) * up).astype(BF16)
            _store_row_tiles(y_ref, _dot(hidden, wdb_ref[...]), first=s * EXPERT_SUB)

    @pl.when(jnp.logical_not(used))
    def _():
        y_ref[...] = jnp.zeros_like(y_ref)


def _experts(block_expert, sched, xs, wg, wu, wd, layer):
    blk = EXPERT_BLOCK * ROW_SUBLANES
    nblk = xs.shape[0] // blk
    hbm = pl.BlockSpec(memory_space=pl.ANY)
    return pl.pallas_call(
        functools.partial(_expert_kernel, layer=layer),
        grid_spec=pltpu.PrefetchScalarGridSpec(
            num_scalar_prefetch=2,
            grid=(nblk,),
            in_specs=[pl.BlockSpec((blk, LANES), lambda i, be, sc: (jnp.minimum(i, sc[0] - 1), 0)), hbm, hbm, hbm],
            out_specs=pl.BlockSpec((blk, LANES), lambda i, be, sc: (i, 0)),
            scratch_shapes=[
                pltpu.VMEM((2, D_MODEL, EXPERT_FF), F32),
                pltpu.VMEM((2, D_MODEL, EXPERT_FF), F32),
                pltpu.VMEM((2, EXPERT_FF, D_MODEL), F32),
                pltpu.VMEM((D_MODEL, EXPERT_FF), BF16),
                pltpu.VMEM((D_MODEL, EXPERT_FF), BF16),
                pltpu.VMEM((EXPERT_FF, D_MODEL), BF16),
                pltpu.SemaphoreType.DMA((2, 3)),
            ],
        ),
        out_shape=jax.ShapeDtypeStruct(xs.shape, xs.dtype),
        compiler_params=pltpu.CompilerParams(dimension_semantics=("arbitrary",)),
        name="moe_experts",
    )(block_expert, sched, xs, wg, wu, wd)


def _ple_kernel(dest_ref, h1_ref, ys_ref, rw_ref, p_ref, wple_ref, pgain_ref, ggain_ref, wpg_ref, *rest):
    *front, ybuf_ref, sem = rest
    mix_in_refs, o_ref, z_refs = ((), front[0], ()) if len(front) == 1 else (front[:-4], front[-4], front[-3:])
    i = pl.program_id(0)
    tm = h1_ref.shape[0]
    slot = i % 2

    def row_copy(tile, buf, t, j):
        a = 2 * t + j
        dst = ybuf_ref.at[buf, pl.ds(pl.multiple_of(a * ROW_SUBLANES, ROW_SUBLANES), ROW_SUBLANES), :]
        return pltpu.make_async_copy(ys_ref.at[dest_ref[2 * tile * tm + a]], dst, sem.at[buf])

    def start_tile(tile, buf):
        def body(t, carry):
            row_copy(tile, buf, t, 0).start()
            row_copy(tile, buf, t, 1).start()
            return carry
        lax.fori_loop(0, tm, body, 0, unroll=COPY_UNROLL)

    def wait_tile(tile, buf):
        def body(t, carry):
            row_copy(tile, buf, t, 0).wait()
            row_copy(tile, buf, t, 1).wait()
            return carry
        lax.fori_loop(0, tm, body, 0, unroll=COPY_UNROLL)

    n_tiles = pl.num_programs(0)
    pl.when(i == 0)(lambda: start_tile(0, 0))
    wait_tile(i, slot)
    nxt = jnp.minimum(i + 1, n_tiles - 1)

    yg_ref = ybuf_ref.at[slot]
    sub = min(SUB_TILE, tm)
    for s in range(tm // sub):
        rows = slice(s * sub, (s + 1) * sub)
        rw = rw_ref[rows, :]
        y1 = _load_row_tiles(yg_ref, sub, first=2 * s * sub, every=2)
        y2 = _load_row_tiles(yg_ref, sub, first=2 * s * sub + 1, every=2)
        h2 = h1_ref[rows, :] + rw[:, 0:1] * y1 + rw[:, 1:2] * y2
        ple = _rms(_dot(p_ref[rows, :].astype(BF16), wple_ref[...]), pgain_ref[...])
        gate = _sigmoid(_dot(_rms(h2, ggain_ref[...]).astype(BF16), wpg_ref[...]))
        o_ref[rows, :] = h2 + ple * gate
        for t in range(s * sub, (s + 1) * sub):
            row_copy(nxt, 1 - slot, t, 0).start()
            row_copy(nxt, 1 - slot, t, 1).start()

    if mix_in_refs:
        _mix_in_rows(o_ref[...], *mix_in_refs, *z_refs)

    pl.when(i == n_tiles - 1)(lambda: wait_tile(i, 1 - slot))


def _ple(dest, h1, ys3, rw, p, ple_params, layer, mix_in_params=None):
    T = h1.shape[0]
    tm = min(TOKEN_TILE, T)
    nt = T // tm
    row = lambda i, d: (i, 0)
    fixed = lambda i, d: (0, 0)
    stacked = lambda i, d: (layer, 0, 0)
    in_specs = [
        pl.BlockSpec((tm, D_MODEL), row),
        pl.BlockSpec(memory_space=pl.ANY),
        pl.BlockSpec((tm, ROUTE_LANES), row),
        pl.BlockSpec((tm, PLE_DIM), lambda i, d: (layer * nt + i, 0)),
        pl.BlockSpec((None, PLE_DIM, D_MODEL), stacked),
        pl.BlockSpec((None, 1, D_MODEL), stacked),
        pl.BlockSpec((None, 1, D_MODEL), stacked),
        pl.BlockSpec((None, D_MODEL, D_MODEL), stacked),
    ]
    out_specs = [pl.BlockSpec((tm, D_MODEL), row)]
    out_shape = [jax.ShapeDtypeStruct((T, D_MODEL), F32)]
    operands = (dest, h1, ys3, rw, p) + tuple(ple_params)
    if mix_in_params is not None:
        mi_in, mi_out, mi_shape = _mix_in_specs(T, tm, layer + 1, row, fixed, lambda i, d: (layer + 1, 0, 0))
        in_specs, out_specs, out_shape = in_specs + mi_in, out_specs + mi_out, out_shape + mi_shape
        operands = operands + tuple(mix_in_params)
    return pl.pallas_call(
        _ple_kernel,
        grid_spec=pltpu.PrefetchScalarGridSpec(
            num_scalar_prefetch=1,
            grid=(nt,),
            in_specs=in_specs,
            out_specs=out_specs,
            scratch_shapes=[pltpu.VMEM((2, 2 * tm * ROW_SUBLANES, LANES), ys3.dtype), pltpu.SemaphoreType.DMA((2,))],
        ),
        out_shape=out_shape,
        compiler_params=pltpu.CompilerParams(dimension_semantics=("arbitrary",)),
        name="combine_ple" if mix_in_params is None else "combine_ple_mix_in",
    )(*operands)


def _rope_tables(positions):
    inv_freq = ROPE_THETA ** (-jnp.arange(0, ROPE_DIM, 2, dtype=F32) / ROPE_DIM)
    ang = positions.astype(F32).reshape(-1, 1) * inv_freq
    cos, sin = jnp.cos(ang), jnp.sin(ang)
    T = ang.shape[0]
    rest = jnp.zeros((T, ATTN_HEAD_DIM - ROPE_DIM), F32)
    zero = jnp.zeros((T, ROPE_HALF), F32)
    rc = jnp.concatenate([cos, cos, rest + 1.0], axis=1)
    rs1 = jnp.concatenate([-sin, zero, rest], axis=1)
    rs2 = jnp.concatenate([zero, sin, rest], axis=1)
    reps = LANES // ATTN_HEAD_DIM
    return tuple(jnp.tile(t, (1, reps)) for t in (rc, rs1, rs2))


def kernel(x, p, positions, mix_norm, w_in, q_norm, k_norm, sinks, lb_logits, rec_norm, w_out, ffn_norm,
           w_router_group, w_router_expert, w_gate, w_up, w_down, w_ple, ple_norm, ple_gate_norm, w_ple_gate):
    B, S, D = x.shape
    depth = w_in.shape[0]
    T = B * S
    n_assign = 2 * T
    assert D == D_MODEL and S % WINDOW == 0 and T % min(TOKEN_TILE, T) == 0

    rc, rs1, rs2 = _rope_tables(positions)
    lb_sm = jax.nn.softmax(lb_logits.astype(F32), axis=0)
    lower_bounds = jnp.cumsum(lb_sm, axis=0) - lb_sm[0:1]

    def twice(cols):
        heads = cols.reshape(depth, D_MODEL, ATTN_KV_HEADS, 1, ATTN_HEAD_DIM)
        return jnp.broadcast_to(heads, (depth, D_MODEL, ATTN_KV_HEADS, KV_DUP, ATTN_HEAD_DIM)).reshape(
            depth, D_MODEL, KV_COLS)

    k0, v0, r0 = ATTN_WIDTH, ATTN_WIDTH + KV_WIDTH, ATTN_WIDTH + 2 * KV_WIDTH
    w_in_p = jnp.concatenate([w_in[:, :, :k0], twice(w_in[:, :, k0:v0]), twice(w_in[:, :, v0:r0]),
                              w_in[:, :, r0:r0 + REC_WIDTH], w_in[:, :, r0 + 2 * REC_WIDTH:],
                              w_in[:, :, r0 + REC_WIDTH:r0 + 2 * REC_WIDTH]], axis=2).astype(BF16)
    w_out_b = w_out.astype(BF16)
    wg_all = w_gate.reshape(depth * N_EXPERTS, D_MODEL, EXPERT_FF)
    wu_all = w_up.reshape(depth * N_EXPERTS, D_MODEL, EXPERT_FF)
    wd_all = w_down.reshape(depth * N_EXPERTS, EXPERT_FF, D_MODEL)
    w_ple_b = w_ple.astype(BF16)
    w_pg_b = w_ple_gate.astype(BF16)
    w_r = jnp.concatenate([w_router_group, w_router_expert,
                           jnp.zeros((depth, D_MODEL, ROUTE_LANES - N_GROUPS - N_EXPERTS), F32)], axis=2)
    w_r_hi = w_r.astype(BF16)
    w_r2 = jnp.concatenate([w_r_hi, (w_r - w_r_hi.astype(F32)).astype(BF16)], axis=2)
    qk_gain = jnp.concatenate([jnp.tile(q_norm, (1, ATTN_HEADS)) * (ATTN_HEAD_DIM ** -0.5),
                               jnp.tile(k_norm, (1, ATTN_KV_HEADS * KV_DUP))], axis=1)
    seg_id = np.arange(QK_WIDTH) // ATTN_HEAD_DIM
    seg = jnp.asarray(seg_id[:, None] == seg_id[None, :], BF16)
    msum_np, pmask_np = _hgrn_constants(min(REC_CHUNK, S))
    msum = jnp.asarray(msum_np, BF16)
    pmask = jnp.asarray(pmask_np, F32)
    sub = min(SUB_TILE, T)
    tril = jnp.asarray(np.tril(np.ones((sub, sub), np.float32), -1), BF16)

    n_rows = n_assign + N_EXPERTS * EXPERT_BLOCK
    nblk = n_rows // EXPERT_BLOCK
    eids = jnp.arange(N_EXPERTS, dtype=I32)
    p2 = p.reshape(depth * T, PLE_DIM)

    mix_in_params = (mix_norm[:, None, :], w_in_p, rc, rs1, rs2, qk_gain[:, None, :], seg)
    ple_params = (w_ple_b, ple_norm[:, None, :], ple_gate_norm[:, None, :], w_pg_b)
    ffn_gain = ffn_norm[:, None, :]

    h = x.reshape(T, D)
    za, zr, zf = _mix_in(h, mix_in_params, 0)
    for l in range(depth):
        attn = _attention(za, sinks[l], B, S)
        rec = _hgrn2(zr, zf, lower_bounds[l][None], rec_norm[l][None], msum, pmask, B, S)
        h1, xn, ri, rw, cnt = _mix_out(attn, rec, h, w_out_b, ffn_gain, w_r2, tril, l)
        counts = cnt[0, :N_EXPERTS].astype(I32)
        padded = ((counts + EXPERT_BLOCK - 1) // EXPERT_BLOCK) * EXPERT_BLOCK
        pad_end = jnp.cumsum(padded)
        pad_start = pad_end - padded
        expert = ri[:, 0:2]
        dest = ri[:, 2:4] + jnp.sum(jnp.where(expert[:, :, None] == eids, pad_start, 0), axis=-1)
        dest = dest.reshape(n_assign).astype(I32)
        n_used = (pad_end[-1] // EXPERT_BLOCK).astype(I32).reshape(1)
        blk_start = jnp.arange(nblk, dtype=I32) * EXPERT_BLOCK
        block_expert = jnp.minimum(jnp.sum(pad_end[None, :] <= blk_start[:, None], axis=1), N_EXPERTS - 1).astype(I32)
        bounds = jnp.concatenate([jnp.zeros((1,), I32), pad_end.astype(I32)])
        xs = _dispatch(dest, bounds, xn.reshape(T, ROW_SUBLANES, LANES), n_rows)
        run_start = jnp.concatenate([jnp.ones((1,), I32), (block_expert[1:] != block_expert[:-1]).astype(I32)])
        run_buf = (jnp.cumsum(run_start) - 1) % 2
        later = jnp.where(padded > 0, eids, N_EXPERTS)
        next_used = jnp.concatenate([lax.cummin(later[::-1])[::-1][1:], jnp.full((1,), N_EXPERTS, I32)])
        next_used = jnp.where(next_used < N_EXPERTS, next_used, -1)
        sched = jnp.concatenate([n_used, run_buf, next_used[block_expert]]).astype(I32)
        ys = _experts(block_expert, sched, xs.reshape(n_rows * ROW_SUBLANES, LANES), wg_all, wu_all, wd_all, l)
        ys3 = ys.reshape(n_rows, ROW_SUBLANES, LANES)
        if l + 1 < depth:
            h, za, zr, zf = _ple(dest, h1, ys3, rw, p2, ple_params, l, mix_in_params)
        else:
            (h,) = _ple(dest, h1, ys3, rw, p2, ple_params, l)
    return h.reshape(B, S, D)
```

```python
import functools

import jax
import jax.numpy as jnp
import numpy as np
from jax import lax
from jax.experimental import pallas as pl
from jax.experimental.pallas import tpu as pltpu

F32 = jnp.float32
BF16 = jnp.bfloat16
I32 = jnp.int32
U32 = jnp.uint32

D_MODEL = 1024
ATTN_HEADS = 8
ATTN_KV_HEADS = 2
ATTN_HEAD_DIM = 64
ATTN_GROUP = ATTN_HEADS // ATTN_KV_HEADS
ATTN_WIDTH = ATTN_HEADS * ATTN_HEAD_DIM
KV_WIDTH = ATTN_KV_HEADS * ATTN_HEAD_DIM
KV_DUP = 2
KV_COLS = KV_DUP * KV_WIDTH
QK_WIDTH = ATTN_WIDTH + KV_COLS
QKV_WIDTH = ATTN_WIDTH + 2 * KV_COLS
WINDOW = 128
ATTN_STEP_BLOCKS = 4
ROPE_THETA = 500000.0
ROPE_DIM = ATTN_HEAD_DIM // 4
ROPE_HALF = ROPE_DIM // 2
REC_HEADS = 4
REC_DIM = 128
REC_WIDTH = REC_HEADS * REC_DIM
REC_CHUNK = 128
HGRN_BROADCAST_MIN = 8
HGRN_BATCH_ROWS = 4
IN_WIDTH = QKV_WIDTH + 4 * REC_WIDTH
N_GROUPS = 4
EXPERTS_PER_GROUP = 8
N_EXPERTS = N_GROUPS * EXPERTS_PER_GROUP
EXPERT_FF = 512
PLE_DIM = 256
RMS_EPS = 1e-6
MASK_VALUE = -1e30
LANES = 128
ROUTE_LANES = LANES
ROW_WORDS = D_MODEL // 2
ROW_SUBLANES = ROW_WORDS // LANES
HIGH_HALF = np.uint32(0xFFFF0000)

TOKEN_TILE = 512
SUB_TILE = 256
EXPERT_BLOCK = 512
EXPERT_SUB = 512
COPY_TOKENS = 512
COPY_UNROLL = 8


def _dot(a, b):
    return jnp.dot(a, b, preferred_element_type=F32)


def _dot_nt(a, b):
    return lax.dot_general(a, b, (((1,), (1,)), ((), ())), preferred_element_type=F32)


def _dot_tn(a, b):
    return lax.dot_general(a, b, (((0,), (0,)), ((), ())), preferred_element_type=F32)


def _store_row_tiles(ref, x, first=0):
    m = x.shape[0]
    lo = lax.bitcast_convert_type(x[:, :ROW_WORDS].astype(BF16).astype(F32), U32) >> 16
    hi = lax.bitcast_convert_type(x[:, ROW_WORDS:].astype(BF16).astype(F32), U32) & HIGH_HALF
    words = lo | hi
    for c in range(ROW_SUBLANES):
        ref[pl.ds(first * ROW_SUBLANES + c, m, stride=ROW_SUBLANES), :] = words[:, c * LANES:(c + 1) * LANES]


def _load_row_tiles(ref, m, first=0, every=1):
    words = jnp.concatenate(
        [ref[pl.ds(first * ROW_SUBLANES + c, m, stride=every * ROW_SUBLANES), :] for c in range(ROW_SUBLANES)], axis=1)
    lo = lax.bitcast_convert_type(words << 16, F32)
    hi = lax.bitcast_convert_type(words & HIGH_HALF, F32)
    return jnp.concatenate([lo, hi], axis=1)


def _sigmoid(x):
    return 1.0 / (1.0 + jnp.exp(-x))


def _rms(x, gain):
    ms = jnp.mean(x * x, axis=-1, keepdims=True)
    return x * lax.rsqrt(ms + RMS_EPS) * gain


def _mix_in_kernel(h_ref, gain_ref, w_ref, rc_ref, rs1_ref, rs2_ref, qkg_ref, seg_ref,
                   za_ref, zr_ref, zf_ref):
    xn = _rms(h_ref[...], gain_ref[...]).astype(BF16)
    z_a = _dot(xn, w_ref[:, 0:QKV_WIDTH])
    qk = z_a[:, 0:QK_WIDTH]
    seg = _dot((qk * qk).astype(BF16), seg_ref[...]) * (1.0 / ATTN_HEAD_DIM)
    qkn = qk * lax.rsqrt(seg + RMS_EPS) * qkg_ref[...]
    rc, rs1, rs2 = rc_ref[...], rs1_ref[...], rs2_ref[...]
    for c in range(QK_WIDTH // LANES):
        col = qkn[:, c * LANES:(c + 1) * LANES]
        rot = col * rc + pltpu.roll(col, LANES - ROPE_HALF, 1) * rs1 + pltpu.roll(col, ROPE_HALF, 1) * rs2
        za_ref[:, c * LANES:(c + 1) * LANES] = rot.astype(BF16)
    za_ref[:, QK_WIDTH:QKV_WIDTH] = z_a[:, QK_WIDTH:QKV_WIDTH].astype(BF16)
    zr_ref[...] = _dot(xn, w_ref[:, QKV_WIDTH:QKV_WIDTH + 3 * REC_WIDTH]).astype(BF16)
    zf_ref[...] = _dot(xn, w_ref[:, QKV_WIDTH + 3 * REC_WIDTH:IN_WIDTH])


def _mix_in(h, mix_in_params, layer):
    T = h.shape[0]
    tm = min(TOKEN_TILE, T)
    row = lambda i: (i, 0)
    stacked = lambda i: (layer, 0, 0)
    return pl.pallas_call(
        _mix_in_kernel,
        grid=(T // tm,),
        in_specs=[
            pl.BlockSpec((tm, D_MODEL), row),
            pl.BlockSpec((None, 1, D_MODEL), stacked),
            pl.BlockSpec((None, D_MODEL, IN_WIDTH), stacked),
            pl.BlockSpec((tm, LANES), row),
            pl.BlockSpec((tm, LANES), row),
            pl.BlockSpec((tm, LANES), row),
            pl.BlockSpec((None, 1, QK_WIDTH), stacked),
            pl.BlockSpec((QK_WIDTH, QK_WIDTH), lambda i: (0, 0)),
        ],
        out_specs=[
            pl.BlockSpec((tm, QKV_WIDTH), row),
            pl.BlockSpec((tm, 3 * REC_WIDTH), row),
            pl.BlockSpec((tm, REC_WIDTH), row),
        ],
        out_shape=[
            jax.ShapeDtypeStruct((T, QKV_WIDTH), BF16),
            jax.ShapeDtypeStruct((T, 3 * REC_WIDTH), BF16),
            jax.ShapeDtypeStruct((T, REC_WIDTH), F32),
        ],
        compiler_params=pltpu.CompilerParams(dimension_semantics=("parallel",)),
        name="mix_in",
    )(h, *mix_in_params)


def _attn_kernel(sink_ref, q_ref, kvc_ref, kvp_ref, o_ref):
    n = pl.program_id(1)
    rows = lax.broadcasted_iota(I32, (ATTN_GROUP * WINDOW, 2 * WINDOW), 0)
    cols = lax.broadcasted_iota(I32, (ATTN_GROUP * WINDOW, 2 * WINDOW), 1)
    qi = rows & (WINDOW - 1)
    in_window = (cols > qi) & (cols <= qi + WINDOW)
    grp = lax.broadcasted_iota(I32, (ATTN_GROUP * WINDOW, 1), 0) // WINDOW
    low_lanes = lax.broadcasted_iota(I32, (1, LANES), 1) < ATTN_HEAD_DIM
    keep_low = jnp.where(low_lanes, 1.0, 0.0).astype(BF16)
    keep_high = jnp.where(low_lanes, 0.0, 1.0).astype(BF16)
    out_low = lax.broadcasted_iota(I32, (WINDOW, LANES), 1) < ATTN_HEAD_DIM
    ones = jnp.ones((2 * WINDOW, LANES), BF16)
    for qb in range(q_ref.shape[0] // WINDOW):
        blk = slice(qb * WINDOW, (qb + 1) * WINDOW)
        kvc = kvc_ref[blk, :]
        kvp = kvp_ref[...] if qb == 0 else kvc_ref[(qb - 1) * WINDOW:qb * WINDOW, :]
        valid = in_window & ((cols >= WINDOW) | (n > 0)) if qb == 0 else in_window
        for j in range(ATTN_KV_HEADS):
            kcols = slice(j * LANES, (j + 1) * LANES)
            vcols = slice(KV_COLS + j * LANES, KV_COLS + (j + 1) * LANES)
            kk = jnp.concatenate([kvp[:, kcols], kvc[:, kcols]], axis=0)
            vv = jnp.concatenate([kvp[:, vcols], kvc[:, vcols]], axis=0)
            pairs = jnp.concatenate([q_ref[blk, (2 * j) * LANES:(2 * j + 1) * LANES],
                                     q_ref[blk, (2 * j + 1) * LANES:(2 * j + 2) * LANES]], axis=0)
            heads = [4 * j, 4 * j + 2, 4 * j + 1, 4 * j + 3]
            s = jnp.concatenate([_dot_nt(pairs, kk * keep_low), _dot_nt(pairs, kk * keep_high)], axis=0)
            s = jnp.where(valid, s, MASK_VALUE)
            sink = jnp.zeros((ATTN_GROUP * WINDOW, 1), F32)
            for g, hh in enumerate(heads):
                sink = jnp.where(grp == g, sink_ref[hh], sink)
            m = jnp.maximum(jnp.max(s, axis=-1, keepdims=True), sink)
            e = jnp.exp(s - m).astype(BF16)
            o2 = _dot(e, jnp.concatenate([vv, ones], axis=1))
            on = o2[:, :LANES] * (1.0 / (o2[:, LANES:] + jnp.exp(sink - m)))
            for c in range(2):
                pair = jnp.where(out_low, on[c * WINDOW:(c + 1) * WINDOW], on[(2 + c) * WINDOW:(3 + c) * WINDOW])
                o_ref[blk, (2 * j + c) * LANES:(2 * j + c + 1) * LANES] = pair.astype(BF16)


def _attention(za, sinks, B, S):
    nq = ATTN_STEP_BLOCKS if (S // WINDOW) % ATTN_STEP_BLOCKS == 0 else 1
    ns = S // (nq * WINDOW)
    kvblk = ATTN_WIDTH // (2 * KV_COLS)
    return pl.pallas_call(
        _attn_kernel,
        grid=(B, ns),
        in_specs=[
            pl.BlockSpec(memory_space=pltpu.SMEM),
            pl.BlockSpec((nq * WINDOW, ATTN_WIDTH), lambda b, n: (b * ns + n, 0)),
            pl.BlockSpec((nq * WINDOW, 2 * KV_COLS), lambda b, n: (b * ns + n, kvblk)),
            pl.BlockSpec((WINDOW, 2 * KV_COLS), lambda b, n: ((b * ns + n) * nq - jnp.minimum(n, 1), kvblk)),
        ],
        out_specs=pl.BlockSpec((nq * WINDOW, ATTN_WIDTH), lambda b, n: (b * ns + n, 0)),
        out_shape=jax.ShapeDtypeStruct((B * S, ATTN_WIDTH), BF16),
        compiler_params=pltpu.CompilerParams(dimension_semantics=("parallel", "arbitrary")),
        name="swa_attention",
    )(sinks, za, za, za)


def _hgrn_levels(C):
    out, s = [], C // 2
    while s >= 1:
        out.append(s)
        s //= 2
    return out


def _hgrn_constants(C):
    r = np.arange(C)[:, None]
    u = np.arange(C)[None, :]
    mats = [u <= r]
    masks = []
    for s in _hgrn_levels(C):
        mid = (r // (2 * s)) * 2 * s + s
        if s < HGRN_BROADCAST_MIN:
            mats.append(np.where(r >= mid, (u >= mid) & (u <= r), (u > r) & (u < mid)))
        masks.append(((r // (2 * s)) == (u // (2 * s))) & ((r & s) != 0) & ((u & s) == 0))
    masks.append(r == u)
    return (np.concatenate(mats, axis=0).astype(np.float32), np.stack(masks).astype(np.float32))


def _hgrn_head(hd, zr_ref, zf_ref, lb_ref, gain_ref, msum_ref, pmask_ref, o_ref, st_ref):
    C = zf_ref.shape[0]
    levels = _hgrn_levels(C)
    cs = slice(hd * REC_DIM, (hd + 1) * REC_DIM)
    zq = zr_ref[:, cs].astype(F32)
    v = zr_ref[:, REC_WIDTH + hd * REC_DIM:REC_WIDTH + (hd + 1) * REC_DIM]
    og = zr_ref[:, 2 * REC_WIDTH + hd * REC_DIM:2 * REC_WIDTH + (hd + 1) * REC_DIM].astype(F32)
    z = zf_ref[:, cs]
    lb = lb_ref[:, cs]
    qp = zq * _sigmoid(zq)
    a = jnp.exp(-jnp.abs(z))
    r = 1.0 / (1.0 + a)
    pos = z >= 0
    g = jnp.log2(lb + (1.0 - lb) * jnp.where(pos, r, a * r))
    k = (1.0 - lb) * jnp.where(pos, a * r, r)
    g_hi = g.astype(BF16)
    g_lo = (g - g_hi.astype(F32)).astype(BF16)
    x2 = _dot(msum_ref[...], jnp.concatenate([g_hi, g_lo], axis=1))
    sums = x2[:, :REC_DIM] + x2[:, REC_DIM:]
    b = sums[0:C]
    b_last = b[C - 1:C, :]
    e_b = jnp.exp2(b)
    st = st_ref[...]
    o = _dot_nt((qp * e_b).astype(BF16), st.astype(BF16))
    amat = pmask_ref[len(levels)] * _dot_nt(qp.astype(BF16), k.astype(BF16))
    n_matrix_levels = 0
    for li, s in enumerate(levels):
        if s >= HGRN_BROADCAST_MIN:
            ref_rows = [jnp.broadcast_to(b[j + s - 1:j + s, :], (2 * s, REC_DIM)) for j in range(0, C, 2 * s)]
            expo = -jnp.abs(b - jnp.concatenate(ref_rows, axis=0))
        else:
            n_matrix_levels += 1
            expo = sums[n_matrix_levels * C:(n_matrix_levels + 1) * C]
        e_l = jnp.exp2(expo)
        amat = amat + pmask_ref[li] * _dot_nt((qp * e_l).astype(BF16), (k * e_l).astype(BF16))
    o = o + _dot(amat.astype(BF16), v)
    st_ref[...] = st * e_b[C - 1:C, :] + _dot_tn(v, (k * jnp.exp2(b_last - b)).astype(BF16))
    og_act = og * _sigmoid(og)
    o_ref[:, cs] = (_rms(o, gain_ref[...]) * og_act).astype(BF16)


def _hgrn_kernel(zr_ref, zf_ref, lb_ref, gain_ref, msum_ref, pmask_ref, o_ref, st_ref):
    @pl.when(pl.program_id(1) == 0)
    def _():
        st_ref[...] = jnp.zeros_like(st_ref)

    for bb in range(zf_ref.shape[0]):
        for hd in range(REC_HEADS):
            _hgrn_head(hd, zr_ref.at[bb], zf_ref.at[bb], lb_ref, gain_ref, msum_ref, pmask_ref, o_ref.at[bb],
                       st_ref.at[bb * REC_HEADS + hd])


def _hgrn2(zr, zf, lb, gain, msum, pmask, B, S):
    C = min(REC_CHUNK, S)
    nb = HGRN_BATCH_ROWS if B % HGRN_BATCH_ROWS == 0 else 1
    blk = lambda b, c: (b, c, 0)
    out = pl.pallas_call(
        _hgrn_kernel,
        grid=(B // nb, S // C),
        in_specs=[
            pl.BlockSpec((nb, C, 3 * REC_WIDTH), blk),
            pl.BlockSpec((nb, C, REC_WIDTH), blk),
            pl.BlockSpec((1, REC_WIDTH), lambda b, c: (0, 0)),
            pl.BlockSpec((1, REC_DIM), lambda b, c: (0, 0)),
            pl.BlockSpec(msum.shape, lambda b, c: (0, 0)),
            pl.BlockSpec(pmask.shape, lambda b, c: (0, 0, 0)),
        ],
        out_specs=pl.BlockSpec((nb, C, REC_WIDTH), blk),
        out_shape=jax.ShapeDtypeStruct((B, S, REC_WIDTH), BF16),
        scratch_shapes=[pltpu.VMEM((nb * REC_HEADS, REC_DIM, REC_DIM), F32)],
        compiler_params=pltpu.CompilerParams(dimension_semantics=("parallel", "arbitrary")),
        name="hgrn2",
    )(zr.reshape(B, S, 3 * REC_WIDTH), zf.reshape(B, S, REC_WIDTH), lb, gain, msum, pmask)
    return out.reshape(B * S, REC_WIDTH)


def _mix_out_kernel(attn_ref, rec_ref, h_ref, wo_ref, gain_ref, wr_ref, tril_ref,
                    h1_ref, xn_ref, ri_ref, rw_ref, cnt_ref, carry_ref):
    @pl.when(pl.program_id(0) == 0)
    def _():
        carry_ref[...] = jnp.zeros_like(carry_ref)

    sub = tril_ref.shape[0]
    carry = carry_ref[...]
    for s in range(h_ref.shape[0] // sub):
        carry = _mix_out_rows(slice(s * sub, (s + 1) * sub), s * sub, carry, attn_ref, rec_ref, h_ref, wo_ref,
                              gain_ref, wr_ref, tril_ref, h1_ref, xn_ref, ri_ref, rw_ref)
    carry_ref[...] = carry
    cnt_ref[...] = jnp.broadcast_to(carry, cnt_ref.shape)


def _mix_out_rows(rows, first, carry, attn_ref, rec_ref, h_ref, wo_ref, gain_ref, wr_ref, tril_ref,
                  h1_ref, xn_ref, ri_ref, rw_ref):
    mixed = jnp.concatenate([attn_ref[rows, :], rec_ref[rows, :]], axis=1)
    h1 = h_ref[rows, :] + _dot(mixed, wo_ref[...])
    h1_ref[rows, :] = h1
    xn = _rms(h1, gain_ref[...])
    _store_row_tiles(xn_ref, xn, first=first)
    xh = xn.astype(BF16)
    xl = (xn - xh.astype(F32)).astype(BF16)
    both = _dot(xh, wr_ref[...])
    logits = both[:, :ROUTE_LANES] + both[:, ROUTE_LANES:] + _dot(xl, wr_ref[:, :ROUTE_LANES])
    lane = lax.broadcasted_iota(I32, logits.shape, 1)
    lanef = lane.astype(F32)
    neg = jnp.float32(-jnp.inf)
    big = jnp.float32(1e9)
    gl = jnp.where(lane < N_GROUPS, logits, neg)
    gmax = jnp.max(gl, axis=-1, keepdims=True)
    gidx = jnp.min(jnp.where(gl == gmax, lanef, big), axis=-1, keepdims=True)
    p_group = 1.0 / jnp.sum(jnp.where(lane < N_GROUPS, jnp.exp(logits - gmax), 0.0), axis=-1, keepdims=True)
    lo = N_GROUPS + gidx * EXPERTS_PER_GROUP
    el = jnp.where((lanef >= lo) & (lanef < lo + EXPERTS_PER_GROUP), logits, neg)
    t1 = jnp.max(el, axis=-1, keepdims=True)
    i1 = jnp.min(jnp.where(el == t1, lanef, big), axis=-1, keepdims=True)
    el2 = jnp.where(lanef == i1, neg, el)
    t2 = jnp.max(el2, axis=-1, keepdims=True)
    i2 = jnp.min(jnp.where(el2 == t2, lanef, big), axis=-1, keepdims=True)
    r21 = jnp.exp(t2 - t1)
    w1 = p_group / (1.0 + r21)
    w2 = w1 * r21
    e1 = i1 - N_GROUPS
    e2 = i2 - N_GROUPS
    oh1 = (lanef == e1).astype(F32)
    oh2 = (lanef == e2).astype(F32)
    oh = oh1 + oh2
    prefix = _dot(tril_ref[...], oh.astype(BF16)) + carry
    rank1 = jnp.sum(prefix * oh1, axis=-1, keepdims=True)
    rank2 = jnp.sum(prefix * oh2, axis=-1, keepdims=True)
    ri = jnp.where(lane == 0, e1, jnp.where(lane == 1, e2, jnp.where(lane == 2, rank1, jnp.where(lane == 3, rank2, 0.0))))
    ri_ref[rows, :] = ri.astype(I32)
    rw_ref[rows, :] = jnp.where(lane == 0, w1, jnp.where(lane == 1, w2, 0.0))
    return carry + jnp.sum(oh, axis=0, keepdims=True)


def _mix_out(attn, rec, h, wo, gain, wr, tril, layer):
    T = h.shape[0]
    tm = min(TOKEN_TILE, T)
    row = lambda i: (i, 0)
    fixed = lambda i: (0, 0)
    stacked = lambda i: (layer, 0, 0)
    return pl.pallas_call(
        _mix_out_kernel,
        grid=(T // tm,),
        in_specs=[
            pl.BlockSpec((tm, ATTN_WIDTH), row),
            pl.BlockSpec((tm, REC_WIDTH), row),
            pl.BlockSpec((tm, D_MODEL), row),
            pl.BlockSpec((None, ATTN_WIDTH + REC_WIDTH, D_MODEL), stacked),
            pl.BlockSpec((None, 1, D_MODEL), stacked),
            pl.BlockSpec((None, D_MODEL, 2 * ROUTE_LANES), stacked),
            pl.BlockSpec(tril.shape, fixed),
        ],
        out_specs=[
            pl.BlockSpec((tm, D_MODEL), row),
            pl.BlockSpec((tm * ROW_SUBLANES, LANES), row),
            pl.BlockSpec((tm, ROUTE_LANES), row),
            pl.BlockSpec((tm, ROUTE_LANES), row),
            pl.BlockSpec((8, ROUTE_LANES), fixed),
        ],
        out_shape=[
            jax.ShapeDtypeStruct((T, D_MODEL), F32),
            jax.ShapeDtypeStruct((T * ROW_SUBLANES, LANES), U32),
            jax.ShapeDtypeStruct((T, ROUTE_LANES), I32),
            jax.ShapeDtypeStruct((T, ROUTE_LANES), F32),
            jax.ShapeDtypeStruct((8, ROUTE_LANES), F32),
        ],
        scratch_shapes=[pltpu.VMEM((1, ROUTE_LANES), F32)],
        compiler_params=pltpu.CompilerParams(dimension_semantics=("arbitrary",)),
        name="mix_out_router",
    )(attn, rec, h, wo, gain, wr, tril)


def _row_move_loops(copy, tokens):
    def start(t, carry):
        copy(t, 0).start()
        copy(t, 1).start()
        return carry

    def wait(t, carry):
        copy(t, 0).wait()
        copy(t, 1).wait()
        return carry

    lax.fori_loop(0, tokens, start, 0, unroll=COPY_UNROLL)
    lax.fori_loop(0, tokens, wait, 0, unroll=COPY_UNROLL)


def _dispatch_kernel(dest_ref, bound_ref, x_ref, out_ref, zero_ref, sem, *, tokens):
    @pl.when(pl.program_id(0) == 0)
    def _():
        zero_ref[...] = jnp.zeros_like(zero_ref)

        def fill(e):
            return pltpu.make_async_copy(zero_ref, out_ref.at[pl.ds(bound_ref[e + 1] - EXPERT_BLOCK, EXPERT_BLOCK)], sem)

        for e in range(N_EXPERTS):
            pl.when(bound_ref[e + 1] > bound_ref[e])(lambda e=e: fill(e).start())
        for e in range(N_EXPERTS):
            pl.when(bound_ref[e + 1] > bound_ref[e])(lambda e=e: fill(e).wait())

        def tail(i):
            return pltpu.make_async_copy(zero_ref, out_ref.at[pl.ds(i * EXPERT_BLOCK, EXPERT_BLOCK)], sem)

        first_unused = bound_ref[N_EXPERTS] // EXPERT_BLOCK
        n_blocks = out_ref.shape[0] // EXPERT_BLOCK
        lax.fori_loop(first_unused, n_blocks, lambda i, c: (tail(i).start(), c)[1], 0)
        lax.fori_loop(first_unused, n_blocks, lambda i, c: (tail(i).wait(), c)[1], 0)

    base = 2 * pl.program_id(0) * tokens

    def copy(t, j):
        return pltpu.make_async_copy(x_ref.at[t], out_ref.at[dest_ref[base + 2 * t + j]], sem)

    _row_move_loops(copy, tokens)


def _dispatch(dest, bounds, x3, n_rows):
    tokens = min(COPY_TOKENS, x3.shape[0])
    tile = x3.shape[1:]
    return pl.pallas_call(
        functools.partial(_dispatch_kernel, tokens=tokens),
        grid_spec=pltpu.PrefetchScalarGridSpec(
            num_scalar_prefetch=2,
            grid=(x3.shape[0] // tokens,),
            in_specs=[pl.BlockSpec((tokens,) + tile, lambda i, d, b: (i, 0, 0))],
            out_specs=pl.BlockSpec(memory_space=pl.ANY),
            scratch_shapes=[pltpu.VMEM((EXPERT_BLOCK,) + tile, x3.dtype), pltpu.SemaphoreType.DMA],
        ),
        out_shape=jax.ShapeDtypeStruct((n_rows,) + tile, x3.dtype),
        compiler_params=pltpu.CompilerParams(dimension_semantics=("arbitrary",)),
        name="moe_dispatch_rows",
    )(dest, bounds, x3)


def _expert_kernel(be_ref, sched_ref, x_ref, wg_ref, wu_ref, wd_ref, y_ref,
                   wgf_ref, wuf_ref, wdf_ref, wgb_ref, wub_ref, wdb_ref, sem, *, layer):
    i = pl.program_id(0)
    nblk = pl.num_programs(0)
    used = i < sched_ref[0]
    buf = sched_ref[1 + i]
    next_expert = sched_ref[1 + nblk + i]
    first_of_run = (i == 0) | (be_ref[i] != be_ref[jnp.maximum(i - 1, 0)])

    def fetch(expert, slot):
        e = layer * N_EXPERTS + expert
        return [pltpu.make_async_copy(src.at[e], dst.at[slot], sem.at[slot, n])
                for n, (src, dst) in enumerate(((wg_ref, wgf_ref), (wu_ref, wuf_ref), (wd_ref, wdf_ref)))]

    @pl.when(i == 0)
    def _():
        for c in fetch(be_ref[0], buf):
            c.start()

    @pl.when(used & first_of_run & (next_expert >= 0))
    def _():
        for c in fetch(next_expert, 1 - buf):
            c.start()

    @pl.when(used & first_of_run)
    def _():
        for c in fetch(be_ref[i], buf):
            c.wait()
        wgb_ref[...] = wgf_ref[buf].astype(BF16)
        wub_ref[...] = wuf_ref[buf].astype(BF16)
        wdb_ref[...] = wdf_ref[buf].astype(BF16)

    @pl.when(used)
    def _():
        for s in range(EXPERT_BLOCK // EXPERT_SUB):
            x = _load_row_tiles(x_ref, EXPERT_SUB, first=s * EXPERT_SUB).astype(BF16)
            gate = _dot(x, wgb_ref[...])
            up = _dot(x, wub_ref[...])
            hidden = (gate * _sigmoid(gate) * up).astype(BF16)
            _store_row_tiles(y_ref, _dot(hidden, wdb_ref[...]), first=s * EXPERT_SUB)

    @pl.when(jnp.logical_not(used))
    def _():
        y_ref[...] = jnp.zeros_like(y_ref)


def _experts(block_expert, sched, xs, wg, wu, wd, layer):
    blk = EXPERT_BLOCK * ROW_SUBLANES
    nblk = xs.shape[0] // blk
    hbm = pl.BlockSpec(memory_space=pl.ANY)
    return pl.pallas_call(
        functools.partial(_expert_kernel, layer=layer),
        grid_spec=pltpu.PrefetchScalarGridSpec(
            num_scalar_prefetch=2,
            grid=(nblk,),
            in_specs=[pl.BlockSpec((blk, LANES), lambda i, be, sc: (jnp.minimum(i, sc[0] - 1), 0)), hbm, hbm, hbm],
            out_specs=pl.BlockSpec((blk, LANES), lambda i, be, sc: (i, 0)),
            scratch_shapes=[
                pltpu.VMEM((2, D_MODEL, EXPERT_FF), F32),
                pltpu.VMEM((2, D_MODEL, EXPERT_FF), F32),
                pltpu.VMEM((2, EXPERT_FF, D_MODEL), F32),
                pltpu.VMEM((D_MODEL, EXPERT_FF), BF16),
                pltpu.VMEM((D_MODEL, EXPERT_FF), BF16),
                pltpu.VMEM((EXPERT_FF, D_MODEL), BF16),
                pltpu.SemaphoreType.DMA((2, 3)),
            ],
        ),
        out_shape=jax.ShapeDtypeStruct(xs.shape, xs.dtype),
        compiler_params=pltpu.CompilerParams(dimension_semantics=("arbitrary",)),
        name="moe_experts",
    )(block_expert, sched, xs, wg, wu, wd)


def _ple_kernel(dest_ref, h1_ref, ys_ref, rw_ref, p_ref, wple_ref, pgain_ref, ggain_ref, wpg_ref, o_ref,
                ybuf_ref, sem):
    i = pl.program_id(0)
    tm = h1_ref.shape[0]
    slot = i % 2

    def row_copy(tile, buf, t, j):
        a = 2 * t + j
        dst = ybuf_ref.at[buf, pl.ds(pl.multiple_of(a * ROW_SUBLANES, ROW_SUBLANES), ROW_SUBLANES), :]
        return pltpu.make_async_copy(ys_ref.at[dest_ref[2 * tile * tm + a]], dst, sem.at[buf])

    def start_tile(tile, buf):
        def body(t, carry):
            row_copy(tile, buf, t, 0).start()
            row_copy(tile, buf, t, 1).start()
            return carry
        lax.fori_loop(0, tm, body, 0, unroll=COPY_UNROLL)

    def wait_tile(tile, buf):
        def body(t, carry):
            row_copy(tile, buf, t, 0).wait()
            row_copy(tile, buf, t, 1).wait()
            return carry
        lax.fori_loop(0, tm, body, 0, unroll=COPY_UNROLL)

    n_tiles = pl.num_programs(0)
    pl.when(i == 0)(lambda: start_tile(0, 0))
    wait_tile(i, slot)
    nxt = jnp.minimum(i + 1, n_tiles - 1)

    yg_ref = ybuf_ref.at[slot]
    sub = min(SUB_TILE, tm)
    for s in range(tm // sub):
        rows = slice(s * sub, (s + 1) * sub)
        rw = rw_ref[rows, :]
        y1 = _load_row_tiles(yg_ref, sub, first=2 * s * sub, every=2)
        y2 = _load_row_tiles(yg_ref, sub, first=2 * s * sub + 1, every=2)
        h2 = h1_ref[rows, :] + rw[:, 0:1] * y1 + rw[:, 1:2] * y2
        ple = _rms(_dot(p_ref[rows, :].astype(BF16), wple_ref[...]), pgain_ref[...])
        gate = _sigmoid(_dot(_rms(h2, ggain_ref[...]).astype(BF16), wpg_ref[...]))
        o_ref[rows, :] = h2 + ple * gate
        for t in range(s * sub, (s + 1) * sub):
            row_copy(nxt, 1 - slot, t, 0).start()
            row_copy(nxt, 1 - slot, t, 1).start()

    pl.when(i == n_tiles - 1)(lambda: wait_tile(i, 1 - slot))


def _ple(dest, h1, ys3, rw, p, ple_params, layer):
    T = h1.shape[0]
    tm = min(TOKEN_TILE, T)
    nt = T // tm
    row = lambda i, d: (i, 0)
    stacked = lambda i, d: (layer, 0, 0)
    return pl.pallas_call(
        _ple_kernel,
        grid_spec=pltpu.PrefetchScalarGridSpec(
            num_scalar_prefetch=1,
            grid=(nt,),
            in_specs=[
                pl.BlockSpec((tm, D_MODEL), row),
                pl.BlockSpec(memory_space=pl.ANY),
                pl.BlockSpec((tm, ROUTE_LANES), row),
                pl.BlockSpec((tm, PLE_DIM), lambda i, d: (layer * nt + i, 0)),
                pl.BlockSpec((None, PLE_DIM, D_MODEL), stacked),
                pl.BlockSpec((None, 1, D_MODEL), stacked),
                pl.BlockSpec((None, 1, D_MODEL), stacked),
                pl.BlockSpec((None, D_MODEL, D_MODEL), stacked),
            ],
            out_specs=pl.BlockSpec((tm, D_MODEL), row),
            scratch_shapes=[pltpu.VMEM((2, 2 * tm * ROW_SUBLANES, LANES), ys3.dtype), pltpu.SemaphoreType.DMA((2,))],
        ),
        out_shape=jax.ShapeDtypeStruct((T, D_MODEL), F32),
        compiler_params=pltpu.CompilerParams(dimension_semantics=("arbitrary",)),
        name="combine_ple",
    )(dest, h1, ys3, rw, p, *ple_params)


def _rope_tables(positions):
    inv_freq = ROPE_THETA ** (-jnp.arange(0, ROPE_DIM, 2, dtype=F32) / ROPE_DIM)
    ang = positions.astype(F32).reshape(-1, 1) * inv_freq
    cos, sin = jnp.cos(ang), jnp.sin(ang)
    T = ang.shape[0]
    rest = jnp.zeros((T, ATTN_HEAD_DIM - ROPE_DIM), F32)
    zero = jnp.zeros((T, ROPE_HALF), F32)
    rc = jnp.concatenate([cos, cos, rest + 1.0], axis=1)
    rs1 = jnp.concatenate([-sin, zero, rest], axis=1)
    rs2 = jnp.concatenate([zero, sin, rest], axis=1)
    reps = LANES // ATTN_HEAD_DIM
    return tuple(jnp.tile(t, (1, reps)) for t in (rc, rs1, rs2))


def kernel(x, p, positions, mix_norm, w_in, q_norm, k_norm, sinks, lb_logits, rec_norm, w_out, ffn_norm,
           w_router_group, w_router_expert, w_gate, w_up, w_down, w_ple, ple_norm, ple_gate_norm, w_ple_gate):
    B, S, D = x.shape
    depth = w_in.shape[0]
    T = B * S
    n_assign = 2 * T
    assert D == D_MODEL and S % WINDOW == 0 and T % min(TOKEN_TILE, T) == 0

    rc, rs1, rs2 = _rope_tables(positions)
    lb_sm = jax.nn.softmax(lb_logits.astype(F32), axis=0)
    lower_bounds = jnp.cumsum(lb_sm, axis=0) - lb_sm[0:1]

    def twice(cols):
        heads = cols.reshape(depth, D_MODEL, ATTN_KV_HEADS, 1, ATTN_HEAD_DIM)
        return jnp.broadcast_to(heads, (depth, D_MODEL, ATTN_KV_HEADS, KV_DUP, ATTN_HEAD_DIM)).reshape(
            depth, D_MODEL, KV_COLS)

    k0, v0, r0 = ATTN_WIDTH, ATTN_WIDTH + KV_WIDTH, ATTN_WIDTH + 2 * KV_WIDTH
    w_in_p = jnp.concatenate([w_in[:, :, :k0], twice(w_in[:, :, k0:v0]), twice(w_in[:, :, v0:r0]),
                              w_in[:, :, r0:r0 + REC_WIDTH], w_in[:, :, r0 + 2 * REC_WIDTH:],
                              w_in[:, :, r0 + REC_WIDTH:r0 + 2 * REC_WIDTH]], axis=2).astype(BF16)
    w_out_b = w_out.astype(BF16)
    wg_all = w_gate.reshape(depth * N_EXPERTS, D_MODEL, EXPERT_FF)
    wu_all = w_up.reshape(depth * N_EXPERTS, D_MODEL, EXPERT_FF)
    wd_all = w_down.reshape(depth * N_EXPERTS, EXPERT_FF, D_MODEL)
    w_ple_b = w_ple.astype(BF16)
    w_pg_b = w_ple_gate.astype(BF16)
    w_r = jnp.concatenate([w_router_group, w_router_expert,
                           jnp.zeros((depth, D_MODEL, ROUTE_LANES - N_GROUPS - N_EXPERTS), F32)], axis=2)
    w_r_hi = w_r.astype(BF16)
    w_r2 = jnp.concatenate([w_r_hi, (w_r - w_r_hi.astype(F32)).astype(BF16)], axis=2)
    qk_gain = jnp.concatenate([jnp.tile(q_norm, (1, ATTN_HEADS)) * (ATTN_HEAD_DIM ** -0.5),
                               jnp.tile(k_norm, (1, ATTN_KV_HEADS * KV_DUP))], axis=1)
    seg_id = np.arange(QK_WIDTH) // ATTN_HEAD_DIM
    seg = jnp.asarray(seg_id[:, None] == seg_id[None, :], BF16)
    msum_np, pmask_np = _hgrn_constants(min(REC_CHUNK, S))
    msum = jnp.asarray(msum_np, BF16)
    pmask = jnp.asarray(pmask_np, F32)
    sub = min(SUB_TILE, T)
    tril = jnp.asarray(np.tril(np.ones((sub, sub), np.float32), -1), BF16)

    n_rows = n_assign + N_EXPERTS * EXPERT_BLOCK
    nblk = n_rows // EXPERT_BLOCK
    eids = jnp.arange(N_EXPERTS, dtype=I32)
    p2 = p.reshape(depth * T, PLE_DIM)

    mix_in_params = (mix_norm[:, None, :], w_in_p, rc, rs1, rs2, qk_gain[:, None, :], seg)
    ple_params = (w_ple_b, ple_norm[:, None, :], ple_gate_norm[:, None, :], w_pg_b)
    ffn_gain = ffn_norm[:, None, :]

    h = x.reshape(T, D)
    for l in range(depth):
        za, zr, zf = _mix_in(h, mix_in_params, l)
        attn = _attention(za, sinks[l], B, S)
        rec = _hgrn2(zr, zf, lower_bounds[l][None], rec_norm[l][None], msum, pmask, B, S)
        h1, xn, ri, rw, cnt = _mix_out(attn, rec, h, w_out_b, ffn_gain, w_r2, tril, l)
        counts = cnt[0, :N_EXPERTS].astype(I32)
        padded = ((counts + EXPERT_BLOCK - 1) // EXPERT_BLOCK) * EXPERT_BLOCK
        pad_end = jnp.cumsum(padded)
        pad_start = pad_end - padded
        expert = ri[:, 0:2]
        dest = ri[:, 2:4] + jnp.sum(jnp.where(expert[:, :, None] == eids, pad_start, 0), axis=-1)
        dest = dest.reshape(n_assign).astype(I32)
        n_used = (pad_end[-1] // EXPERT_BLOCK).astype(I32).reshape(1)
        blk_start = jnp.arange(nblk, dtype=I32) * EXPERT_BLOCK
        block_expert = jnp.minimum(jnp.sum(pad_end[None, :] <= blk_start[:, None], axis=1), N_EXPERTS - 1).astype(I32)
        bounds = jnp.concatenate([jnp.zeros((1,), I32), pad_end.astype(I32)])
        xs = _dispatch(dest, bounds, xn.reshape(T, ROW_SUBLANES, LANES), n_rows)
        run_start = jnp.concatenate([jnp.ones((1,), I32), (block_expert[1:] != block_expert[:-1]).astype(I32)])
        run_buf = (jnp.cumsum(run_start) - 1) % 2
        later = jnp.where(padded > 0, eids, N_EXPERTS)
        next_used = jnp.concatenate([lax.cummin(later[::-1])[::-1][1:], jnp.full((1,), N_EXPERTS, I32)])
        next_used = jnp.where(next_used < N_EXPERTS, next_used, -1)
        sched = jnp.concatenate([n_used, run_buf, next_used[block_expert]]).astype(I32)
        ys = _experts(block_expert, sched, xs.reshape(n_rows * ROW_SUBLANES, LANES), wg_all, wu_all, wd_all, l)
        h = _ple(dest, h1, ys.reshape(n_rows, ROW_SUBLANES, LANES), rw, p2, ple_params, l)
    return h.reshape(B, S, D)
```

```python
import functools

import jax
import jax.numpy as jnp
import numpy as np
from jax import lax
from jax.experimental import pallas as pl
from jax.experimental.pallas import tpu as pltpu

F32 = jnp.float32
BF16 = jnp.bfloat16
I32 = jnp.int32
U32 = jnp.uint32

D_MODEL = 1024
ATTN_HEADS = 8
ATTN_KV_HEADS = 2
ATTN_HEAD_DIM = 64
ATTN_GROUP = ATTN_HEADS // ATTN_KV_HEADS
ATTN_WIDTH = ATTN_HEADS * ATTN_HEAD_DIM
KV_WIDTH = ATTN_KV_HEADS * ATTN_HEAD_DIM
KV_DUP = 2
KV_COLS = KV_DUP * KV_WIDTH
QK_WIDTH = ATTN_WIDTH + KV_COLS
QKV_WIDTH = ATTN_WIDTH + 2 * KV_COLS
WINDOW = 128
ATTN_STEP_BLOCKS = 8
ROPE_THETA = 500000.0
ROPE_DIM = ATTN_HEAD_DIM // 4
ROPE_HALF = ROPE_DIM // 2
REC_HEADS = 4
REC_DIM = 128
REC_WIDTH = REC_HEADS * REC_DIM
REC_CHUNK = 128
HGRN_BROADCAST_MIN = 8
HGRN_BATCH_ROWS = 4
IN_WIDTH = QKV_WIDTH + 4 * REC_WIDTH
N_GROUPS = 4
EXPERTS_PER_GROUP = 8
N_EXPERTS = N_GROUPS * EXPERTS_PER_GROUP
EXPERT_FF = 512
PLE_DIM = 256
RMS_EPS = 1e-6
MASK_VALUE = -1e30
LANES = 128
ROUTE_LANES = LANES
ROW_WORDS = D_MODEL // 2
ROW_SUBLANES = ROW_WORDS // LANES
HIGH_HALF = np.uint32(0xFFFF0000)

TOKEN_TILE = 512
SUB_TILE = 256
EXPERT_BLOCK = 512
EXPERT_SUB = 512
COPY_TOKENS = 512
COPY_UNROLL = 8


def _dot(a, b):
    return jnp.dot(a, b, preferred_element_type=F32)


def _dot_nt(a, b):
    return lax.dot_general(a, b, (((1,), (1,)), ((), ())), preferred_element_type=F32)


def _dot_tn(a, b):
    return lax.dot_general(a, b, (((0,), (0,)), ((), ())), preferred_element_type=F32)


def _store_row_tiles(ref, x, first=0):
    m = x.shape[0]
    lo = lax.bitcast_convert_type(x[:, :ROW_WORDS].astype(BF16).astype(F32), U32) >> 16
    hi = lax.bitcast_convert_type(x[:, ROW_WORDS:].astype(BF16).astype(F32), U32) & HIGH_HALF
    words = lo | hi
    for c in range(ROW_SUBLANES):
        ref[pl.ds(first * ROW_SUBLANES + c, m, stride=ROW_SUBLANES), :] = words[:, c * LANES:(c + 1) * LANES]


def _load_row_tiles(ref, m, first=0, every=1):
    words = jnp.concatenate(
        [ref[pl.ds(first * ROW_SUBLANES + c, m, stride=every * ROW_SUBLANES), :] for c in range(ROW_SUBLANES)], axis=1)
    lo = lax.bitcast_convert_type(words << 16, F32)
    hi = lax.bitcast_convert_type(words & HIGH_HALF, F32)
    return jnp.concatenate([lo, hi], axis=1)


def _sigmoid(x):
    return 1.0 / (1.0 + jnp.exp(-x))


def _rms(x, gain):
    ms = jnp.mean(x * x, axis=-1, keepdims=True)
    return x * lax.rsqrt(ms + RMS_EPS) * gain


def _mix_in_kernel(h_ref, gain_ref, w_ref, rc_ref, rs1_ref, rs2_ref, qkg_ref, seg_ref,
                   za_ref, zr_ref, zf_ref):
    xn = _rms(h_ref[...], gain_ref[...]).astype(BF16)
    z_a = _dot(xn, w_ref[:, 0:QKV_WIDTH])
    qk = z_a[:, 0:QK_WIDTH]
    seg = _dot((qk * qk).astype(BF16), seg_ref[...]) * (1.0 / ATTN_HEAD_DIM)
    qkn = qk * lax.rsqrt(seg + RMS_EPS) * qkg_ref[...]
    rc, rs1, rs2 = rc_ref[...], rs1_ref[...], rs2_ref[...]
    for c in range(QK_WIDTH // LANES):
        col = qkn[:, c * LANES:(c + 1) * LANES]
        rot = col * rc + pltpu.roll(col, LANES - ROPE_HALF, 1) * rs1 + pltpu.roll(col, ROPE_HALF, 1) * rs2
        za_ref[:, c * LANES:(c + 1) * LANES] = rot.astype(BF16)
    za_ref[:, QK_WIDTH:QKV_WIDTH] = z_a[:, QK_WIDTH:QKV_WIDTH].astype(BF16)
    zr_ref[...] = _dot(xn, w_ref[:, QKV_WIDTH:QKV_WIDTH + 3 * REC_WIDTH]).astype(BF16)
    zf_ref[...] = _dot(xn, w_ref[:, QKV_WIDTH + 3 * REC_WIDTH:IN_WIDTH])


def _mix_in(h, mix_in_params, layer):
    T = h.shape[0]
    tm = min(TOKEN_TILE, T)
    row = lambda i: (i, 0)
    stacked = lambda i: (layer, 0, 0)
    return pl.pallas_call(
        _mix_in_kernel,
        grid=(T // tm,),
        in_specs=[
            pl.BlockSpec((tm, D_MODEL), row),
            pl.BlockSpec((None, 1, D_MODEL), stacked),
            pl.BlockSpec((None, D_MODEL, IN_WIDTH), stacked),
            pl.BlockSpec((tm, LANES), row),
            pl.BlockSpec((tm, LANES), row),
            pl.BlockSpec((tm, LANES), row),
            pl.BlockSpec((None, 1, QK_WIDTH), stacked),
            pl.BlockSpec((QK_WIDTH, QK_WIDTH), lambda i: (0, 0)),
        ],
        out_specs=[
            pl.BlockSpec((tm, QKV_WIDTH), row),
            pl.BlockSpec((tm, 3 * REC_WIDTH), row),
            pl.BlockSpec((tm, REC_WIDTH), row),
        ],
        out_shape=[
            jax.ShapeDtypeStruct((T, QKV_WIDTH), BF16),
            jax.ShapeDtypeStruct((T, 3 * REC_WIDTH), BF16),
            jax.ShapeDtypeStruct((T, REC_WIDTH), F32),
        ],
        compiler_params=pltpu.CompilerParams(dimension_semantics=("parallel",)),
        name="mix_in",
    )(h, *mix_in_params)


def _attn_kernel(sink_ref, q_ref, kvc_ref, kvp_ref, o_ref):
    n = pl.program_id(1)
    rows = lax.broadcasted_iota(I32, (ATTN_GROUP * WINDOW, 2 * WINDOW), 0)
    cols = lax.broadcasted_iota(I32, (ATTN_GROUP * WINDOW, 2 * WINDOW), 1)
    qi = rows & (WINDOW - 1)
    in_window = (cols > qi) & (cols <= qi + WINDOW)
    grp = lax.broadcasted_iota(I32, (ATTN_GROUP * WINDOW, 1), 0) // WINDOW
    low_lanes = lax.broadcasted_iota(I32, (1, LANES), 1) < ATTN_HEAD_DIM
    keep_low = jnp.where(low_lanes, 1.0, 0.0).astype(BF16)
    keep_high = jnp.where(low_lanes, 0.0, 1.0).astype(BF16)
    out_low = lax.broadcasted_iota(I32, (WINDOW, LANES), 1) < ATTN_HEAD_DIM
    ones = jnp.ones((2 * WINDOW, LANES), BF16)
    for qb in range(q_ref.shape[0] // WINDOW):
        blk = slice(qb * WINDOW, (qb + 1) * WINDOW)
        kvc = kvc_ref[blk, :]
        kvp = kvp_ref[...] if qb == 0 else kvc_ref[(qb - 1) * WINDOW:qb * WINDOW, :]
        valid = in_window & ((cols >= WINDOW) | (n > 0)) if qb == 0 else in_window
        for j in range(ATTN_KV_HEADS):
            kcols = slice(j * LANES, (j + 1) * LANES)
            vcols = slice(KV_COLS + j * LANES, KV_COLS + (j + 1) * LANES)
            kk = jnp.concatenate([kvp[:, kcols], kvc[:, kcols]], axis=0)
            vv = jnp.concatenate([kvp[:, vcols], kvc[:, vcols]], axis=0)
            pairs = jnp.concatenate([q_ref[blk, (2 * j) * LANES:(2 * j + 1) * LANES],
                                     q_ref[blk, (2 * j + 1) * LANES:(2 * j + 2) * LANES]], axis=0)
            heads = [4 * j, 4 * j + 2, 4 * j + 1, 4 * j + 3]
            s = jnp.concatenate([_dot_nt(pairs, kk * keep_low), _dot_nt(pairs, kk * keep_high)], axis=0)
            s = jnp.where(valid, s, MASK_VALUE)
            sink = jnp.zeros((ATTN_GROUP * WINDOW, 1), F32)
            for g, hh in enumerate(heads):
                sink = jnp.where(grp == g, sink_ref[hh], sink)
            m = jnp.maximum(jnp.max(s, axis=-1, keepdims=True), sink)
            e = jnp.exp(s - m).astype(BF16)
            o2 = _dot(e, jnp.concatenate([vv, ones], axis=1))
            on = o2[:, :LANES] * (1.0 / (o2[:, LANES:] + jnp.exp(sink - m)))
            for c in range(2):
                pair = jnp.where(out_low, on[c * WINDOW:(c + 1) * WINDOW], on[(2 + c) * WINDOW:(3 + c) * WINDOW])
                o_ref[blk, (2 * j + c) * LANES:(2 * j + c + 1) * LANES] = pair.astype(BF16)


def _attention(za, sinks, B, S):
    nq = ATTN_STEP_BLOCKS if (S // WINDOW) % ATTN_STEP_BLOCKS == 0 else 1
    ns = S // (nq * WINDOW)
    kvblk = ATTN_WIDTH // (2 * KV_COLS)
    return pl.pallas_call(
        _attn_kernel,
        grid=(B, ns),
        in_specs=[
            pl.BlockSpec(memory_space=pltpu.SMEM),
            pl.BlockSpec((nq * WINDOW, ATTN_WIDTH), lambda b, n: (b * ns + n, 0)),
            pl.BlockSpec((nq * WINDOW, 2 * KV_COLS), lambda b, n: (b * ns + n, kvblk)),
            pl.BlockSpec((WINDOW, 2 * KV_COLS), lambda b, n: ((b * ns + n) * nq - jnp.minimum(n, 1), kvblk)),
        ],
        out_specs=pl.BlockSpec((nq * WINDOW, ATTN_WIDTH), lambda b, n: (b * ns + n, 0)),
        out_shape=jax.ShapeDtypeStruct((B * S, ATTN_WIDTH), BF16),
        compiler_params=pltpu.CompilerParams(dimension_semantics=("parallel", "arbitrary")),
        name="swa_attention",
    )(sinks, za, za, za)


def _hgrn_levels(C):
    out, s = [], C // 2
    while s >= 1:
        out.append(s)
        s //= 2
    return out


def _hgrn_constants(C):
    r = np.arange(C)[:, None]
    u = np.arange(C)[None, :]
    mats = [u <= r]
    masks = []
    for s in _hgrn_levels(C):
        mid = (r // (2 * s)) * 2 * s + s
        if s < HGRN_BROADCAST_MIN:
            mats.append(np.where(r >= mid, (u >= mid) & (u <= r), (u > r) & (u < mid)))
        masks.append(((r // (2 * s)) == (u // (2 * s))) & ((r & s) != 0) & ((u & s) == 0))
    masks.append(r == u)
    return (np.concatenate(mats, axis=0).astype(np.float32), np.stack(masks).astype(np.float32))


def _hgrn_head(hd, zr_ref, zf_ref, lb_ref, gain_ref, msum_ref, pmask_ref, o_ref, st_ref):
    C = zf_ref.shape[0]
    levels = _hgrn_levels(C)
    cs = slice(hd * REC_DIM, (hd + 1) * REC_DIM)
    zq = zr_ref[:, cs].astype(F32)
    v = zr_ref[:, REC_WIDTH + hd * REC_DIM:REC_WIDTH + (hd + 1) * REC_DIM]
    og = zr_ref[:, 2 * REC_WIDTH + hd * REC_DIM:2 * REC_WIDTH + (hd + 1) * REC_DIM].astype(F32)
    z = zf_ref[:, cs]
    lb = lb_ref[:, cs]
    qp = zq * _sigmoid(zq)
    a = jnp.exp(-jnp.abs(z))
    r = 1.0 / (1.0 + a)
    pos = z >= 0
    g = jnp.log2(lb + (1.0 - lb) * jnp.where(pos, r, a * r))
    k = (1.0 - lb) * jnp.where(pos, a * r, r)
    g_hi = g.astype(BF16)
    g_lo = (g - g_hi.astype(F32)).astype(BF16)
    x2 = _dot(msum_ref[...], jnp.concatenate([g_hi, g_lo], axis=1))
    sums = x2[:, :REC_DIM] + x2[:, REC_DIM:]
    b = sums[0:C]
    b_last = b[C - 1:C, :]
    e_b = jnp.exp2(b)
    st = st_ref[...]
    o = _dot_nt((qp * e_b).astype(BF16), st.astype(BF16))
    amat = pmask_ref[len(levels)] * _dot_nt(qp.astype(BF16), k.astype(BF16))
    n_matrix_levels = 0
    for li, s in enumerate(levels):
        if s >= HGRN_BROADCAST_MIN:
            ref_rows = [jnp.broadcast_to(b[j + s - 1:j + s, :], (2 * s, REC_DIM)) for j in range(0, C, 2 * s)]
            expo = -jnp.abs(b - jnp.concatenate(ref_rows, axis=0))
        else:
            n_matrix_levels += 1
            expo = sums[n_matrix_levels * C:(n_matrix_levels + 1) * C]
        e_l = jnp.exp2(expo)
        amat = amat + pmask_ref[li] * _dot_nt((qp * e_l).astype(BF16), (k * e_l).astype(BF16))
    o = o + _dot(amat.astype(BF16), v)
    st_ref[...] = st * e_b[C - 1:C, :] + _dot_tn(v, (k * jnp.exp2(b_last - b)).astype(BF16))
    og_act = og * _sigmoid(og)
    o_ref[:, cs] = (_rms(o, gain_ref[...]) * og_act).astype(BF16)


def _hgrn_kernel(zr_ref, zf_ref, lb_ref, gain_ref, msum_ref, pmask_ref, o_ref, st_ref):
    @pl.when(pl.program_id(1) == 0)
    def _():
        st_ref[...] = jnp.zeros_like(st_ref)

    for bb in range(zf_ref.shape[0]):
        for hd in range(REC_HEADS):
            _hgrn_head(hd, zr_ref.at[bb], zf_ref.at[bb], lb_ref, gain_ref, msum_ref, pmask_ref, o_ref.at[bb],
                       st_ref.at[bb * REC_HEADS + hd])


def _hgrn2(zr, zf, lb, gain, msum, pmask, B, S):
    C = min(REC_CHUNK, S)
    nb = HGRN_BATCH_ROWS if B % HGRN_BATCH_ROWS == 0 else 1
    blk = lambda b, c: (b, c, 0)
    out = pl.pallas_call(
        _hgrn_kernel,
        grid=(B // nb, S // C),
        in_specs=[
            pl.BlockSpec((nb, C, 3 * REC_WIDTH), blk),
            pl.BlockSpec((nb, C, REC_WIDTH), blk),
            pl.BlockSpec((1, REC_WIDTH), lambda b, c: (0, 0)),
            pl.BlockSpec((1, REC_DIM), lambda b, c: (0, 0)),
            pl.BlockSpec(msum.shape, lambda b, c: (0, 0)),
            pl.BlockSpec(pmask.shape, lambda b, c: (0, 0, 0)),
        ],
        out_specs=pl.BlockSpec((nb, C, REC_WIDTH), blk),
        out_shape=jax.ShapeDtypeStruct((B, S, REC_WIDTH), BF16),
        scratch_shapes=[pltpu.VMEM((nb * REC_HEADS, REC_DIM, REC_DIM), F32)],
        compiler_params=pltpu.CompilerParams(dimension_semantics=("parallel", "arbitrary")),
        name="hgrn2",
    )(zr.reshape(B, S, 3 * REC_WIDTH), zf.reshape(B, S, REC_WIDTH), lb, gain, msum, pmask)
    return out.reshape(B * S, REC_WIDTH)


def _mix_out_kernel(attn_ref, rec_ref, h_ref, wo_ref, gain_ref, wr_ref, tril_ref,
                    h1_ref, xn_ref, ri_ref, rw_ref, cnt_ref, carry_ref):
    @pl.when(pl.program_id(0) == 0)
    def _():
        carry_ref[...] = jnp.zeros_like(carry_ref)

    sub = tril_ref.shape[0]
    carry = carry_ref[...]
    for s in range(h_ref.shape[0] // sub):
        carry = _mix_out_rows(slice(s * sub, (s + 1) * sub), s * sub, carry, attn_ref, rec_ref, h_ref, wo_ref,
                              gain_ref, wr_ref, tril_ref, h1_ref, xn_ref, ri_ref, rw_ref)
    carry_ref[...] = carry
    cnt_ref[...] = jnp.broadcast_to(carry, cnt_ref.shape)


def _mix_out_rows(rows, first, carry, attn_ref, rec_ref, h_ref, wo_ref, gain_ref, wr_ref, tril_ref,
                  h1_ref, xn_ref, ri_ref, rw_ref):
    mixed = jnp.concatenate([attn_ref[rows, :], rec_ref[rows, :]], axis=1)
    h1 = h_ref[rows, :] + _dot(mixed, wo_ref[...])
    h1_ref[rows, :] = h1
    xn = _rms(h1, gain_ref[...])
    _store_row_tiles(xn_ref, xn, first=first)
    xh = xn.astype(BF16)
    xl = (xn - xh.astype(F32)).astype(BF16)
    both = _dot(xh, wr_ref[...])
    logits = both[:, :ROUTE_LANES] + both[:, ROUTE_LANES:] + _dot(xl, wr_ref[:, :ROUTE_LANES])
    lane = lax.broadcasted_iota(I32, logits.shape, 1)
    lanef = lane.astype(F32)
    neg = jnp.float32(-jnp.inf)
    big = jnp.float32(1e9)
    gl = jnp.where(lane < N_GROUPS, logits, neg)
    gmax = jnp.max(gl, axis=-1, keepdims=True)
    gidx = jnp.min(jnp.where(gl == gmax, lanef, big), axis=-1, keepdims=True)
    p_group = 1.0 / jnp.sum(jnp.where(lane < N_GROUPS, jnp.exp(logits - gmax), 0.0), axis=-1, keepdims=True)
    lo = N_GROUPS + gidx * EXPERTS_PER_GROUP
    el = jnp.where((lanef >= lo) & (lanef < lo + EXPERTS_PER_GROUP), logits, neg)
    t1 = jnp.max(el, axis=-1, keepdims=True)
    i1 = jnp.min(jnp.where(el == t1, lanef, big), axis=-1, keepdims=True)
    el2 = jnp.where(lanef == i1, neg, el)
    t2 = jnp.max(el2, axis=-1, keepdims=True)
    i2 = jnp.min(jnp.where(el2 == t2, lanef, big), axis=-1, keepdims=True)
    r21 = jnp.exp(t2 - t1)
    w1 = p_group / (1.0 + r21)
    w2 = w1 * r21
    e1 = i1 - N_GROUPS
    e2 = i2 - N_GROUPS
    oh1 = (lanef == e1).astype(F32)
    oh2 = (lanef == e2).astype(F32)
    oh = oh1 + oh2
    prefix = _dot(tril_ref[...], oh.astype(BF16)) + carry
    rank1 = jnp.sum(prefix * oh1, axis=-1, keepdims=True)
    rank2 = jnp.sum(prefix * oh2, axis=-1, keepdims=True)
    ri = jnp.where(lane == 0, e1, jnp.where(lane == 1, e2, jnp.where(lane == 2, rank1, jnp.where(lane == 3, rank2, 0.0))))
    ri_ref[rows, :] = ri.astype(I32)
    rw_ref[rows, :] = jnp.where(lane == 0, w1, jnp.where(lane == 1, w2, 0.0))
    return carry + jnp.sum(oh, axis=0, keepdims=True)


def _mix_out(attn, rec, h, wo, gain, wr, tril, layer):
    T = h.shape[0]
    tm = min(TOKEN_TILE, T)
    row = lambda i: (i, 0)
    fixed = lambda i: (0, 0)
    stacked = lambda i: (layer, 0, 0)
    return pl.pallas_call(
        _mix_out_kernel,
        grid=(T // tm,),
        in_specs=[
            pl.BlockSpec((tm, ATTN_WIDTH), row),
            pl.BlockSpec((tm, REC_WIDTH), row),
            pl.BlockSpec((tm, D_MODEL), row),
            pl.BlockSpec((None, ATTN_WIDTH + REC_WIDTH, D_MODEL), stacked),
            pl.BlockSpec((None, 1, D_MODEL), stacked),
            pl.BlockSpec((None, D_MODEL, 2 * ROUTE_LANES), stacked),
            pl.BlockSpec(tril.shape, fixed),
        ],
        out_specs=[
            pl.BlockSpec((tm, D_MODEL), row),
            pl.BlockSpec((tm * ROW_SUBLANES, LANES), row),
            pl.BlockSpec((tm, ROUTE_LANES), row),
            pl.BlockSpec((tm, ROUTE_LANES), row),
            pl.BlockSpec((8, ROUTE_LANES), fixed),
        ],
        out_shape=[
            jax.ShapeDtypeStruct((T, D_MODEL), F32),
            jax.ShapeDtypeStruct((T * ROW_SUBLANES, LANES), U32),
            jax.ShapeDtypeStruct((T, ROUTE_LANES), I32),
            jax.ShapeDtypeStruct((T, ROUTE_LANES), F32),
            jax.ShapeDtypeStruct((8, ROUTE_LANES), F32),
        ],
        scratch_shapes=[pltpu.VMEM((1, ROUTE_LANES), F32)],
        compiler_params=pltpu.CompilerParams(dimension_semantics=("arbitrary",)),
        name="mix_out_router",
    )(attn, rec, h, wo, gain, wr, tril)


def _row_move_loops(copy, tokens):
    def start(t, carry):
        copy(t, 0).start()
        copy(t, 1).start()
        return carry

    def wait(t, carry):
        copy(t, 0).wait()
        copy(t, 1).wait()
        return carry

    lax.fori_loop(0, tokens, start, 0, unroll=COPY_UNROLL)
    lax.fori_loop(0, tokens, wait, 0, unroll=COPY_UNROLL)


def _dispatch_kernel(dest_ref, bound_ref, x_ref, out_ref, zero_ref, sem, *, tokens):
    @pl.when(pl.program_id(0) == 0)
    def _():
        zero_ref[...] = jnp.zeros_like(zero_ref)

        def fill(e):
            return pltpu.make_async_copy(zero_ref, out_ref.at[pl.ds(bound_ref[e + 1] - EXPERT_BLOCK, EXPERT_BLOCK)], sem)

        for e in range(N_EXPERTS):
            pl.when(bound_ref[e + 1] > bound_ref[e])(lambda e=e: fill(e).start())
        for e in range(N_EXPERTS):
            pl.when(bound_ref[e + 1] > bound_ref[e])(lambda e=e: fill(e).wait())

        def tail(i):
            return pltpu.make_async_copy(zero_ref, out_ref.at[pl.ds(i * EXPERT_BLOCK, EXPERT_BLOCK)], sem)

        first_unused = bound_ref[N_EXPERTS] // EXPERT_BLOCK
        n_blocks = out_ref.shape[0] // EXPERT_BLOCK
        lax.fori_loop(first_unused, n_blocks, lambda i, c: (tail(i).start(), c)[1], 0)
        lax.fori_loop(first_unused, n_blocks, lambda i, c: (tail(i).wait(), c)[1], 0)

    base = 2 * pl.program_id(0) * tokens

    def copy(t, j):
        return pltpu.make_async_copy(x_ref.at[t], out_ref.at[dest_ref[base + 2 * t + j]], sem)

    _row_move_loops(copy, tokens)


def _dispatch(dest, bounds, x3, n_rows):
    tokens = min(COPY_TOKENS, x3.shape[0])
    tile = x3.shape[1:]
    return pl.pallas_call(
        functools.partial(_dispatch_kernel, tokens=tokens),
        grid_spec=pltpu.PrefetchScalarGridSpec(
            num_scalar_prefetch=2,
            grid=(x3.shape[0] // tokens,),
            in_specs=[pl.BlockSpec((tokens,) + tile, lambda i, d, b: (i, 0, 0))],
            out_specs=pl.BlockSpec(memory_space=pl.ANY),
            scratch_shapes=[pltpu.VMEM((EXPERT_BLOCK,) + tile, x3.dtype), pltpu.SemaphoreType.DMA],
        ),
        out_shape=jax.ShapeDtypeStruct((n_rows,) + tile, x3.dtype),
        compiler_params=pltpu.CompilerParams(dimension_semantics=("arbitrary",)),
        name="moe_dispatch_rows",
    )(dest, bounds, x3)


def _expert_kernel(be_ref, sched_ref, x_ref, wg_ref, wu_ref, wd_ref, y_ref,
                   wgf_ref, wuf_ref, wdf_ref, wgb_ref, wub_ref, wdb_ref, sem, *, layer):
    i = pl.program_id(0)
    nblk = pl.num_programs(0)
    used = i < sched_ref[0]
    buf = sched_ref[1 + i]
    next_expert = sched_ref[1 + nblk + i]
    first_of_run = (i == 0) | (be_ref[i] != be_ref[jnp.maximum(i - 1, 0)])

    def fetch(expert, slot):
        e = layer * N_EXPERTS + expert
        return [pltpu.make_async_copy(src.at[e], dst.at[slot], sem.at[slot, n])
                for n, (src, dst) in enumerate(((wg_ref, wgf_ref), (wu_ref, wuf_ref), (wd_ref, wdf_ref)))]

    @pl.when(i == 0)
    def _():
        for c in fetch(be_ref[0], buf):
            c.start()

    @pl.when(used & first_of_run & (next_expert >= 0))
    def _():
        for c in fetch(next_expert, 1 - buf):
            c.start()

    @pl.when(used & first_of_run)
    def _():
        for c in fetch(be_ref[i], buf):
            c.wait()
        wgb_ref[...] = wgf_ref[buf].astype(BF16)
        wub_ref[...] = wuf_ref[buf].astype(BF16)
        wdb_ref[...] = wdf_ref[buf].astype(BF16)

    @pl.when(used)
    def _():
        for s in range(EXPERT_BLOCK // EXPERT_SUB):
            x = _load_row_tiles(x_ref, EXPERT_SUB, first=s * EXPERT_SUB).astype(BF16)
            gate = _dot(x, wgb_ref[...])
            up = _dot(x, wub_ref[...])
            hidden = (gate * _sigmoid(gate) * up).astype(BF16)
            _store_row_tiles(y_ref, _dot(hidden, wdb_ref[...]), first=s * EXPERT_SUB)

    @pl.when(jnp.logical_not(used))
    def _():
        y_ref[...] = jnp.zeros_like(y_ref)


def _experts(block_expert, sched, xs, wg, wu, wd, layer):
    blk = EXPERT_BLOCK * ROW_SUBLANES
    nblk = xs.shape[0] // blk
    hbm = pl.BlockSpec(memory_space=pl.ANY)
    return pl.pallas_call(
        functools.partial(_expert_kernel, layer=layer),
        grid_spec=pltpu.PrefetchScalarGridSpec(
            num_scalar_prefetch=2,
            grid=(nblk,),
            in_specs=[pl.BlockSpec((blk, LANES), lambda i, be, sc: (jnp.minimum(i, sc[0] - 1), 0)), hbm, hbm, hbm],
            out_specs=pl.BlockSpec((blk, LANES), lambda i, be, sc: (i, 0)),
            scratch_shapes=[
                pltpu.VMEM((2, D_MODEL, EXPERT_FF), F32),
                pltpu.VMEM((2, D_MODEL, EXPERT_FF), F32),
                pltpu.VMEM((2, EXPERT_FF, D_MODEL), F32),
                pltpu.VMEM((D_MODEL, EXPERT_FF), BF16),
                pltpu.VMEM((D_MODEL, EXPERT_FF), BF16),
                pltpu.VMEM((EXPERT_FF, D_MODEL), BF16),
                pltpu.SemaphoreType.DMA((2, 3)),
            ],
        ),
        out_shape=jax.ShapeDtypeStruct(xs.shape, xs.dtype),
        compiler_params=pltpu.CompilerParams(dimension_semantics=("arbitrary",)),
        name="moe_experts",
    )(block_expert, sched, xs, wg, wu, wd)


def _ple_kernel(dest_ref, h1_ref, ys_ref, rw_ref, p_ref, wple_ref, pgain_ref, ggain_ref, wpg_ref, o_ref,
                ybuf_ref, sem):
    i = pl.program_id(0)
    tm = h1_ref.shape[0]
    slot = i % 2

    def row_copy(tile, buf, t, j):
        a = 2 * t + j
        dst = ybuf_ref.at[buf, pl.ds(pl.multiple_of(a * ROW_SUBLANES, ROW_SUBLANES), ROW_SUBLANES), :]
        return pltpu.make_async_copy(ys_ref.at[dest_ref[2 * tile * tm + a]], dst, sem.at[buf])

    def start_tile(tile, buf):
        def body(t, carry):
            row_copy(tile, buf, t, 0).start()
            row_copy(tile, buf, t, 1).start()
            return carry
        lax.fori_loop(0, tm, body, 0, unroll=COPY_UNROLL)

    def wait_tile(tile, buf):
        def body(t, carry):
            row_copy(tile, buf, t, 0).wait()
            row_copy(tile, buf, t, 1).wait()
            return carry
        lax.fori_loop(0, tm, body, 0, unroll=COPY_UNROLL)

    n_tiles = pl.num_programs(0)
    pl.when(i == 0)(lambda: start_tile(0, 0))
    wait_tile(i, slot)
    nxt = jnp.minimum(i + 1, n_tiles - 1)

    yg_ref = ybuf_ref.at[slot]
    sub = min(SUB_TILE, tm)
    for s in range(tm // sub):
        rows = slice(s * sub, (s + 1) * sub)
        rw = rw_ref[rows, :]
        y1 = _load_row_tiles(yg_ref, sub, first=2 * s * sub, every=2)
        y2 = _load_row_tiles(yg_ref, sub, first=2 * s * sub + 1, every=2)
        h2 = h1_ref[rows, :] + rw[:, 0:1] * y1 + rw[:, 1:2] * y2
        ple = _rms(_dot(p_ref[rows, :].astype(BF16), wple_ref[...]), pgain_ref[...])
        gate = _sigmoid(_dot(_rms(h2, ggain_ref[...]).astype(BF16), wpg_ref[...]))
        o_ref[rows, :] = h2 + ple * gate
        for t in range(s * sub, (s + 1) * sub):
            row_copy(nxt, 1 - slot, t, 0).start()
            row_copy(nxt, 1 - slot, t, 1).start()

    pl.when(i == n_tiles - 1)(lambda: wait_tile(i, 1 - slot))


def _ple(dest, h1, ys3, rw, p, ple_params, layer):
    T = h1.shape[0]
    tm = min(TOKEN_TILE, T)
    nt = T // tm
    row = lambda i, d: (i, 0)
    stacked = lambda i, d: (layer, 0, 0)
    return pl.pallas_call(
        _ple_kernel,
        grid_spec=pltpu.PrefetchScalarGridSpec(
            num_scalar_prefetch=1,
            grid=(nt,),
            in_specs=[
                pl.BlockSpec((tm, D_MODEL), row),
                pl.BlockSpec(memory_space=pl.ANY),
                pl.BlockSpec((tm, ROUTE_LANES), row),
                pl.BlockSpec((tm, PLE_DIM), lambda i, d: (layer * nt + i, 0)),
                pl.BlockSpec((None, PLE_DIM, D_MODEL), stacked),
                pl.BlockSpec((None, 1, D_MODEL), stacked),
                pl.BlockSpec((None, 1, D_MODEL), stacked),
                pl.BlockSpec((None, D_MODEL, D_MODEL), stacked),
            ],
            out_specs=pl.BlockSpec((tm, D_MODEL), row),
            scratch_shapes=[pltpu.VMEM((2, 2 * tm * ROW_SUBLANES, LANES), ys3.dtype), pltpu.SemaphoreType.DMA((2,))],
        ),
        out_shape=jax.ShapeDtypeStruct((T, D_MODEL), F32),
        compiler_params=pltpu.CompilerParams(dimension_semantics=("arbitrary",)),
        name="combine_ple",
    )(dest, h1, ys3, rw, p, *ple_params)


def _rope_tables(positions):
    inv_freq = ROPE_THETA ** (-jnp.arange(0, ROPE_DIM, 2, dtype=F32) / ROPE_DIM)
    ang = positions.astype(F32).reshape(-1, 1) * inv_freq
    cos, sin = jnp.cos(ang), jnp.sin(ang)
    T = ang.shape[0]
    rest = jnp.zeros((T, ATTN_HEAD_DIM - ROPE_DIM), F32)
    zero = jnp.zeros((T, ROPE_HALF), F32)
    rc = jnp.concatenate([cos, cos, rest + 1.0], axis=1)
    rs1 = jnp.concatenate([-sin, zero, rest], axis=1)
    rs2 = jnp.concatenate([zero, sin, rest], axis=1)
    reps = LANES // ATTN_HEAD_DIM
    return tuple(jnp.tile(t, (1, reps)) for t in (rc, rs1, rs2))


def kernel(x, p, positions, mix_norm, w_in, q_norm, k_norm, sinks, lb_logits, rec_norm, w_out, ffn_norm,
           w_router_group, w_router_expert, w_gate, w_up, w_down, w_ple, ple_norm, ple_gate_norm, w_ple_gate):
    B, S, D = x.shape
    depth = w_in.shape[0]
    T = B * S
    n_assign = 2 * T
    assert D == D_MODEL and S % WINDOW == 0 and T % min(TOKEN_TILE, T) == 0

    rc, rs1, rs2 = _rope_tables(positions)
    lb_sm = jax.nn.softmax(lb_logits.astype(F32), axis=0)
    lower_bounds = jnp.cumsum(lb_sm, axis=0) - lb_sm[0:1]

    def twice(cols):
        heads = cols.reshape(depth, D_MODEL, ATTN_KV_HEADS, 1, ATTN_HEAD_DIM)
        return jnp.broadcast_to(heads, (depth, D_MODEL, ATTN_KV_HEADS, KV_DUP, ATTN_HEAD_DIM)).reshape(
            depth, D_MODEL, KV_COLS)

    k0, v0, r0 = ATTN_WIDTH, ATTN_WIDTH + KV_WIDTH, ATTN_WIDTH + 2 * KV_WIDTH
    w_in_p = jnp.concatenate([w_in[:, :, :k0], twice(w_in[:, :, k0:v0]), twice(w_in[:, :, v0:r0]),
                              w_in[:, :, r0:r0 + REC_WIDTH], w_in[:, :, r0 + 2 * REC_WIDTH:],
                              w_in[:, :, r0 + REC_WIDTH:r0 + 2 * REC_WIDTH]], axis=2).astype(BF16)
    w_out_b = w_out.astype(BF16)
    wg_all = w_gate.reshape(depth * N_EXPERTS, D_MODEL, EXPERT_FF)
    wu_all = w_up.reshape(depth * N_EXPERTS, D_MODEL, EXPERT_FF)
    wd_all = w_down.reshape(depth * N_EXPERTS, EXPERT_FF, D_MODEL)
    w_ple_b = w_ple.astype(BF16)
    w_pg_b = w_ple_gate.astype(BF16)
    w_r = jnp.concatenate([w_router_group, w_router_expert,
                           jnp.zeros((depth, D_MODEL, ROUTE_LANES - N_GROUPS - N_EXPERTS), F32)], axis=2)
    w_r_hi = w_r.astype(BF16)
    w_r2 = jnp.concatenate([w_r_hi, (w_r - w_r_hi.astype(F32)).astype(BF16)], axis=2)
    qk_gain = jnp.concatenate([jnp.tile(q_norm, (1, ATTN_HEADS)) * (ATTN_HEAD_DIM ** -0.5),
                               jnp.tile(k_norm, (1, ATTN_KV_HEADS * KV_DUP))], axis=1)
    seg_id = np.arange(QK_WIDTH) // ATTN_HEAD_DIM
    seg = jnp.asarray(seg_id[:, None] == seg_id[None, :], BF16)
    msum_np, pmask_np = _hgrn_constants(min(REC_CHUNK, S))
    msum = jnp.asarray(msum_np, BF16)
    pmask = jnp.asarray(pmask_np, F32)
    sub = min(SUB_TILE, T)
    tril = jnp.asarray(np.tril(np.ones((sub, sub), np.float32), -1), BF16)

    n_rows = n_assign + N_EXPERTS * EXPERT_BLOCK
    nblk = n_rows // EXPERT_BLOCK
    eids = jnp.arange(N_EXPERTS, dtype=I32)
    p2 = p.reshape(depth * T, PLE_DIM)

    mix_in_params = (mix_norm[:, None, :], w_in_p, rc, rs1, rs2, qk_gain[:, None, :], seg)
    ple_params = (w_ple_b, ple_norm[:, None, :], ple_gate_norm[:, None, :], w_pg_b)
    ffn_gain = ffn_norm[:, None, :]

    h = x.reshape(T, D)
    for l in range(depth):
        za, zr, zf = _mix_in(h, mix_in_params, l)
        attn = _attention(za, sinks[l], B, S)
        rec = _hgrn2(zr, zf, lower_bounds[l][None], rec_norm[l][None], msum, pmask, B, S)
        h1, xn, ri, rw, cnt = _mix_out(attn, rec, h, w_out_b, ffn_gain, w_r2, tril, l)
        counts = cnt[0, :N_EXPERTS].astype(I32)
        padded = ((counts + EXPERT_BLOCK - 1) // EXPERT_BLOCK) * EXPERT_BLOCK
        pad_end = jnp.cumsum(padded)
        pad_start = pad_end - padded
        expert = ri[:, 0:2]
        dest = ri[:, 2:4] + jnp.sum(jnp.where(expert[:, :, None] == eids, pad_start, 0), axis=-1)
        dest = dest.reshape(n_assign).astype(I32)
        n_used = (pad_end[-1] // EXPERT_BLOCK).astype(I32).reshape(1)
        blk_start = jnp.arange(nblk, dtype=I32) * EXPERT_BLOCK
        block_expert = jnp.minimum(jnp.sum(pad_end[None, :] <= blk_start[:, None], axis=1), N_EXPERTS - 1).astype(I32)
        bounds = jnp.concatenate([jnp.zeros((1,), I32), pad_end.astype(I32)])
        xs = _dispatch(dest, bounds, xn.reshape(T, ROW_SUBLANES, LANES), n_rows)
        run_start = jnp.concatenate([jnp.ones((1,), I32), (block_expert[1:] != block_expert[:-1]).astype(I32)])
        run_buf = (jnp.cumsum(run_start) - 1) % 2
        later = jnp.where(padded > 0, eids, N_EXPERTS)
        next_used = jnp.concatenate([lax.cummin(later[::-1])[::-1][1:], jnp.full((1,), N_EXPERTS, I32)])
        next_used = jnp.where(next_used < N_EXPERTS, next_used, -1)
        sched = jnp.concatenate([n_used, run_buf, next_used[block_expert]]).astype(I32)
        ys = _experts(block_expert, sched, xs.reshape(n_rows * ROW_SUBLANES, LANES), wg_all, wu_all, wd_all, l)
        h = _ple(dest, h1, ys.reshape(n_rows, ROW_SUBLANES, LANES), rw, p2, ple_params, l)
    return h.reshape(B, S, D)
```

```python
import functools

import jax
import jax.numpy as jnp
import numpy as np
from jax import lax
from jax.experimental import pallas as pl
from jax.experimental.pallas import tpu as pltpu

F32 = jnp.float32
BF16 = jnp.bfloat16
I32 = jnp.int32
U32 = jnp.uint32

D_MODEL = 1024
ATTN_HEADS = 8
ATTN_KV_HEADS = 2
ATTN_HEAD_DIM = 64
ATTN_GROUP = ATTN_HEADS // ATTN_KV_HEADS
ATTN_WIDTH = ATTN_HEADS * ATTN_HEAD_DIM
KV_WIDTH = ATTN_KV_HEADS * ATTN_HEAD_DIM
KV_DUP = 2
KV_COLS = KV_DUP * KV_WIDTH
QK_WIDTH = ATTN_WIDTH + KV_COLS
QKV_WIDTH = ATTN_WIDTH + 2 * KV_COLS
WINDOW = 128
ATTN_STEP_BLOCKS = 8
ROPE_THETA = 500000.0
ROPE_DIM = ATTN_HEAD_DIM // 4
ROPE_HALF = ROPE_DIM // 2
REC_HEADS = 4
REC_DIM = 128
REC_WIDTH = REC_HEADS * REC_DIM
REC_CHUNK = 128
HGRN_BROADCAST_MIN = 8
HGRN_BATCH_ROWS = 4
IN_WIDTH = QKV_WIDTH + 4 * REC_WIDTH
N_GROUPS = 4
EXPERTS_PER_GROUP = 8
N_EXPERTS = N_GROUPS * EXPERTS_PER_GROUP
EXPERT_FF = 512
PLE_DIM = 256
RMS_EPS = 1e-6
MASK_VALUE = -1e30
LANES = 128
ROUTE_LANES = LANES
ROW_WORDS = D_MODEL // 2
ROW_SUBLANES = ROW_WORDS // LANES
HIGH_HALF = np.uint32(0xFFFF0000)

TOKEN_TILE = 512
PROJ_TILE = 1024
SUB_TILE = 256
EXPERT_BLOCK = 512
EXPERT_SUB = 512
COPY_TOKENS = 512
COPY_UNROLL = 8


def _dot(a, b):
    return jnp.dot(a, b, preferred_element_type=F32)


def _dot_nt(a, b):
    return lax.dot_general(a, b, (((1,), (1,)), ((), ())), preferred_element_type=F32)


def _dot_tn(a, b):
    return lax.dot_general(a, b, (((0,), (0,)), ((), ())), preferred_element_type=F32)


def _store_row_tiles(ref, x, first=0):
    m = x.shape[0]
    lo = lax.bitcast_convert_type(x[:, :ROW_WORDS].astype(BF16).astype(F32), U32) >> 16
    hi = lax.bitcast_convert_type(x[:, ROW_WORDS:].astype(BF16).astype(F32), U32) & HIGH_HALF
    words = lo | hi
    for c in range(ROW_SUBLANES):
        ref[pl.ds(first * ROW_SUBLANES + c, m, stride=ROW_SUBLANES), :] = words[:, c * LANES:(c + 1) * LANES]


def _load_row_tiles(ref, m, first=0, every=1):
    words = jnp.concatenate(
        [ref[pl.ds(first * ROW_SUBLANES + c, m, stride=every * ROW_SUBLANES), :] for c in range(ROW_SUBLANES)], axis=1)
    lo = lax.bitcast_convert_type(words << 16, F32)
    hi = lax.bitcast_convert_type(words & HIGH_HALF, F32)
    return jnp.concatenate([lo, hi], axis=1)


def _sigmoid(x):
    return 1.0 / (1.0 + jnp.exp(-x))


def _rms(x, gain):
    ms = jnp.mean(x * x, axis=-1, keepdims=True)
    return x * lax.rsqrt(ms + RMS_EPS) * gain


def _mix_in_kernel(h_ref, gain_ref, w_ref, rc_ref, rs1_ref, rs2_ref, qkg_ref, seg_ref,
                   za_ref, zr_ref, zf_ref):
    xn = _rms(h_ref[...], gain_ref[...]).astype(BF16)
    z_a = _dot(xn, w_ref[:, 0:QKV_WIDTH])
    qk = z_a[:, 0:QK_WIDTH]
    seg = _dot((qk * qk).astype(BF16), seg_ref[...]) * (1.0 / ATTN_HEAD_DIM)
    qkn = qk * lax.rsqrt(seg + RMS_EPS) * qkg_ref[...]
    rc, rs1, rs2 = rc_ref[...], rs1_ref[...], rs2_ref[...]
    for c in range(QK_WIDTH // LANES):
        col = qkn[:, c * LANES:(c + 1) * LANES]
        rot = col * rc + pltpu.roll(col, LANES - ROPE_HALF, 1) * rs1 + pltpu.roll(col, ROPE_HALF, 1) * rs2
        za_ref[:, c * LANES:(c + 1) * LANES] = rot.astype(BF16)
    za_ref[:, QK_WIDTH:QKV_WIDTH] = z_a[:, QK_WIDTH:QKV_WIDTH].astype(BF16)
    zr_ref[...] = _dot(xn, w_ref[:, QKV_WIDTH:QKV_WIDTH + 3 * REC_WIDTH]).astype(BF16)
    zf_ref[...] = _dot(xn, w_ref[:, QKV_WIDTH + 3 * REC_WIDTH:IN_WIDTH])


def _mix_in(h, mix_in_params, layer):
    T = h.shape[0]
    tm = min(PROJ_TILE, T)
    row = lambda i: (i, 0)
    stacked = lambda i: (layer, 0, 0)
    return pl.pallas_call(
        _mix_in_kernel,
        grid=(T // tm,),
        in_specs=[
            pl.BlockSpec((tm, D_MODEL), row),
            pl.BlockSpec((None, 1, D_MODEL), stacked),
            pl.BlockSpec((None, D_MODEL, IN_WIDTH), stacked),
            pl.BlockSpec((tm, LANES), row),
            pl.BlockSpec((tm, LANES), row),
            pl.BlockSpec((tm, LANES), row),
            pl.BlockSpec((None, 1, QK_WIDTH), stacked),
            pl.BlockSpec((QK_WIDTH, QK_WIDTH), lambda i: (0, 0)),
        ],
        out_specs=[
            pl.BlockSpec((tm, QKV_WIDTH), row),
            pl.BlockSpec((tm, 3 * REC_WIDTH), row),
            pl.BlockSpec((tm, REC_WIDTH), row),
        ],
        out_shape=[
            jax.ShapeDtypeStruct((T, QKV_WIDTH), BF16),
            jax.ShapeDtypeStruct((T, 3 * REC_WIDTH), BF16),
            jax.ShapeDtypeStruct((T, REC_WIDTH), F32),
        ],
        compiler_params=pltpu.CompilerParams(dimension_semantics=("parallel",)),
        name="mix_in",
    )(h, *mix_in_params)


def _attn_kernel(sink_ref, q_ref, kvc_ref, kvp_ref, o_ref):
    n = pl.program_id(1)
    rows = lax.broadcasted_iota(I32, (ATTN_GROUP * WINDOW, 2 * WINDOW), 0)
    cols = lax.broadcasted_iota(I32, (ATTN_GROUP * WINDOW, 2 * WINDOW), 1)
    qi = rows & (WINDOW - 1)
    in_window = (cols > qi) & (cols <= qi + WINDOW)
    grp = lax.broadcasted_iota(I32, (ATTN_GROUP * WINDOW, 1), 0) // WINDOW
    low_lanes = lax.broadcasted_iota(I32, (1, LANES), 1) < ATTN_HEAD_DIM
    keep_low = jnp.where(low_lanes, 1.0, 0.0).astype(BF16)
    keep_high = jnp.where(low_lanes, 0.0, 1.0).astype(BF16)
    out_low = lax.broadcasted_iota(I32, (WINDOW, LANES), 1) < ATTN_HEAD_DIM
    ones = jnp.ones((2 * WINDOW, LANES), BF16)
    for qb in range(q_ref.shape[0] // WINDOW):
        blk = slice(qb * WINDOW, (qb + 1) * WINDOW)
        kvc = kvc_ref[blk, :]
        kvp = kvp_ref[...] if qb == 0 else kvc_ref[(qb - 1) * WINDOW:qb * WINDOW, :]
        valid = in_window & ((cols >= WINDOW) | (n > 0)) if qb == 0 else in_window
        for j in range(ATTN_KV_HEADS):
            kcols = slice(j * LANES, (j + 1) * LANES)
            vcols = slice(KV_COLS + j * LANES, KV_COLS + (j + 1) * LANES)
            kk = jnp.concatenate([kvp[:, kcols], kvc[:, kcols]], axis=0)
            vv = jnp.concatenate([kvp[:, vcols], kvc[:, vcols]], axis=0)
            pairs = jnp.concatenate([q_ref[blk, (2 * j) * LANES:(2 * j + 1) * LANES],
                                     q_ref[blk, (2 * j + 1) * LANES:(2 * j + 2) * LANES]], axis=0)
            heads = [4 * j, 4 * j + 2, 4 * j + 1, 4 * j + 3]
            s = jnp.concatenate([_dot_nt(pairs, kk * keep_low), _dot_nt(pairs, kk * keep_high)], axis=0)
            s = jnp.where(valid, s, MASK_VALUE)
            sink = jnp.zeros((ATTN_GROUP * WINDOW, 1), F32)
            for g, hh in enumerate(heads):
                sink = jnp.where(grp == g, sink_ref[hh], sink)
            m = jnp.maximum(jnp.max(s, axis=-1, keepdims=True), sink)
            e = jnp.exp(s - m).astype(BF16)
            o2 = _dot(e, jnp.concatenate([vv, ones], axis=1))
            on = o2[:, :LANES] * (1.0 / (o2[:, LANES:] + jnp.exp(sink - m)))
            for c in range(2):
                pair = jnp.where(out_low, on[c * WINDOW:(c + 1) * WINDOW], on[(2 + c) * WINDOW:(3 + c) * WINDOW])
                o_ref[blk, (2 * j + c) * LANES:(2 * j + c + 1) * LANES] = pair.astype(BF16)


def _attention(za, sinks, B, S):
    nq = ATTN_STEP_BLOCKS if (S // WINDOW) % ATTN_STEP_BLOCKS == 0 else 1
    ns = S // (nq * WINDOW)
    kvblk = ATTN_WIDTH // (2 * KV_COLS)
    return pl.pallas_call(
        _attn_kernel,
        grid=(B, ns),
        in_specs=[
            pl.BlockSpec(memory_space=pltpu.SMEM),
            pl.BlockSpec((nq * WINDOW, ATTN_WIDTH), lambda b, n: (b * ns + n, 0)),
            pl.BlockSpec((nq * WINDOW, 2 * KV_COLS), lambda b, n: (b * ns + n, kvblk)),
            pl.BlockSpec((WINDOW, 2 * KV_COLS), lambda b, n: ((b * ns + n) * nq - jnp.minimum(n, 1), kvblk)),
        ],
        out_specs=pl.BlockSpec((nq * WINDOW, ATTN_WIDTH), lambda b, n: (b * ns + n, 0)),
        out_shape=jax.ShapeDtypeStruct((B * S, ATTN_WIDTH), BF16),
        compiler_params=pltpu.CompilerParams(dimension_semantics=("parallel", "arbitrary")),
        name="swa_attention",
    )(sinks, za, za, za)


def _hgrn_levels(C):
    out, s = [], C // 2
    while s >= 1:
        out.append(s)
        s //= 2
    return out


def _hgrn_constants(C):
    r = np.arange(C)[:, None]
    u = np.arange(C)[None, :]
    mats = [u <= r]
    masks = []
    for s in _hgrn_levels(C):
        mid = (r // (2 * s)) * 2 * s + s
        if s < HGRN_BROADCAST_MIN:
            mats.append(np.where(r >= mid, (u >= mid) & (u <= r), (u > r) & (u < mid)))
        masks.append(((r // (2 * s)) == (u // (2 * s))) & ((r & s) != 0) & ((u & s) == 0))
    masks.append(r == u)
    return (np.concatenate(mats, axis=0).astype(np.float32), np.stack(masks).astype(np.float32))


def _hgrn_head(hd, zr_ref, zf_ref, lb_ref, gain_ref, msum_ref, pmask_ref, o_ref, st_ref):
    C = zf_ref.shape[0]
    levels = _hgrn_levels(C)
    cs = slice(hd * REC_DIM, (hd + 1) * REC_DIM)
    zq = zr_ref[:, cs].astype(F32)
    v = zr_ref[:, REC_WIDTH + hd * REC_DIM:REC_WIDTH + (hd + 1) * REC_DIM]
    og = zr_ref[:, 2 * REC_WIDTH + hd * REC_DIM:2 * REC_WIDTH + (hd + 1) * REC_DIM].astype(F32)
    z = zf_ref[:, cs]
    lb = lb_ref[:, cs]
    qp = zq * _sigmoid(zq)
    a = jnp.exp(-jnp.abs(z))
    r = 1.0 / (1.0 + a)
    pos = z >= 0
    g = jnp.log2(lb + (1.0 - lb) * jnp.where(pos, r, a * r))
    k = (1.0 - lb) * jnp.where(pos, a * r, r)
    g_hi = g.astype(BF16)
    g_lo = (g - g_hi.astype(F32)).astype(BF16)
    x2 = _dot(msum_ref[...], jnp.concatenate([g_hi, g_lo], axis=1))
    sums = x2[:, :REC_DIM] + x2[:, REC_DIM:]
    b = sums[0:C]
    b_last = b[C - 1:C, :]
    e_b = jnp.exp2(b)
    st = st_ref[...]
    o = _dot_nt((qp * e_b).astype(BF16), st.astype(BF16))
    amat = pmask_ref[len(levels)] * _dot_nt(qp.astype(BF16), k.astype(BF16))
    n_matrix_levels = 0
    for li, s in enumerate(levels):
        if s >= HGRN_BROADCAST_MIN:
            ref_rows = [jnp.broadcast_to(b[j + s - 1:j + s, :], (2 * s, REC_DIM)) for j in range(0, C, 2 * s)]
            expo = -jnp.abs(b - jnp.concatenate(ref_rows, axis=0))
        else:
            n_matrix_levels += 1
            expo = sums[n_matrix_levels * C:(n_matrix_levels + 1) * C]
        e_l = jnp.exp2(expo)
        amat = amat + pmask_ref[li] * _dot_nt((qp * e_l).astype(BF16), (k * e_l).astype(BF16))
    o = o + _dot(amat.astype(BF16), v)
    st_ref[...] = st * e_b[C - 1:C, :] + _dot_tn(v, (k * jnp.exp2(b_last - b)).astype(BF16))
    og_act = og * _sigmoid(og)
    o_ref[:, cs] = (_rms(o, gain_ref[...]) * og_act).astype(BF16)


def _hgrn_kernel(zr_ref, zf_ref, lb_ref, gain_ref, msum_ref, pmask_ref, o_ref, st_ref):
    @pl.when(pl.program_id(1) == 0)
    def _():
        st_ref[...] = jnp.zeros_like(st_ref)

    for bb in range(zf_ref.shape[0]):
        for hd in range(REC_HEADS):
            _hgrn_head(hd, zr_ref.at[bb], zf_ref.at[bb], lb_ref, gain_ref, msum_ref, pmask_ref, o_ref.at[bb],
                       st_ref.at[bb * REC_HEADS + hd])


def _hgrn2(zr, zf, lb, gain, msum, pmask, B, S):
    C = min(REC_CHUNK, S)
    nb = HGRN_BATCH_ROWS if B % HGRN_BATCH_ROWS == 0 else 1
    blk = lambda b, c: (b, c, 0)
    out = pl.pallas_call(
        _hgrn_kernel,
        grid=(B // nb, S // C),
        in_specs=[
            pl.BlockSpec((nb, C, 3 * REC_WIDTH), blk),
            pl.BlockSpec((nb, C, REC_WIDTH), blk),
            pl.BlockSpec((1, REC_WIDTH), lambda b, c: (0, 0)),
            pl.BlockSpec((1, REC_DIM), lambda b, c: (0, 0)),
            pl.BlockSpec(msum.shape, lambda b, c: (0, 0)),
            pl.BlockSpec(pmask.shape, lambda b, c: (0, 0, 0)),
        ],
        out_specs=pl.BlockSpec((nb, C, REC_WIDTH), blk),
        out_shape=jax.ShapeDtypeStruct((B, S, REC_WIDTH), BF16),
        scratch_shapes=[pltpu.VMEM((nb * REC_HEADS, REC_DIM, REC_DIM), F32)],
        compiler_params=pltpu.CompilerParams(dimension_semantics=("parallel", "arbitrary")),
        name="hgrn2",
    )(zr.reshape(B, S, 3 * REC_WIDTH), zf.reshape(B, S, REC_WIDTH), lb, gain, msum, pmask)
    return out.reshape(B * S, REC_WIDTH)


def _mix_out_kernel(attn_ref, rec_ref, h_ref, wo_ref, gain_ref, wr_ref, tril_ref,
                    h1_ref, xn_ref, ri_ref, rw_ref, cnt_ref, carry_ref):
    @pl.when(pl.program_id(0) == 0)
    def _():
        carry_ref[...] = jnp.zeros_like(carry_ref)

    sub = tril_ref.shape[0]
    carry = carry_ref[...]
    for s in range(h_ref.shape[0] // sub):
        carry = _mix_out_rows(slice(s * sub, (s + 1) * sub), s * sub, carry, attn_ref, rec_ref, h_ref, wo_ref,
                              gain_ref, wr_ref, tril_ref, h1_ref, xn_ref, ri_ref, rw_ref)
    carry_ref[...] = carry
    cnt_ref[...] = jnp.broadcast_to(carry, cnt_ref.shape)


def _mix_out_rows(rows, first, carry, attn_ref, rec_ref, h_ref, wo_ref, gain_ref, wr_ref, tril_ref,
                  h1_ref, xn_ref, ri_ref, rw_ref):
    mixed = jnp.concatenate([attn_ref[rows, :], rec_ref[rows, :]], axis=1)
    h1 = h_ref[rows, :] + _dot(mixed, wo_ref[...])
    h1_ref[rows, :] = h1
    xn = _rms(h1, gain_ref[...])
    _store_row_tiles(xn_ref, xn, first=first)
    xh = xn.astype(BF16)
    xl = (xn - xh.astype(F32)).astype(BF16)
    both = _dot(xh, wr_ref[...])
    logits = both[:, :ROUTE_LANES] + both[:, ROUTE_LANES:] + _dot(xl, wr_ref[:, :ROUTE_LANES])
    lane = lax.broadcasted_iota(I32, logits.shape, 1)
    lanef = lane.astype(F32)
    neg = jnp.float32(-jnp.inf)
    big = jnp.float32(1e9)
    gl = jnp.where(lane < N_GROUPS, logits, neg)
    gmax = jnp.max(gl, axis=-1, keepdims=True)
    gidx = jnp.min(jnp.where(gl == gmax, lanef, big), axis=-1, keepdims=True)
    p_group = 1.0 / jnp.sum(jnp.where(lane < N_GROUPS, jnp.exp(logits - gmax), 0.0), axis=-1, keepdims=True)
    lo = N_GROUPS + gidx * EXPERTS_PER_GROUP
    el = jnp.where((lanef >= lo) & (lanef < lo + EXPERTS_PER_GROUP), logits, neg)
    t1 = jnp.max(el, axis=-1, keepdims=True)
    i1 = jnp.min(jnp.where(el == t1, lanef, big), axis=-1, keepdims=True)
    el2 = jnp.where(lanef == i1, neg, el)
    t2 = jnp.max(el2, axis=-1, keepdims=True)
    i2 = jnp.min(jnp.where(el2 == t2, lanef, big), axis=-1, keepdims=True)
    r21 = jnp.exp(t2 - t1)
    w1 = p_group / (1.0 + r21)
    w2 = w1 * r21
    e1 = i1 - N_GROUPS
    e2 = i2 - N_GROUPS
    oh1 = (lanef == e1).astype(F32)
    oh2 = (lanef == e2).astype(F32)
    oh = oh1 + oh2
    prefix = _dot(tril_ref[...], oh.astype(BF16)) + carry
    rank1 = jnp.sum(prefix * oh1, axis=-1, keepdims=True)
    rank2 = jnp.sum(prefix * oh2, axis=-1, keepdims=True)
    ri = jnp.where(lane == 0, e1, jnp.where(lane == 1, e2, jnp.where(lane == 2, rank1, jnp.where(lane == 3, rank2, 0.0))))
    ri_ref[rows, :] = ri.astype(I32)
    rw_ref[rows, :] = jnp.where(lane == 0, w1, jnp.where(lane == 1, w2, 0.0))
    return carry + jnp.sum(oh, axis=0, keepdims=True)


def _mix_out(attn, rec, h, wo, gain, wr, tril, layer):
    T = h.shape[0]
    tm = min(PROJ_TILE, T)
    row = lambda i: (i, 0)
    fixed = lambda i: (0, 0)
    stacked = lambda i: (layer, 0, 0)
    return pl.pallas_call(
        _mix_out_kernel,
        grid=(T // tm,),
        in_specs=[
            pl.BlockSpec((tm, ATTN_WIDTH), row),
            pl.BlockSpec((tm, REC_WIDTH), row),
            pl.BlockSpec((tm, D_MODEL), row),
            pl.BlockSpec((None, ATTN_WIDTH + REC_WIDTH, D_MODEL), stacked),
            pl.BlockSpec((None, 1, D_MODEL), stacked),
            pl.BlockSpec((None, D_MODEL, 2 * ROUTE_LANES), stacked),
            pl.BlockSpec(tril.shape, fixed),
        ],
        out_specs=[
            pl.BlockSpec((tm, D_MODEL), row),
            pl.BlockSpec((tm * ROW_SUBLANES, LANES), row),
            pl.BlockSpec((tm, ROUTE_LANES), row),
            pl.BlockSpec((tm, ROUTE_LANES), row),
            pl.BlockSpec((8, ROUTE_LANES), fixed),
        ],
        out_shape=[
            jax.ShapeDtypeStruct((T, D_MODEL), F32),
            jax.ShapeDtypeStruct((T * ROW_SUBLANES, LANES), U32),
            jax.ShapeDtypeStruct((T, ROUTE_LANES), I32),
            jax.ShapeDtypeStruct((T, ROUTE_LANES), F32),
            jax.ShapeDtypeStruct((8, ROUTE_LANES), F32),
        ],
        scratch_shapes=[pltpu.VMEM((1, ROUTE_LANES), F32)],
        compiler_params=pltpu.CompilerParams(dimension_semantics=("arbitrary",)),
        name="mix_out_router",
    )(attn, rec, h, wo, gain, wr, tril)


def _row_move_loops(copy, tokens):
    def start(t, carry):
        copy(t, 0).start()
        copy(t, 1).start()
        return carry

    def wait(t, carry):
        copy(t, 0).wait()
        copy(t, 1).wait()
        return carry

    lax.fori_loop(0, tokens, start, 0, unroll=COPY_UNROLL)
    lax.fori_loop(0, tokens, wait, 0, unroll=COPY_UNROLL)


def _dispatch_kernel(dest_ref, bound_ref, x_ref, out_ref, zero_ref, sem, *, tokens):
    @pl.when(pl.program_id(0) == 0)
    def _():
        zero_ref[...] = jnp.zeros_like(zero_ref)

        def fill(e):
            return pltpu.make_async_copy(zero_ref, out_ref.at[pl.ds(bound_ref[e + 1] - EXPERT_BLOCK, EXPERT_BLOCK)], sem)

        for e in range(N_EXPERTS):
            pl.when(bound_ref[e + 1] > bound_ref[e])(lambda e=e: fill(e).start())
        for e in range(N_EXPERTS):
            pl.when(bound_ref[e + 1] > bound_ref[e])(lambda e=e: fill(e).wait())

        def tail(i):
            return pltpu.make_async_copy(zero_ref, out_ref.at[pl.ds(i * EXPERT_BLOCK, EXPERT_BLOCK)], sem)

        first_unused = bound_ref[N_EXPERTS] // EXPERT_BLOCK
        n_blocks = out_ref.shape[0] // EXPERT_BLOCK
        lax.fori_loop(first_unused, n_blocks, lambda i, c: (tail(i).start(), c)[1], 0)
        lax.fori_loop(first_unused, n_blocks, lambda i, c: (tail(i).wait(), c)[1], 0)

    base = 2 * pl.program_id(0) * tokens

    def copy(t, j):
        return pltpu.make_async_copy(x_ref.at[t], out_ref.at[dest_ref[base + 2 * t + j]], sem)

    _row_move_loops(copy, tokens)


def _dispatch(dest, bounds, x3, n_rows):
    tokens = min(COPY_TOKENS, x3.shape[0])
    tile = x3.shape[1:]
    return pl.pallas_call(
        functools.partial(_dispatch_kernel, tokens=tokens),
        grid_spec=pltpu.PrefetchScalarGridSpec(
            num_scalar_prefetch=2,
            grid=(x3.shape[0] // tokens,),
            in_specs=[pl.BlockSpec((tokens,) + tile, lambda i, d, b: (i, 0, 0))],
            out_specs=pl.BlockSpec(memory_space=pl.ANY),
            scratch_shapes=[pltpu.VMEM((EXPERT_BLOCK,) + tile, x3.dtype), pltpu.SemaphoreType.DMA],
        ),
        out_shape=jax.ShapeDtypeStruct((n_rows,) + tile, x3.dtype),
        compiler_params=pltpu.CompilerParams(dimension_semantics=("arbitrary",)),
        name="moe_dispatch_rows",
    )(dest, bounds, x3)


def _expert_kernel(be_ref, sched_ref, x_ref, wg_ref, wu_ref, wd_ref, y_ref,
                   wgf_ref, wuf_ref, wdf_ref, wgb_ref, wub_ref, wdb_ref, sem, *, layer):
    i = pl.program_id(0)
    nblk = pl.num_programs(0)
    used = i < sched_ref[0]
    buf = sched_ref[1 + i]
    next_expert = sched_ref[1 + nblk + i]
    first_of_run = (i == 0) | (be_ref[i] != be_ref[jnp.maximum(i - 1, 0)])

    def fetch(expert, slot):
        e = layer * N_EXPERTS + expert
        return [pltpu.make_async_copy(src.at[e], dst.at[slot], sem.at[slot, n])
                for n, (src, dst) in enumerate(((wg_ref, wgf_ref), (wu_ref, wuf_ref), (wd_ref, wdf_ref)))]

    @pl.when(i == 0)
    def _():
        for c in fetch(be_ref[0], buf):
            c.start()

    @pl.when(used & first_of_run & (next_expert >= 0))
    def _():
        for c in fetch(next_expert, 1 - buf):
            c.start()

    @pl.when(used & first_of_run)
    def _():
        for c in fetch(be_ref[i], buf):
            c.wait()
        wgb_ref[...] = wgf_ref[buf].astype(BF16)
        wub_ref[...] = wuf_ref[buf].astype(BF16)
        wdb_ref[...] = wdf_ref[buf].astype(BF16)

    @pl.when(used)
    def _():
        for s in range(EXPERT_BLOCK // EXPERT_SUB):
            x = _load_row_tiles(x_ref, EXPERT_SUB, first=s * EXPERT_SUB).astype(BF16)
            gate = _dot(x, wgb_ref[...])
            up = _dot(x, wub_ref[...])
            hidden = (gate * _sigmoid(gate) * up).astype(BF16)
            _store_row_tiles(y_ref, _dot(hidden, wdb_ref[...]), first=s * EXPERT_SUB)

    @pl.when(jnp.logical_not(used))
    def _():
        y_ref[...] = jnp.zeros_like(y_ref)


def _experts(block_expert, sched, xs, wg, wu, wd, layer):
    blk = EXPERT_BLOCK * ROW_SUBLANES
    nblk = xs.shape[0] // blk
    hbm = pl.BlockSpec(memory_space=pl.ANY)
    return pl.pallas_call(
        functools.partial(_expert_kernel, layer=layer),
        grid_spec=pltpu.PrefetchScalarGridSpec(
            num_scalar_prefetch=2,
            grid=(nblk,),
            in_specs=[pl.BlockSpec((blk, LANES), lambda i, be, sc: (jnp.minimum(i, sc[0] - 1), 0)), hbm, hbm, hbm],
            out_specs=pl.BlockSpec((blk, LANES), lambda i, be, sc: (i, 0)),
            scratch_shapes=[
                pltpu.VMEM((2, D_MODEL, EXPERT_FF), F32),
                pltpu.VMEM((2, D_MODEL, EXPERT_FF), F32),
                pltpu.VMEM((2, EXPERT_FF, D_MODEL), F32),
                pltpu.VMEM((D_MODEL, EXPERT_FF), BF16),
                pltpu.VMEM((D_MODEL, EXPERT_FF), BF16),
                pltpu.VMEM((EXPERT_FF, D_MODEL), BF16),
                pltpu.SemaphoreType.DMA((2, 3)),
            ],
        ),
        out_shape=jax.ShapeDtypeStruct(xs.shape, xs.dtype),
        compiler_params=pltpu.CompilerParams(dimension_semantics=("arbitrary",)),
        name="moe_experts",
    )(block_expert, sched, xs, wg, wu, wd)


def _ple_kernel(dest_ref, h1_ref, ys_ref, rw_ref, p_ref, wple_ref, pgain_ref, ggain_ref, wpg_ref, o_ref,
                ybuf_ref, sem):
    i = pl.program_id(0)
    tm = h1_ref.shape[0]
    slot = i % 2

    def row_copy(tile, buf, t, j):
        a = 2 * t + j
        dst = ybuf_ref.at[buf, pl.ds(pl.multiple_of(a * ROW_SUBLANES, ROW_SUBLANES), ROW_SUBLANES), :]
        return pltpu.make_async_copy(ys_ref.at[dest_ref[2 * tile * tm + a]], dst, sem.at[buf])

    def start_tile(tile, buf):
        def body(t, carry):
            row_copy(tile, buf, t, 0).start()
            row_copy(tile, buf, t, 1).start()
            return carry
        lax.fori_loop(0, tm, body, 0, unroll=COPY_UNROLL)

    def wait_tile(tile, buf):
        def body(t, carry):
            row_copy(tile, buf, t, 0).wait()
            row_copy(tile, buf, t, 1).wait()
            return carry
        lax.fori_loop(0, tm, body, 0, unroll=COPY_UNROLL)

    n_tiles = pl.num_programs(0)
    pl.when(i == 0)(lambda: start_tile(0, 0))
    wait_tile(i, slot)
    nxt = jnp.minimum(i + 1, n_tiles - 1)

    yg_ref = ybuf_ref.at[slot]
    sub = min(SUB_TILE, tm)
    for s in range(tm // sub):
        rows = slice(s * sub, (s + 1) * sub)
        rw = rw_ref[rows, :]
        y1 = _load_row_tiles(yg_ref, sub, first=2 * s * sub, every=2)
        y2 = _load_row_tiles(yg_ref, sub, first=2 * s * sub + 1, every=2)
        h2 = h1_ref[rows, :] + rw[:, 0:1] * y1 + rw[:, 1:2] * y2
        ple = _rms(_dot(p_ref[rows, :].astype(BF16), wple_ref[...]), pgain_ref[...])
        gate = _sigmoid(_dot(_rms(h2, ggain_ref[...]).astype(BF16), wpg_ref[...]))
        o_ref[rows, :] = h2 + ple * gate
        for t in range(s * sub, (s + 1) * sub):
            row_copy(nxt, 1 - slot, t, 0).start()
            row_copy(nxt, 1 - slot, t, 1).start()

    pl.when(i == n_tiles - 1)(lambda: wait_tile(i, 1 - slot))


def _ple(dest, h1, ys3, rw, p, ple_params, layer):
    T = h1.shape[0]
    tm = min(TOKEN_TILE, T)
    nt = T // tm
    row = lambda i, d: (i, 0)
    stacked = lambda i, d: (layer, 0, 0)
    return pl.pallas_call(
        _ple_kernel,
        grid_spec=pltpu.PrefetchScalarGridSpec(
            num_scalar_prefetch=1,
            grid=(nt,),
            in_specs=[
                pl.BlockSpec((tm, D_MODEL), row),
                pl.BlockSpec(memory_space=pl.ANY),
                pl.BlockSpec((tm, ROUTE_LANES), row),
                pl.BlockSpec((tm, PLE_DIM), lambda i, d: (layer * nt + i, 0)),
                pl.BlockSpec((None, PLE_DIM, D_MODEL), stacked),
                pl.BlockSpec((None, 1, D_MODEL), stacked),
                pl.BlockSpec((None, 1, D_MODEL), stacked),
                pl.BlockSpec((None, D_MODEL, D_MODEL), stacked),
            ],
            out_specs=pl.BlockSpec((tm, D_MODEL), row),
            scratch_shapes=[pltpu.VMEM((2, 2 * tm * ROW_SUBLANES, LANES), ys3.dtype), pltpu.SemaphoreType.DMA((2,))],
        ),
        out_shape=jax.ShapeDtypeStruct((T, D_MODEL), F32),
        compiler_params=pltpu.CompilerParams(dimension_semantics=("arbitrary",)),
        name="combine_ple",
    )(dest, h1, ys3, rw, p, *ple_params)


def _rope_tables(positions):
    inv_freq = ROPE_THETA ** (-jnp.arange(0, ROPE_DIM, 2, dtype=F32) / ROPE_DIM)
    ang = positions.astype(F32).reshape(-1, 1) * inv_freq
    cos, sin = jnp.cos(ang), jnp.sin(ang)
    T = ang.shape[0]
    rest = jnp.zeros((T, ATTN_HEAD_DIM - ROPE_DIM), F32)
    zero = jnp.zeros((T, ROPE_HALF), F32)
    rc = jnp.concatenate([cos, cos, rest + 1.0], axis=1)
    rs1 = jnp.concatenate([-sin, zero, rest], axis=1)
    rs2 = jnp.concatenate([zero, sin, rest], axis=1)
    reps = LANES // ATTN_HEAD_DIM
    return tuple(jnp.tile(t, (1, reps)) for t in (rc, rs1, rs2))


def kernel(x, p, positions, mix_norm, w_in, q_norm, k_norm, sinks, lb_logits, rec_norm, w_out, ffn_norm,
           w_router_group, w_router_expert, w_gate, w_up, w_down, w_ple, ple_norm, ple_gate_norm, w_ple_gate):
    B, S, D = x.shape
    depth = w_in.shape[0]
    T = B * S
    n_assign = 2 * T
    assert D == D_MODEL and S % WINDOW == 0 and T % min(PROJ_TILE, T) == 0 and T % min(TOKEN_TILE, T) == 0

    rc, rs1, rs2 = _rope_tables(positions)
    lb_sm = jax.nn.softmax(lb_logits.astype(F32), axis=0)
    lower_bounds = jnp.cumsum(lb_sm, axis=0) - lb_sm[0:1]

    def twice(cols):
        heads = cols.reshape(depth, D_MODEL, ATTN_KV_HEADS, 1, ATTN_HEAD_DIM)
        return jnp.broadcast_to(heads, (depth, D_MODEL, ATTN_KV_HEADS, KV_DUP, ATTN_HEAD_DIM)).reshape(
            depth, D_MODEL, KV_COLS)

    k0, v0, r0 = ATTN_WIDTH, ATTN_WIDTH + KV_WIDTH, ATTN_WIDTH + 2 * KV_WIDTH
    w_in_p = jnp.concatenate([w_in[:, :, :k0], twice(w_in[:, :, k0:v0]), twice(w_in[:, :, v0:r0]),
                              w_in[:, :, r0:r0 + REC_WIDTH], w_in[:, :, r0 + 2 * REC_WIDTH:],
                              w_in[:, :, r0 + REC_WIDTH:r0 + 2 * REC_WIDTH]], axis=2).astype(BF16)
    w_out_b = w_out.astype(BF16)
    wg_all = w_gate.reshape(depth * N_EXPERTS, D_MODEL, EXPERT_FF)
    wu_all = w_up.reshape(depth * N_EXPERTS, D_MODEL, EXPERT_FF)
    wd_all = w_down.reshape(depth * N_EXPERTS, EXPERT_FF, D_MODEL)
    w_ple_b = w_ple.astype(BF16)
    w_pg_b = w_ple_gate.astype(BF16)
    w_r = jnp.concatenate([w_router_group, w_router_expert,
                           jnp.zeros((depth, D_MODEL, ROUTE_LANES - N_GROUPS - N_EXPERTS), F32)], axis=2)
    w_r_hi = w_r.astype(BF16)
    w_r2 = jnp.concatenate([w_r_hi, (w_r - w_r_hi.astype(F32)).astype(BF16)], axis=2)
    qk_gain = jnp.concatenate([jnp.tile(q_norm, (1, ATTN_HEADS)) * (ATTN_HEAD_DIM ** -0.5),
                               jnp.tile(k_norm, (1, ATTN_KV_HEADS * KV_DUP))], axis=1)
    seg_id = np.arange(QK_WIDTH) // ATTN_HEAD_DIM
    seg = jnp.asarray(seg_id[:, None] == seg_id[None, :], BF16)
    msum_np, pmask_np = _hgrn_constants(min(REC_CHUNK, S))
    msum = jnp.asarray(msum_np, BF16)
    pmask = jnp.asarray(pmask_np, F32)
    sub = min(SUB_TILE, T)
    tril = jnp.asarray(np.tril(np.ones((sub, sub), np.float32), -1), BF16)

    n_rows = n_assign + N_EXPERTS * EXPERT_BLOCK
    nblk = n_rows // EXPERT_BLOCK
    eids = jnp.arange(N_EXPERTS, dtype=I32)
    p2 = p.reshape(depth * T, PLE_DIM)

    mix_in_params = (mix_norm[:, None, :], w_in_p, rc, rs1, rs2, qk_gain[:, None, :], seg)
    ple_params = (w_ple_b, ple_norm[:, None, :], ple_gate_norm[:, None, :], w_pg_b)
    ffn_gain = ffn_norm[:, None, :]

    h = x.reshape(T, D)
    for l in range(depth):
        za, zr, zf = _mix_in(h, mix_in_params, l)
        attn = _attention(za, sinks[l], B, S)
        rec = _hgrn2(zr, zf, lower_bounds[l][None], rec_norm[l][None], msum, pmask, B, S)
        h1, xn, ri, rw, cnt = _mix_out(attn, rec, h, w_out_b, ffn_gain, w_r2, tril, l)
        counts = cnt[0, :N_EXPERTS].astype(I32)
        padded = ((counts + EXPERT_BLOCK - 1) // EXPERT_BLOCK) * EXPERT_BLOCK
        pad_end = jnp.cumsum(padded)
        pad_start = pad_end - padded
        expert = ri[:, 0:2]
        dest = ri[:, 2:4] + jnp.sum(jnp.where(expert[:, :, None] == eids, pad_start, 0), axis=-1)
        dest = dest.reshape(n_assign).astype(I32)
        n_used = (pad_end[-1] // EXPERT_BLOCK).astype(I32).reshape(1)
        blk_start = jnp.arange(nblk, dtype=I32) * EXPERT_BLOCK
        block_expert = jnp.minimum(jnp.sum(pad_end[None, :] <= blk_start[:, None], axis=1), N_EXPERTS - 1).astype(I32)
        bounds = jnp.concatenate([jnp.zeros((1,), I32), pad_end.astype(I32)])
        xs = _dispatch(dest, bounds, xn.reshape(T, ROW_SUBLANES, LANES), n_rows)
        run_start = jnp.concatenate([jnp.ones((1,), I32), (block_expert[1:] != block_expert[:-1]).astype(I32)])
        run_buf = (jnp.cumsum(run_start) - 1) % 2
        later = jnp.where(padded > 0, eids, N_EXPERTS)
        next_used = jnp.concatenate([lax.cummin(later[::-1])[::-1][1:], jnp.full((1,), N_EXPERTS, I32)])
        next_used = jnp.where(next_used < N_EXPERTS, next_used, -1)
        sched = jnp.concatenate([n_used, run_buf, next_used[block_expert]]).astype(I32)
        ys = _experts(block_expert, sched, xs.reshape(n_rows * ROW_SUBLANES, LANES), wg_all, wu_all, wd_all, l)
        h = _ple(dest, h1, ys.reshape(n_rows, ROW_SUBLANES, LANES), rw, p2, ple_params, l)
    return h.reshape(B, S, D)
```
